```python
import jax, jax.numpy as jnp
from jax import lax
import numpy as np

D_MODEL = 1024
BATCH = 8
SEQ = 2048
DEPTH = 1
DEC_BATCH = 128
DEC_SEQ = 4
PAST_LEN = 16384
PAGE_SIZE = 128

HEAD_DIM = 64
N_HEADS = D_MODEL // HEAD_DIM
D_RWKV = N_HEADS * HEAD_DIM
D_CONV = D_MODEL
CONV_W = 3
LORA_W = 64
LORA_A = 64
RMS_EPS = 1e-6
GN_EPS = 64e-5
CONV_SIZES = [D_CONV, D_CONV, D_CONV, D_CONV]
RWKV_SIZES = [D_RWKV, D_RWKV, D_RWKV, LORA_W, LORA_A, D_RWKV]
N_CONV_COLS = sum(CONV_SIZES)
N_RWKV_COLS = sum(RWKV_SIZES)
N_IN = N_CONV_COLS + N_RWKV_COLS + 2 * D_MODEL

kernel_name = "hybrid_shortconv_rwkv7_gated_merge_step"


def _split(p, sizes):
    idx = np.cumsum(sizes)[:-1].tolist()
    return jnp.split(p, idx, axis=-1)


def _rmsnorm(x, g):
    xf = x.astype(jnp.float32)
    y = xf * lax.rsqrt(jnp.mean(xf * xf, axis=-1, keepdims=True) + RMS_EPS)
    return (y * g.astype(jnp.float32)).astype(x.dtype)


def _rwkv7_scan(S0, r, k, v, decay, kk, a):
    def step(S, inp):
        r_t, k_t, v_t, d_t, kk_t, a_t = inp
        sa = jnp.einsum('bhvk,bhk->bhv', S, -kk_t)
        S = (S * d_t[:, :, None, :] + sa[..., None] * (kk_t * a_t)[:, :, None, :]
             + v_t[..., None] * k_t[:, :, None, :])
        y = jnp.einsum('bhvk,bhk->bhv', S, r_t)
        return S, y
    xs = tuple(jnp.moveaxis(t.astype(jnp.float32), 1, 0) for t in (r, k, v, decay, kk, a))
    S_fin, ys = lax.scan(step, S0.astype(jnp.float32), xs)
    return S_fin, jnp.moveaxis(ys, 0, 1)


def _layer(x, conv_buf, h_last, S0, norm_g, w_in, conv_w, mu_shift, w0, w_up, a0, a_up,
           k_k, k_a, r_k, ln_w, ln_b, w_out_c, w_out_r, w_o):
    b, s, _ = x.shape
    h = _rmsnorm(x, norm_g)
    p = jnp.einsum('bsd,dn->bsn', h, w_in)
    p_conv, p_rwkv, p_gate = _split(p, [N_CONV_COLS, N_RWKV_COLS, 2 * D_MODEL])

    xin, bg, cg, zc = _split(p_conv, CONV_SIZES)
    u = cg * xin
    u_ext = jnp.concatenate([conv_buf.astype(u.dtype), u], axis=1)
    conv = sum(conv_w[j] * u_ext[:, j:j + s] for j in range(CONV_W))
    y_c = bg * conv * jax.nn.silu(zc)
    new_conv = u_ext[:, -(CONV_W - 1):]

    w_r = w_in[:, N_CONV_COLS:N_CONV_COLS + N_RWKV_COLS]
    p_last = jnp.einsum('bd,dn->bn', h_last.astype(h.dtype), w_r)
    prev = jnp.concatenate([p_last[:, None], p_rwkv[:, :-1]], axis=1)
    xm = p_rwkv + (prev - p_rwkv) * mu_shift
    r, k, v, wd, ad, zr = _split(xm, RWKV_SIZES)
    w_logit = w0 + jnp.einsum('bsr,rc->bsc', jnp.tanh(wd), w_up)
    w_log = (-jax.nn.softplus(-w_logit.astype(jnp.float32)) - 0.5)
    decay = jnp.exp(-jnp.exp(w_log))
    a = jax.nn.sigmoid((a0 + jnp.einsum('bsr,rc->bsc', ad, a_up)).astype(jnp.float32))
    hs = (b, s, N_HEADS, HEAD_DIM)
    r_h = r.reshape(hs).astype(jnp.float32)
    v_h = v.reshape(hs).astype(jnp.float32)
    k_f = k.astype(jnp.float32)
    kk = (k_f * k_k).reshape(hs)
    kk = kk / jnp.maximum(jnp.sqrt(jnp.sum(kk * kk, axis=-1, keepdims=True)), 1e-12)
    k_h = (k_f * (1.0 + (a - 1.0) * k_a)).reshape(hs)
    a_h = a.reshape(hs)
    S_fin, ys = _rwkv7_scan(S0, r_h, k_h, v_h, decay.reshape(hs), kk, a_h)
    mu = jnp.mean(ys, axis=-1, keepdims=True)
    var = jnp.mean(jnp.square(ys - mu), axis=-1, keepdims=True)
    yn = ((ys - mu) * lax.rsqrt(var + GN_EPS)).reshape(b, s, D_RWKV) * ln_w + ln_b
    bonus = (jnp.sum(r_h * k_h * r_k, axis=-1, keepdims=True) * v_h).reshape(b, s, D_RWKV)
    y_r = ((yn + bonus) * jax.nn.silu(zr.astype(jnp.float32))).astype(x.dtype)

    gc, gr = _split(p_gate, [D_MODEL, D_MODEL])
    pc = jnp.einsum('bsc,cd->bsd', y_c, w_out_c)
    pr = jnp.einsum('bsc,cd->bsd', y_r, w_out_r)
    m = jax.nn.sigmoid(gc) * pc + jax.nn.sigmoid(gr) * pr
    out = jnp.einsum('bsd,de->bse', m, w_o)
    return x + out, new_conv, h[:, -1], S_fin.astype(S0.dtype)


def setup_inputs(seed: int = 0) -> dict:
    key = jax.random.key(seed)
    ks = jax.random.split(key, 24)
    f = jnp.float32
    n = lambda i, shape, sc: jax.random.normal(ks[i], shape, f) * sc
    L = DEPTH
    return {
        "x_prompt": n(0, (BATCH, SEQ, D_MODEL), 1.0),
        "x_sample": n(1, (DEC_BATCH, DEC_SEQ, D_MODEL), 1.0),
        "state_conv": n(2, (L, DEC_BATCH, CONV_W - 1, D_CONV), 0.5),
        "state_shift": n(3, (L, DEC_BATCH, D_MODEL), 1.0),
        "state_rwkv": n(4, (L, DEC_BATCH, N_HEADS, HEAD_DIM, HEAD_DIM), 0.3),
        "norm_g": 1.0 + n(5, (L, D_MODEL), 0.05),
        "w_in": n(6, (L, D_MODEL, N_IN), D_MODEL ** -0.5),
        "conv_w": n(7, (L, CONV_W, D_CONV), CONV_W ** -0.5),
        "mu_shift": jax.random.uniform(ks[8], (L, N_RWKV_COLS), f),
        "w0": jax.random.uniform(ks[9], (L, D_RWKV), f, -4.0, 1.0),
        "w_up": n(10, (L, LORA_W, D_RWKV), 0.5 * LORA_W ** -0.5),
        "a0": n(11, (L, D_RWKV), 0.5),
        "a_up": n(12, (L, LORA_A, D_RWKV), 0.5 * LORA_A ** -0.5),
        "k_k": 0.85 + n(13, (L, D_RWKV), 0.05),
        "k_a": 1.0 + n(14, (L, D_RWKV), 0.05),
        "r_k": n(15, (L, N_HEADS, HEAD_DIM), 0.1),
        "ln_w": 1.0 + n(16, (L, D_RWKV), 0.05),
        "ln_b": n(17, (L, D_RWKV), 0.02),
        "w_out_c": n(18, (L, D_CONV, D_MODEL), D_CONV ** -0.5),
        "w_out_r": n(19, (L, D_RWKV, D_MODEL), D_RWKV ** -0.5),
        "w_o": n(20, (L, D_MODEL, D_MODEL), D_MODEL ** -0.5),
        "final_g": 1.0 + n(21, (D_MODEL,), 0.05),
    }


def reference(x_prompt, x_sample, state_conv, state_shift, state_rwkv, norm_g, w_in, conv_w,
              mu_shift, w0, w_up, a0, a_up, k_k, k_a, r_k, ln_w, ln_b, w_out_c, w_out_r, w_o,
              final_g):
    bp = x_prompt.shape[0]
    xp, xs = x_prompt, x_sample
    cp, sp, rp, cs, ss, rs = [], [], [], [], [], []
    for l in range(DEPTH):
        wl = (norm_g[l], w_in[l], conv_w[l], mu_shift[l], w0[l], w_up[l], a0[l], a_up[l],
              k_k[l], k_a[l], r_k[l], ln_w[l], ln_b[l], w_out_c[l], w_out_r[l], w_o[l])
        conv0 = jnp.zeros((bp, CONV_W - 1, D_CONV), x_prompt.dtype)
        shift0 = jnp.zeros((bp, D_MODEL), x_prompt.dtype)
        S00 = jnp.zeros((bp, N_HEADS, HEAD_DIM, HEAD_DIM), state_rwkv.dtype)
        xp, c1, s1, r1 = _layer(xp, conv0, shift0, S00, *wl)
        xs, c2, s2, r2 = _layer(xs, state_conv[l], state_shift[l], state_rwkv[l], *wl)
        cp.append(c1); sp.append(s1); rp.append(r1)
        cs.append(c2); ss.append(s2); rs.append(r2)
    y_prompt = _rmsnorm(xp, final_g)
    y_sample = _rmsnorm(xs, final_g)
    return (y_prompt, y_sample, jnp.stack(cp), jnp.stack(sp), jnp.stack(rp),
            jnp.stack(cs), jnp.stack(ss), jnp.stack(rs))
```

```python
import functools

import jax
import jax.numpy as jnp
from jax import lax
from jax.experimental import pallas as pl
from jax.experimental.pallas import tpu as pltpu

F32 = jnp.float32
BF16 = jnp.bfloat16

D_MODEL = 1024
HEAD_DIM = 64
N_HEADS = D_MODEL // HEAD_DIM
LORA = 64
RMS_EPS = 1e-6
GN_EPS = 64e-5
LANES = 128
SUBLANES = 8
PAIR = 2 * HEAD_DIM
N_PAIRS = D_MODEL // PAIR
CHUNK = 64
N_RW = 4 * D_MODEL + 2 * LORA
VMEM_LIMIT = 56 * 1024 * 1024


def _dot(a, b):
    return jnp.dot(a.astype(BF16), b.astype(BF16), preferred_element_type=F32)


def _dot_nt(a, b):
    return lax.dot_general(a.astype(BF16), b.astype(BF16), (((1,), (1,)), ((), ())),
                           preferred_element_type=F32)


def _dot_tn(a, b):
    return lax.dot_general(a.astype(BF16), b.astype(BF16), (((0,), (0,)), ((), ())),
                           preferred_element_type=F32)


def _split3(x):
    hi = x.astype(BF16)
    r1 = x - hi.astype(F32)
    mid = r1.astype(BF16)
    lo = (r1 - mid.astype(F32)).astype(BF16)
    return hi, mid, lo


def _dot_exact_lhs(m_bf16, x):
    hi, mid, lo = _split3(x)
    return (jnp.dot(m_bf16, hi, preferred_element_type=F32)
            + jnp.dot(m_bf16, mid, preferred_element_type=F32)
            + jnp.dot(m_bf16, lo, preferred_element_type=F32))


def _dot_exact_rhs(x, m_bf16):
    hi, mid, lo = _split3(x)
    return (jnp.dot(hi, m_bf16, preferred_element_type=F32)
            + jnp.dot(mid, m_bf16, preferred_element_type=F32)
            + jnp.dot(lo, m_bf16, preferred_element_type=F32))


def _rms(x, g):
    return x * lax.rsqrt(jnp.mean(x * x, axis=-1, keepdims=True) + RMS_EPS) * g


def _sigmoid(x):
    return 1.0 / (1.0 + jnp.exp(-x))


def _silu(x):
    return x * _sigmoid(x)


def _softplus(x):
    return jnp.maximum(x, 0.0) + jnp.log(1.0 + jnp.exp(-jnp.abs(x)))


def _iota(shape, dim):
    return lax.broadcasted_iota(jnp.int32, shape, dim)


def _shift_rows(x, n, carry8):
    rolled = pltpu.roll(x, n, 0)
    head = jnp.where(_iota((SUBLANES, x.shape[1]), 0) < n, pltpu.roll(carry8, n, 0),
                     rolled[:SUBLANES])
    return jnp.concatenate([head, rolled[SUBLANES:]], axis=0)


def _seg_sum(x, e_ref, et_ref):
    s = _dot_exact_rhs(x, e_ref[...])
    return _dot_exact_rhs(s, et_ref[...])


def _block_masks(rows, blk):
    r = _iota((rows, rows), 0)
    c = _iota((rows, rows), 1)
    same = (r // blk) == (c // blk)
    return (jnp.where(same & (c <= r), 1.0, 0.0).astype(BF16),
            jnp.where(same, 1.0, 0.0).astype(BF16))


def _rwkv_pointwise(xm, w0, lora_w, a0, k_k, k_a, e_ref, et_ref):
    r = xm[:, 0:D_MODEL]
    k = xm[:, D_MODEL:2 * D_MODEL]
    v = xm[:, 2 * D_MODEL:3 * D_MODEL]
    zr = xm[:, 3 * D_MODEL:4 * D_MODEL]
    da = xm[:, 4 * D_MODEL:]
    lane = _iota(da.shape, 1)
    lora_in = jnp.where(lane < LORA, jnp.tanh(da), da)
    lo = _dot(lora_in, lora_w)
    w_logit = w0 + lo[:, :D_MODEL]
    w_log = -_softplus(-w_logit) - 0.5
    logd = -jnp.exp(w_log)
    a = _sigmoid(a0 + lo[:, D_MODEL:])
    kkr = k * k_k
    nrm = jnp.sqrt(_seg_sum(kkr * kkr, e_ref, et_ref))
    kk = kkr / jnp.maximum(nrm, 1e-12)
    k_h = k * (1.0 + (a - 1.0) * k_a)
    b = kk * a
    return r, k_h, v, kk, b, logd, zr


def _rwkv_post(ys, r, k_h, v, zr, r_k, ln_w, ln_b, e_ref, et_ref):
    inv_n = 1.0 / HEAD_DIM
    mu = _seg_sum(ys, e_ref, et_ref) * inv_n
    yc = ys - mu
    var = _seg_sum(yc * yc, e_ref, et_ref) * inv_n
    yn = yc * lax.rsqrt(var + GN_EPS) * ln_w + ln_b
    bonus = _seg_sum(r * k_h * r_k, e_ref, et_ref) * v
    return (yn + bonus) * _silu(zr)


def _scan_operands(r, k_h, v, kk, b, logd, cum, cum_c):
    enc = jnp.exp(-cum)
    e_c = jnp.exp(cum_c - cum)
    khat = kk * jnp.exp(cum - logd)
    rhat = r * jnp.exp(cum)
    btil = b * enc
    ktil = k_h * enc
    bchk = -(b * e_c)
    kchk = k_h * e_c
    g_c = jnp.exp(cum_c)
    return khat, rhat, btil, ktil, bchk, kchk, g_c


def _pair_chunk(kh, rh, bt, kt, bc, kc, v, g_c, s_pair, m_bd, m_sl, m_l):
    two = lambda x: jnp.concatenate([x, x], axis=0)
    fold = lambda x: x[:CHUNK] + x[CHUNK:]
    expand = lambda x: jnp.where(m_bd, two(x), 0.0)
    g = _dot_nt(jnp.concatenate([kh, rh], axis=0),
                jnp.concatenate([expand(bt), expand(kt)], axis=0))
    gk, gr = g[:CHUNK], g[CHUNK:]
    n = jnp.where(m_sl, -two(gk[:, :PAIR]), 0.0)
    m_ak = jnp.where(m_sl, two(gk[:, PAIR:]), 0.0)
    m_rb = jnp.where(m_l, -two(gr[:, :PAIR]), 0.0)
    m_rk = jnp.where(m_l, two(gr[:, PAIR:]), 0.0)
    vx = expand(v)
    x = jnp.concatenate([expand(kh), _dot(m_ak, vx)], axis=1)
    p = n
    for j in range(6):
        x = x + _dot(p, x)
        if j < 5:
            p = _dot(p, p)
    t1 = _dot(m_rb, x)
    t2 = _dot(m_rk, vx)
    y_w = rh + fold(t1[:, :PAIR])
    y_c = fold(t1[:, PAIR:] + t2)
    w_tok = fold(x[:, :PAIR])
    uv_tok = fold(x[:, PAIR:])
    p_t = jnp.where(m_bd, _dot_tn(w_tok, bc), 0.0)
    q_t = jnp.where(m_bd, _dot_tn(jnp.concatenate([uv_tok, v], axis=0),
                                  jnp.concatenate([bc, kc], axis=0)), 0.0)
    y = _dot_nt(y_w, s_pair) + y_c
    s_new = s_pair * g_c + _dot(s_pair, p_t) + q_t
    return y, s_new


def _rwkv_prompt_kernel(x_ref, g_ref, w_ref, mu_ref, w0_ref, lw_ref, a0_ref, kk_ref, ka_ref,
                        rk_ref, lnw_ref, lnb_ref, e_ref, et_ref,
                        yr_ref, sfin_ref,
                        carry_ref, st_ref, ops_ref, ys_ref, *, tm):
    s_idx = pl.program_id(1)

    @pl.when(s_idx == 0)
    def _():
        carry_ref[...] = jnp.zeros_like(carry_ref)
        st_ref[...] = jnp.zeros_like(st_ref)

    h = _rms(x_ref[0], g_ref[...]).astype(BF16)
    p = jnp.dot(h, w_ref[...], preferred_element_type=F32)
    prev = _shift_rows(p, 1, carry_ref[...])
    carry_ref[...] = p[tm - SUBLANES:, :]
    xm = p + (prev - p) * mu_ref[...]
    r, k_h, v, kk, b, logd, zr = _rwkv_pointwise(xm, w0_ref[...], lw_ref[...], a0_ref[...],
                                                 kk_ref[...], ka_ref[...], e_ref, et_ref)
    m_low, m_all = _block_masks(tm, CHUNK)
    cum = _dot_exact_lhs(m_low, logd)
    cum_c = _dot_exact_lhs(m_all, logd)
    ops = _scan_operands(r, k_h, v, kk, b, logd, cum, cum_c) + (v,)
    for i, arr in enumerate(ops):
        for hp in range(N_PAIRS):
            ops_ref[i, hp] = arr[:, hp * PAIR:(hp + 1) * PAIR]

    rr = _iota((PAIR, PAIR), 0)
    cc = _iota((PAIR, PAIR), 1)
    m_bd = (rr // HEAD_DIM) == (cc // HEAD_DIM)
    m_sl = m_bd & ((cc % HEAD_DIM) < (rr % HEAD_DIM))
    m_l = m_bd & ((cc % HEAD_DIM) <= (rr % HEAD_DIM))

    def pair_body(hp, carry):
        s_pair = st_ref[hp]
        for c in range(tm // CHUNK):
            rows = pl.ds(c * CHUNK, CHUNK)
            kh, rh, bt, kt, bc, kc, gc, vv = [ops_ref[i, hp, rows, :] for i in range(8)]
            y, s_pair = _pair_chunk(kh, rh, bt, kt, bc, kc, vv, gc[0:1, :], s_pair,
                                    m_bd, m_sl, m_l)
            ys_ref[hp, rows, :] = y
        st_ref[hp] = s_pair
        return carry

    lax.fori_loop(0, N_PAIRS, pair_body, 0)

    ys = jnp.concatenate([ys_ref[hp] for hp in range(N_PAIRS)], axis=1)
    y_r = _rwkv_post(ys, r, k_h, v, zr, rk_ref[...], lnw_ref[...], lnb_ref[...], e_ref, et_ref)
    yr_ref[0] = y_r.astype(BF16)

    @pl.when(s_idx == pl.num_programs(1) - 1)
    def _():
        for hp in range(N_PAIRS):
            sp = st_ref[hp]
            sfin_ref[0, 2 * hp] = sp[:HEAD_DIM, :HEAD_DIM]
            sfin_ref[0, 2 * hp + 1] = sp[HEAD_DIM:, HEAD_DIM:]


def _const_spec(shape):
    nd = len(shape)
    return pl.BlockSpec(shape, lambda *_: (0,) * nd)


def _rwkv_prompt(x, wts, tm=256):
    bsz, seq, _ = x.shape
    grid = (bsz, seq // tm)
    row = lambda a: a.reshape(1, -1)
    consts = [row(wts["norm_g"]), wts["w_rw"], row(wts["mu_rw"]), row(wts["w0"]), wts["lora_w"],
              row(wts["a0"]), row(wts["k_k"]), row(wts["k_a"]), row(wts["r_k"]),
              row(wts["ln_w"]), row(wts["ln_b"]), wts["e"], wts["et"]]
    kern = functools.partial(_rwkv_prompt_kernel, tm=tm)
    return pl.pallas_call(
        kern,
        grid=grid,
        in_specs=[pl.BlockSpec((1, tm, D_MODEL), lambda b, s: (b, s, 0))]
        + [_const_spec(c.shape) for c in consts],
        out_specs=[pl.BlockSpec((1, tm, D_MODEL), lambda b, s: (b, s, 0)),
                   pl.BlockSpec((1, N_HEADS, HEAD_DIM, HEAD_DIM), lambda b, s: (b, 0, 0, 0))],
        out_shape=[jax.ShapeDtypeStruct((bsz, seq, D_MODEL), BF16),
                   jax.ShapeDtypeStruct((bsz, N_HEADS, HEAD_DIM, HEAD_DIM), F32)],
        scratch_shapes=[pltpu.VMEM((SUBLANES, N_RW), F32),
                        pltpu.VMEM((N_PAIRS, PAIR, PAIR), F32),
                        pltpu.VMEM((8, N_PAIRS, tm, PAIR), F32),
                        pltpu.VMEM((N_PAIRS, tm, PAIR), F32)],
        compiler_params=pltpu.CompilerParams(
            dimension_semantics=("arbitrary", "arbitrary"), vmem_limit_bytes=VMEM_LIMIT),
        name="rwkv_prompt",
    )(x, *consts)


def _conv_kernel(*refs, tm, sample, seq_len=0):
    if sample:
        (x_ref, g_ref, w_ref, cw_ref, cb0_ref, cb1_ref, yc_ref, u_ref, h_ref) = refs
    else:
        (x_ref, g_ref, w_ref, cw_ref, yc_ref, u_ref, h_ref, carry_ref) = refs
        s_idx = pl.program_id(1)

        @pl.when(s_idx == 0)
        def _():
            carry_ref[...] = jnp.zeros_like(carry_ref)

    x = x_ref[0] if not sample else x_ref[...]
    hf = _rms(x, g_ref[...])
    p = jnp.dot(hf.astype(BF16), w_ref[...], preferred_element_type=F32)
    xin = p[:, 0:D_MODEL]
    bg = p[:, D_MODEL:2 * D_MODEL]
    cg = p[:, 2 * D_MODEL:3 * D_MODEL]
    zc = p[:, 3 * D_MODEL:]
    u = cg * xin
    if sample:
        tpos = _iota(u.shape, 0) % seq_len
        u1 = jnp.where(tpos == 0, cb1_ref[...], pltpu.roll(u, 1, 0))
        u2 = jnp.where(tpos == 0, cb0_ref[...],
                       jnp.where(tpos == 1, cb1_ref[...], pltpu.roll(u, 2, 0)))
        u_ref[...] = u
        h_ref[...] = hf
    else:
        carry = carry_ref[...]
        u1 = _shift_rows(u, 1, carry)
        u2 = _shift_rows(u, 2, carry)
        carry_ref[...] = u[tm - SUBLANES:, :]
        u_ref[0] = u[tm - SUBLANES:, :]
        h_ref[0] = hf[tm - SUBLANES:, :]
    cw = cw_ref[...]
    conv = cw[0:1, :] * u2 + cw[1:2, :] * u1 + cw[2:3, :] * u
    y_c = bg * conv * _silu(zc)
    if sample:
        yc_ref[...] = y_c.astype(BF16)
    else:
        yc_ref[0] = y_c.astype(BF16)


def _conv_prompt(x, wts, tm=512):
    bsz, seq, _ = x.shape
    consts = [wts["norm_g"].reshape(1, -1), wts["w_conv"], wts["conv_w"]]
    tail = pl.BlockSpec((1, SUBLANES, D_MODEL), lambda b, s: (b, 0, 0))
    return pl.pallas_call(
        functools.partial(_conv_kernel, tm=tm, sample=False),
        grid=(bsz, seq // tm),
        in_specs=[pl.BlockSpec((1, tm, D_MODEL), lambda b, s: (b, s, 0))]
        + [_const_spec(c.shape) for c in consts],
        out_specs=[pl.BlockSpec((1, tm, D_MODEL), lambda b, s: (b, s, 0)), tail, tail],
        out_shape=[jax.ShapeDtypeStruct((bsz, seq, D_MODEL), BF16),
                   jax.ShapeDtypeStruct((bsz, SUBLANES, D_MODEL), F32),
                   jax.ShapeDtypeStruct((bsz, SUBLANES, D_MODEL), F32)],
        scratch_shapes=[pltpu.VMEM((SUBLANES, D_MODEL), F32)],
        compiler_params=pltpu.CompilerParams(
            dimension_semantics=("arbitrary", "arbitrary"), vmem_limit_bytes=VMEM_LIMIT),
        name="conv_prompt",
    )(x, *consts)


def _conv_sample(x2d, cb0, cb1, wts, seq_len, tm=256):
    rows = x2d.shape[0]
    tm = min(tm, rows)
    consts = [wts["norm_g"].reshape(1, -1), wts["w_conv"], wts["conv_w"]]
    blk = pl.BlockSpec((tm, D_MODEL), lambda i: (i, 0))
    return pl.pallas_call(
        functools.partial(_conv_kernel, tm=tm, sample=True, seq_len=seq_len),
        grid=(rows // tm,),
        in_specs=[blk] + [_const_spec(c.shape) for c in consts] + [blk, blk],
        out_specs=[blk, blk, blk],
        out_shape=[jax.ShapeDtypeStruct((rows, D_MODEL), BF16),
                   jax.ShapeDtypeStruct((rows, D_MODEL), F32),
                   jax.ShapeDtypeStruct((rows, D_MODEL), F32)],
        compiler_params=pltpu.CompilerParams(
            dimension_semantics=("arbitrary",), vmem_limit_bytes=VMEM_LIMIT),
        name="conv_sample",
    )(x2d, *consts, cb0, cb1)


def _out_kernel(x_ref, yc_ref, yr_ref, g_ref, wg_ref, woc_ref, wor_ref, wo_ref, fg_ref, y_ref):
    x = x_ref[...]
    h = _rms(x, g_ref[...]).astype(BF16)
    gates = jnp.dot(h, wg_ref[...], preferred_element_type=F32)
    pc = jnp.dot(yc_ref[...], woc_ref[...], preferred_element_type=F32)
    pr = jnp.dot(yr_ref[...], wor_ref[...], preferred_element_type=F32)
    m = _sigmoid(gates[:, :D_MODEL]) * pc + _sigmoid(gates[:, D_MODEL:]) * pr
    out = jnp.dot(m.astype(BF16), wo_ref[...], preferred_element_type=F32)
    y_ref[...] = _rms(x + out, fg_ref[...])


def _out_stage(x2d, yc, yr, wts, tm=512):
    rows = x2d.shape[0]
    tm = min(tm, rows)
    consts = [wts["norm_g"].reshape(1, -1), wts["w_gate"], wts["w_out_c"], wts["w_out_r"],
              wts["w_o"], wts["final_g"].reshape(1, -1)]
    blk = pl.BlockSpec((tm, D_MODEL), lambda i: (i, 0))
    return pl.pallas_call(
        _out_kernel,
        grid=(rows // tm,),
        in_specs=[blk, blk, blk] + [_const_spec(c.shape) for c in consts],
        out_specs=blk,
        out_shape=jax.ShapeDtypeStruct((rows, D_MODEL), F32),
        compiler_params=pltpu.CompilerParams(
            dimension_semantics=("arbitrary",), vmem_limit_bytes=VMEM_LIMIT),
        name="out_stage",
    )(x2d, yc, yr, *consts)


SEQ_PAD = SUBLANES


def _unfold_heads(x8, fold_ref, exact):
    t = jnp.concatenate([x8] * N_HEADS, axis=0)
    keep = (_iota(t.shape, 0) // SUBLANES) == (_iota(t.shape, 1) // HEAD_DIM)
    t = jnp.where(keep, t, 0.0)
    if exact:
        return _dot_exact_rhs(t, fold_ref[...])
    return _dot(t, fold_ref[...])


def _rwkv_sample_kernel(x_ref, xp_ref, hl_ref, s0_ref, g_ref, w_ref, mu_ref, w0_ref, lw_ref,
                        a0_ref, kk_ref, ka_ref, rk_ref, lnw_ref, lnb_ref, e_ref, et_ref,
                        fold_ref, rep_ref,
                        yr_ref, s1_ref,
                        xk_ref, z_ref, uv_ref, bk_ref, gc_ref, *, nseq, seq_len):
    rows = nseq * SEQ_PAD
    g = g_ref[...]
    tpos = _iota((rows, 1), 0) % SEQ_PAD
    h = _rms(x_ref[...], g).astype(BF16)
    hprev = jnp.where(tpos == 0, hl_ref[...], _rms(xp_ref[...], g)).astype(BF16)
    p = jnp.dot(h, w_ref[...], preferred_element_type=F32)
    pp = jnp.dot(hprev, w_ref[...], preferred_element_type=F32)
    xm = p + (pp - p) * mu_ref[...]
    r, k_h, v, kk, b, logd, zr = _rwkv_pointwise(xm, w0_ref[...], lw_ref[...], a0_ref[...],
                                                 kk_ref[...], ka_ref[...], e_ref, et_ref)
    valid = tpos < seq_len
    logd = jnp.where(valid, logd, 0.0)
    m_low, m_all = _block_masks(rows, SEQ_PAD)
    cum = _dot_exact_lhs(m_low, logd)
    cum_c = _dot_exact_lhs(m_all, logd)
    khat, rhat, btil, ktil, bchk, kchk, g_c = _scan_operands(r, k_h, v, kk, b, logd, cum, cum_c)

    def sh(x, j):
        return x if j == 0 else pltpu.roll(x, j, 0)

    def up4(x):
        return pltpu.roll(x, rows - seq_len, 0)

    xk_ref[...] = jnp.where(valid, khat, sh(rhat, seq_len))

    rep = rep_ref[...]
    hv = _iota((D_MODEL, PAIR), 0) // HEAD_DIM
    hj = _iota((D_MODEL, PAIR), 1) // SUBLANES
    m_diag = hv == hj

    def phase1(i, carry):
        r8 = pl.ds(pl.multiple_of(i * SEQ_PAD, SEQ_PAD), SEQ_PAD)
        s2d = s0_ref[i].reshape(D_MODEL, HEAD_DIM)
        xc = _unfold_heads(xk_ref[r8, :], fold_ref, exact=False)
        zz = jnp.where(m_diag, _dot_nt(s2d, xc), 0.0)
        z_ref[r8, :] = _dot_nt(rep, zz)
        return carry

    lax.fori_loop(0, nseq, phase1, 0)

    z = z_ref[...]
    z_r = up4(z)

    def seg(x, y, j):
        return _seg_sum(x * sh(y, j), e_ref, et_ref)

    u = z
    for j in range(1, seq_len):
        u = u + jnp.where(tpos >= j, seg(khat, ktil, j) * sh(v, j), 0.0)
    m_ab = [None] + [-seg(khat, btil, j) for j in range(1, seq_len)]
    for t in range(1, seq_len):
        acc = u
        for j in range(1, t + 1):
            acc = acc + m_ab[j] * sh(u, j)
        u = jnp.where(tpos == t, acc, u)
    y = z_r
    for j in range(seq_len):
        term = seg(rhat, ktil, j) * sh(v, j) - seg(rhat, btil, j) * sh(u, j)
        y = y + jnp.where(tpos >= j, term, 0.0)
    y_r = _rwkv_post(y, r, k_h, v, zr, rk_ref[...], lnw_ref[...], lnb_ref[...], e_ref, et_ref)
    yr_ref[...] = y_r.astype(BF16)

    uv_ref[...] = jnp.where(valid, u, sh(v, seq_len))
    bk_ref[...] = jnp.where(valid, bchk, sh(kchk, seq_len))
    gc_ref[...] = g_c
    row_is = _iota((PAIR, D_MODEL), 0) // SUBLANES
    lane_is = _iota((PAIR, D_MODEL), 1) // HEAD_DIM
    m_keep = row_is == lane_is

    def phase3(i, carry):
        r8 = pl.ds(pl.multiple_of(i * SEQ_PAD, SEQ_PAD), SEQ_PAD)
        a_m = jnp.where(m_keep, jnp.concatenate([uv_ref[r8, :]] * N_HEADS, axis=0), 0.0)
        bk = _unfold_heads(bk_ref[r8, :], fold_ref, exact=False)
        upd = _dot_tn(a_m, bk)
        gcf = _unfold_heads(gc_ref[r8, :], fold_ref, exact=True)
        for hd in range(N_HEADS):
            s1_ref[i, hd] = (s0_ref[i, hd] * gcf[hd * SUBLANES:hd * SUBLANES + 1, :]
                             + upd[hd * HEAD_DIM:(hd + 1) * HEAD_DIM, :])
        return carry

    lax.fori_loop(0, nseq, phase3, 0)


def _rwkv_sample(x_pad, xprev_pad, hlast_pad, s0, wts, seq_len, nseq=8):
    rows_total = x_pad.shape[0]
    n_all = rows_total // SEQ_PAD
    nseq = min(nseq, n_all)
    rows = nseq * SEQ_PAD
    row = lambda a: a.reshape(1, -1)
    consts = [row(wts["norm_g"]), wts["w_rw"], row(wts["mu_rw"]), row(wts["w0"]), wts["lora_w"],
              row(wts["a0"]), row(wts["k_k"]), row(wts["k_a"]), row(wts["r_k"]),
              row(wts["ln_w"]), row(wts["ln_b"]), wts["e"], wts["et"], wts["fold"], wts["rep"]]
    blk = pl.BlockSpec((rows, D_MODEL), lambda i: (i, 0))
    sblk = pl.BlockSpec((nseq, N_HEADS, HEAD_DIM, HEAD_DIM), lambda i: (i, 0, 0, 0))
    return pl.pallas_call(
        functools.partial(_rwkv_sample_kernel, nseq=nseq, seq_len=seq_len),
        grid=(n_all // nseq,),
        in_specs=[blk, blk, blk, sblk] + [_const_spec(c.shape) for c in consts],
        out_specs=[blk, sblk],
        out_shape=[jax.ShapeDtypeStruct((rows_total, D_MODEL), BF16),
                   jax.ShapeDtypeStruct(s0.shape, F32)],
        scratch_shapes=[pltpu.VMEM((rows, D_MODEL), F32) for _ in range(5)],
        compiler_params=pltpu.CompilerParams(
            dimension_semantics=("arbitrary",), vmem_limit_bytes=VMEM_LIMIT),
        name="rwkv_sample",
    )(x_pad, xprev_pad, hlast_pad, s0, *consts)


def _prep_weights(norm_g, w_in, conv_w, mu_shift, w0, w_up, a0, a_up, k_k, k_a, r_k, ln_w, ln_b,
                  w_out_c, w_out_r, w_o, final_g):
    d = D_MODEL
    c0 = 4 * d
    r_cols = lambda a: jnp.concatenate(
        [a[..., 0:3 * d], a[..., 3 * d + 2 * LORA:4 * d + 2 * LORA], a[..., 3 * d:3 * d + 2 * LORA]],
        axis=-1)
    w_rwkv = w_in[:, c0:c0 + N_RW]
    zeros = jnp.zeros((LORA, d), F32)
    lora_w = jnp.concatenate([jnp.concatenate([w_up, zeros], axis=1),
                              jnp.concatenate([zeros, a_up], axis=1)], axis=0)
    head_of_lane = jnp.arange(d) // HEAD_DIM
    e = (head_of_lane[:, None] == jnp.arange(LANES)[None, :]).astype(BF16)
    fold = (jnp.arange(d)[:, None] % HEAD_DIM == jnp.arange(HEAD_DIM)[None, :]).astype(BF16)
    rep = (jnp.arange(SUBLANES)[:, None] == jnp.arange(LANES)[None, :] % SUBLANES).astype(BF16)
    return dict(
        norm_g=norm_g, final_g=final_g, conv_w=conv_w,
        w_conv=w_in[:, :c0].astype(BF16),
        w_rw=r_cols(w_rwkv).astype(BF16),
        mu_rw=r_cols(mu_shift),
        w_gate=w_in[:, c0 + N_RW:].astype(BF16),
        w0=w0, a0=a0, lora_w=lora_w.astype(BF16), k_k=k_k, k_a=k_a, r_k=r_k.reshape(-1),
        ln_w=ln_w, ln_b=ln_b,
        w_out_c=w_out_c.astype(BF16), w_out_r=w_out_r.astype(BF16), w_o=w_o.astype(BF16),
        e=e, et=e.T, fold=fold, rep=rep)


def _layer_prompt(x, wts):
    bsz, seq, d = x.shape
    yc, u_tail, h_tail = _conv_prompt(x, wts)
    yr, s_fin = _rwkv_prompt(x, wts)
    y = _out_stage(x.reshape(bsz * seq, d), yc.reshape(bsz * seq, d), yr.reshape(bsz * seq, d), wts)
    return (y.reshape(bsz, seq, d), u_tail[:, SUBLANES - 2:, :], h_tail[:, SUBLANES - 1, :], s_fin)


def _layer_sample(x, conv_buf, h_last, s0, wts):
    bsz, seq, d = x.shape
    rows = bsz * seq
    x2d = x.reshape(rows, d)
    rep_rows = lambda a: jnp.broadcast_to(a[:, None, :], (bsz, seq, d)).reshape(rows, d)
    yc, u, h = _conv_sample(x2d, rep_rows(conv_buf[:, 0]), rep_rows(conv_buf[:, 1]), wts, seq)
    pad = lambda a: jnp.pad(a, ((0, 0), (0, SEQ_PAD - seq), (0, 0))).reshape(bsz * SEQ_PAD, d)
    x_prev = jnp.concatenate([jnp.zeros((bsz, 1, d), x.dtype), x[:, :-1]], axis=1)
    hl = jnp.broadcast_to(h_last[:, None, :], (bsz, seq, d))
    yr_pad, s1 = _rwkv_sample(pad(x), pad(x_prev), pad(hl), s0, wts, seq)
    yr = yr_pad.reshape(bsz, SEQ_PAD, d)[:, :seq].reshape(rows, d)
    y = _out_stage(x2d, yc, yr, wts)
    u3 = u.reshape(bsz, seq, d)
    return (y.reshape(bsz, seq, d), u3[:, seq - 2:], h.reshape(bsz, seq, d)[:, seq - 1], s1)


def kernel(x_prompt, x_sample, state_conv, state_shift, state_rwkv, norm_g, w_in, conv_w, mu_shift,
           w0, w_up, a0, a_up, k_k, k_a, r_k, ln_w, ln_b, w_out_c, w_out_r, w_o, final_g):
    assert norm_g.shape[0] == 1, "single-layer step"
    wts = _prep_weights(norm_g[0], w_in[0], conv_w[0], mu_shift[0], w0[0], w_up[0], a0[0], a_up[0],
                        k_k[0], k_a[0], r_k[0], ln_w[0], ln_b[0], w_out_c[0], w_out_r[0], w_o[0],
                        final_g)
    y_p, c_p, s_p, r_p = _layer_prompt(x_prompt, wts)
    y_s, c_s, s_s, r_s = _layer_sample(x_sample, state_conv[0], state_shift[0], state_rwkv[0], wts)
    lead = lambda a: a[None]
    return (y_p, y_s, lead(c_p), lead(s_p), lead(r_p), lead(c_s), lead(s_s), lead(r_s))
```

```python
import functools

import jax
import jax.numpy as jnp
from jax import lax
from jax.experimental import pallas as pl
from jax.experimental.pallas import tpu as pltpu

F32 = jnp.float32
BF16 = jnp.bfloat16

D_MODEL = 1024
HEAD_DIM = 64
N_HEADS = D_MODEL // HEAD_DIM
LORA = 64
RMS_EPS = 1e-6
GN_EPS = 64e-5
LANES = 128
SUBLANES = 8
PAIR = 2 * HEAD_DIM
N_PAIRS = D_MODEL // PAIR
CHUNK = 64
PAIRS_PER_STEP = 2
N_RW = 4 * D_MODEL + 2 * LORA
VMEM_LIMIT = 56 * 1024 * 1024


def _dot(a, b):
    return jnp.dot(a.astype(BF16), b.astype(BF16), preferred_element_type=F32)


def _dot_nt(a, b):
    return lax.dot_general(a.astype(BF16), b.astype(BF16), (((1,), (1,)), ((), ())),
                           preferred_element_type=F32)


def _dot_tn(a, b):
    return lax.dot_general(a.astype(BF16), b.astype(BF16), (((0,), (0,)), ((), ())),
                           preferred_element_type=F32)


def _split3(x):
    hi = x.astype(BF16)
    r1 = x - hi.astype(F32)
    mid = r1.astype(BF16)
    lo = (r1 - mid.astype(F32)).astype(BF16)
    return hi, mid, lo


def _dot_exact_lhs(m_bf16, x):
    hi, mid, lo = _split3(x)
    return (jnp.dot(m_bf16, hi, preferred_element_type=F32)
            + jnp.dot(m_bf16, mid, preferred_element_type=F32)
            + jnp.dot(m_bf16, lo, preferred_element_type=F32))


def _dot_exact_rhs(x, m_bf16):
    hi, mid, lo = _split3(x)
    return (jnp.dot(hi, m_bf16, preferred_element_type=F32)
            + jnp.dot(mid, m_bf16, preferred_element_type=F32)
            + jnp.dot(lo, m_bf16, preferred_element_type=F32))


def _rms(x, g):
    return x * lax.rsqrt(jnp.mean(x * x, axis=-1, keepdims=True) + RMS_EPS) * g


def _sigmoid(x):
    return 1.0 / (1.0 + jnp.exp(-x))


def _silu(x):
    return x * _sigmoid(x)


def _softplus(x):
    return jnp.maximum(x, 0.0) + jnp.log(1.0 + jnp.exp(-jnp.abs(x)))


def _iota(shape, dim):
    return lax.broadcasted_iota(jnp.int32, shape, dim)


def _shift_rows(x, n, carry8):
    rolled = pltpu.roll(x, n, 0)
    head = jnp.where(_iota((SUBLANES, x.shape[1]), 0) < n, pltpu.roll(carry8, n, 0),
                     rolled[:SUBLANES])
    return jnp.concatenate([head, rolled[SUBLANES:]], axis=0)


def _seg_sum(x, e_ref, et_ref):
    s = _dot_exact_rhs(x, e_ref[...])
    return _dot_exact_rhs(s, et_ref[...])


def _block_masks(rows, blk):
    r = _iota((rows, rows), 0)
    c = _iota((rows, rows), 1)
    same = (r // blk) == (c // blk)
    return (jnp.where(same & (c <= r), 1.0, 0.0).astype(BF16),
            jnp.where(same, 1.0, 0.0).astype(BF16))


def _rwkv_pointwise(xm, w0, lora_w, a0, k_k, k_a, e_ref, et_ref):
    r = xm[:, 0:D_MODEL]
    k = xm[:, D_MODEL:2 * D_MODEL]
    v = xm[:, 2 * D_MODEL:3 * D_MODEL]
    zr = xm[:, 3 * D_MODEL:4 * D_MODEL]
    da = xm[:, 4 * D_MODEL:]
    lane = _iota(da.shape, 1)
    lora_in = jnp.where(lane < LORA, jnp.tanh(da), da)
    lo = _dot(lora_in, lora_w)
    w_logit = w0 + lo[:, :D_MODEL]
    w_log = -_softplus(-w_logit) - 0.5
    logd = -jnp.exp(w_log)
    a = _sigmoid(a0 + lo[:, D_MODEL:])
    kkr = k * k_k
    nrm = jnp.sqrt(_seg_sum(kkr * kkr, e_ref, et_ref))
    kk = kkr / jnp.maximum(nrm, 1e-12)
    k_h = k * (1.0 + (a - 1.0) * k_a)
    b = kk * a
    return r, k_h, v, kk, b, logd, zr


def _rwkv_post(ys, r, k_h, v, zr, r_k, ln_w, ln_b, e_ref, et_ref):
    inv_n = 1.0 / HEAD_DIM
    mu = _seg_sum(ys, e_ref, et_ref) * inv_n
    yc = ys - mu
    var = _seg_sum(yc * yc, e_ref, et_ref) * inv_n
    yn = yc * lax.rsqrt(var + GN_EPS) * ln_w + ln_b
    bonus = _seg_sum(r * k_h * r_k, e_ref, et_ref) * v
    return (yn + bonus) * _silu(zr)


def _scan_operands(r, k_h, v, kk, b, logd, cum, cum_c):
    enc = jnp.exp(-cum)
    e_c = jnp.exp(cum_c - cum)
    khat = kk * jnp.exp(cum - logd)
    rhat = r * jnp.exp(cum)
    btil = b * enc
    ktil = k_h * enc
    bchk = -(b * e_c)
    kchk = k_h * e_c
    g_c = jnp.exp(cum_c)
    return khat, rhat, btil, ktil, bchk, kchk, g_c


def _chunk_transfer(units, m_bd, m_sl, m_l):
    two = lambda x: jnp.concatenate([x, x], axis=0)
    fold = lambda x: x[:CHUNK] + x[CHUNK:]
    expand = lambda x: jnp.where(m_bd, two(x), 0.0)
    gs = [_dot_nt(jnp.concatenate([kh, rh], axis=0),
                  jnp.concatenate([expand(bt), expand(kt)], axis=0))
          for kh, rh, bt, kt, _, _, _ in units]
    ps = [jnp.where(m_sl, -two(g[:CHUNK, :PAIR]), 0.0) for g in gs]
    m_aks = [jnp.where(m_sl, two(g[:CHUNK, PAIR:]), 0.0) for g in gs]
    m_rbs = [jnp.where(m_l, -two(g[CHUNK:, :PAIR]), 0.0) for g in gs]
    m_rks = [jnp.where(m_l, two(g[CHUNK:, PAIR:]), 0.0) for g in gs]
    vxs = [expand(u[6]) for u in units]
    zs = [_dot(jnp.concatenate([m_ak, m_rk], axis=0), vx)
          for m_ak, m_rk, vx in zip(m_aks, m_rks, vxs)]
    xs = [jnp.concatenate([expand(u[0]), z[:PAIR]], axis=1) for u, z in zip(units, zs)]
    for j in range(6):
        xs = [x + _dot(p, x) for p, x in zip(ps, xs)]
        if j < 5:
            ps = [_dot(p, p) for p in ps]
    t1s = [_dot(m_rb, x) for m_rb, x in zip(m_rbs, xs)]
    out = []
    p_ts = [_dot_tn(fold(x[:, :PAIR]), u[4]) for x, u in zip(xs, units)]
    q_ts = [_dot_tn(jnp.concatenate([fold(x[:, PAIR:]), u[6]], axis=0),
                    jnp.concatenate([u[4], u[5]], axis=0)) for x, u in zip(xs, units)]
    for u, z, t1, p_t, q_t in zip(units, zs, t1s, p_ts, q_ts):
        y_w = u[1] + fold(t1[:, :PAIR])
        y_c = fold(t1[:, PAIR:] + z[PAIR:])
        out.append((y_w, y_c, jnp.where(m_bd, p_t, 0.0), jnp.where(m_bd, q_t, 0.0)))
    return out


def _rwkv_prompt_kernel(x_ref, g_ref, w_ref, mu_ref, w0_ref, lw_ref, a0_ref, kk_ref, ka_ref,
                        rk_ref, lnw_ref, lnb_ref, e_ref, et_ref,
                        yr_ref, sfin_ref,
                        carry_ref, st_ref, ops_ref, ys_ref, *, tm):
    s_idx = pl.program_id(1)

    @pl.when(s_idx == 0)
    def _():
        carry_ref[...] = jnp.zeros_like(carry_ref)
        st_ref[...] = jnp.zeros_like(st_ref)

    h = _rms(x_ref[0], g_ref[...]).astype(BF16)
    p = jnp.dot(h, w_ref[...], preferred_element_type=F32)
    prev = _shift_rows(p, 1, carry_ref[...])
    carry_ref[...] = p[tm - SUBLANES:, :]
    xm = p + (prev - p) * mu_ref[...]
    r, k_h, v, kk, b, logd, zr = _rwkv_pointwise(xm, w0_ref[...], lw_ref[...], a0_ref[...],
                                                 kk_ref[...], ka_ref[...], e_ref, et_ref)
    m_low, m_all = _block_masks(tm, CHUNK)
    cum = _dot_exact_lhs(m_low, logd)
    cum_c = _dot_exact_lhs(m_all, logd)
    ops = _scan_operands(r, k_h, v, kk, b, logd, cum, cum_c) + (v,)
    for i, arr in enumerate(ops):
        for hp in range(N_PAIRS):
            ops_ref[i, hp] = arr[:, hp * PAIR:(hp + 1) * PAIR]

    rr = _iota((PAIR, PAIR), 0)
    cc = _iota((PAIR, PAIR), 1)
    m_bd = (rr // HEAD_DIM) == (cc // HEAD_DIM)
    m_sl = m_bd & ((cc % HEAD_DIM) < (rr % HEAD_DIM))
    m_l = m_bd & ((cc % HEAD_DIM) <= (rr % HEAD_DIM))

    n_chunks = tm // CHUNK

    def group_body(gi, carry):
        hps = [gi * PAIRS_PER_STEP + i for i in range(PAIRS_PER_STEP)]
        rows = [pl.ds(c * CHUNK, CHUNK) for c in range(n_chunks)]
        units = [tuple(ops_ref[i, hp, rw, :] for i in (0, 1, 2, 3, 4, 5, 7))
                 for hp in hps for rw in rows]
        tr = _chunk_transfer(units, m_bd, m_sl, m_l)
        s_pairs = [st_ref[hp] for hp in hps]
        for c in range(n_chunks):
            for i, hp in enumerate(hps):
                y_w, y_c, p_t, q_t = tr[i * n_chunks + c]
                g_c = ops_ref[6, hp, pl.ds(c * CHUNK, 1), :]
                s0 = s_pairs[i]
                ys_ref[hp, rows[c], :] = _dot_nt(y_w, s0) + y_c
                s_pairs[i] = s0 * g_c + _dot(s0, p_t) + q_t
        for i, hp in enumerate(hps):
            st_ref[hp] = s_pairs[i]
        return carry

    lax.fori_loop(0, N_PAIRS // PAIRS_PER_STEP, group_body, 0)

    ys = jnp.concatenate([ys_ref[hp] for hp in range(N_PAIRS)], axis=1)
    y_r = _rwkv_post(ys, r, k_h, v, zr, rk_ref[...], lnw_ref[...], lnb_ref[...], e_ref, et_ref)
    yr_ref[0] = y_r.astype(BF16)

    @pl.when(s_idx == pl.num_programs(1) - 1)
    def _():
        for hp in range(N_PAIRS):
            sp = st_ref[hp]
            sfin_ref[0, 2 * hp] = sp[:HEAD_DIM, :HEAD_DIM]
            sfin_ref[0, 2 * hp + 1] = sp[HEAD_DIM:, HEAD_DIM:]


def _const_spec(shape):
    nd = len(shape)
    return pl.BlockSpec(shape, lambda *_: (0,) * nd)


def _rwkv_prompt(x, wts, tm=256):
    bsz, seq, _ = x.shape
    grid = (bsz, seq // tm)
    row = lambda a: a.reshape(1, -1)
    consts = [row(wts["norm_g"]), wts["w_rw"], row(wts["mu_rw"]), row(wts["w0"]), wts["lora_w"],
              row(wts["a0"]), row(wts["k_k"]), row(wts["k_a"]), row(wts["r_k"]),
              row(wts["ln_w"]), row(wts["ln_b"]), wts["e"], wts["et"]]
    kern = functools.partial(_rwkv_prompt_kernel, tm=tm)
    return pl.pallas_call(
        kern,
        grid=grid,
        in_specs=[pl.BlockSpec((1, tm, D_MODEL), lambda b, s: (b, s, 0))]
        + [_const_spec(c.shape) for c in consts],
        out_specs=[pl.BlockSpec((1, tm, D_MODEL), lambda b, s: (b, s, 0)),
                   pl.BlockSpec((1, N_HEADS, HEAD_DIM, HEAD_DIM), lambda b, s: (b, 0, 0, 0))],
        out_shape=[jax.ShapeDtypeStruct((bsz, seq, D_MODEL), BF16),
                   jax.ShapeDtypeStruct((bsz, N_HEADS, HEAD_DIM, HEAD_DIM), F32)],
        scratch_shapes=[pltpu.VMEM((SUBLANES, N_RW), F32),
                        pltpu.VMEM((N_PAIRS, PAIR, PAIR), F32),
                        pltpu.VMEM((8, N_PAIRS, tm, PAIR), F32),
                        pltpu.VMEM((N_PAIRS, tm, PAIR), F32)],
        compiler_params=pltpu.CompilerParams(
            dimension_semantics=("arbitrary", "arbitrary"), vmem_limit_bytes=VMEM_LIMIT),
        name="rwkv_prompt",
    )(x, *consts)


def _conv_kernel(*refs, tm, sample, seq_len=0):
    if sample:
        (x_ref, g_ref, w_ref, cw_ref, cb0_ref, cb1_ref, yc_ref, u_ref, h_ref) = refs
    else:
        (x_ref, g_ref, w_ref, cw_ref, yc_ref, u_ref, h_ref, carry_ref) = refs
        s_idx = pl.program_id(1)

        @pl.when(s_idx == 0)
        def _():
            carry_ref[...] = jnp.zeros_like(carry_ref)

    x = x_ref[0] if not sample else x_ref[...]
    hf = _rms(x, g_ref[...])
    p = jnp.dot(hf.astype(BF16), w_ref[...], preferred_element_type=F32)
    xin = p[:, 0:D_MODEL]
    bg = p[:, D_MODEL:2 * D_MODEL]
    cg = p[:, 2 * D_MODEL:3 * D_MODEL]
    zc = p[:, 3 * D_MODEL:]
    u = cg * xin
    if sample:
        tpos = _iota(u.shape, 0) % seq_len
        u1 = jnp.where(tpos == 0, cb1_ref[...], pltpu.roll(u, 1, 0))
        u2 = jnp.where(tpos == 0, cb0_ref[...],
                       jnp.where(tpos == 1, cb1_ref[...], pltpu.roll(u, 2, 0)))
        u_ref[...] = u
        h_ref[...] = hf
    else:
        carry = carry_ref[...]
        u1 = _shift_rows(u, 1, carry)
        u2 = _shift_rows(u, 2, carry)
        carry_ref[...] = u[tm - SUBLANES:, :]
        u_ref[0] = u[tm - SUBLANES:, :]
        h_ref[0] = hf[tm - SUBLANES:, :]
    cw = cw_ref[...]
    conv = cw[0:1, :] * u2 + cw[1:2, :] * u1 + cw[2:3, :] * u
    y_c = bg * conv * _silu(zc)
    if sample:
        yc_ref[...] = y_c.astype(BF16)
    else:
        yc_ref[0] = y_c.astype(BF16)


def _conv_prompt(x, wts, tm=512):
    bsz, seq, _ = x.shape
    consts = [wts["norm_g"].reshape(1, -1), wts["w_conv"], wts["conv_w"]]
    tail = pl.BlockSpec((1, SUBLANES, D_MODEL), lambda b, s: (b, 0, 0))
    return pl.pallas_call(
        functools.partial(_conv_kernel, tm=tm, sample=False),
        grid=(bsz, seq // tm),
        in_specs=[pl.BlockSpec((1, tm, D_MODEL), lambda b, s: (b, s, 0))]
        + [_const_spec(c.shape) for c in consts],
        out_specs=[pl.BlockSpec((1, tm, D_MODEL), lambda b, s: (b, s, 0)), tail, tail],
        out_shape=[jax.ShapeDtypeStruct((bsz, seq, D_MODEL), BF16),
                   jax.ShapeDtypeStruct((bsz, SUBLANES, D_MODEL), F32),
                   jax.ShapeDtypeStruct((bsz, SUBLANES, D_MODEL), F32)],
        scratch_shapes=[pltpu.VMEM((SUBLANES, D_MODEL), F32)],
        compiler_params=pltpu.CompilerParams(
            dimension_semantics=("arbitrary", "arbitrary"), vmem_limit_bytes=VMEM_LIMIT),
        name="conv_prompt",
    )(x, *consts)


def _conv_sample(x2d, cb0, cb1, wts, seq_len, tm=256):
    rows = x2d.shape[0]
    tm = min(tm, rows)
    consts = [wts["norm_g"].reshape(1, -1), wts["w_conv"], wts["conv_w"]]
    blk = pl.BlockSpec((tm, D_MODEL), lambda i: (i, 0))
    return pl.pallas_call(
        functools.partial(_conv_kernel, tm=tm, sample=True, seq_len=seq_len),
        grid=(rows // tm,),
        in_specs=[blk] + [_const_spec(c.shape) for c in consts] + [blk, blk],
        out_specs=[blk, blk, blk],
        out_shape=[jax.ShapeDtypeStruct((rows, D_MODEL), BF16),
                   jax.ShapeDtypeStruct((rows, D_MODEL), F32),
                   jax.ShapeDtypeStruct((rows, D_MODEL), F32)],
        compiler_params=pltpu.CompilerParams(
            dimension_semantics=("arbitrary",), vmem_limit_bytes=VMEM_LIMIT),
        name="conv_sample",
    )(x2d, *consts, cb0, cb1)


def _out_kernel(x_ref, yc_ref, yr_ref, g_ref, wg_ref, woc_ref, wor_ref, wo_ref, fg_ref, y_ref):
    x = x_ref[...]
    h = _rms(x, g_ref[...]).astype(BF16)
    gates = jnp.dot(h, wg_ref[...], preferred_element_type=F32)
    pc = jnp.dot(yc_ref[...], woc_ref[...], preferred_element_type=F32)
    pr = jnp.dot(yr_ref[...], wor_ref[...], preferred_element_type=F32)
    m = _sigmoid(gates[:, :D_MODEL]) * pc + _sigmoid(gates[:, D_MODEL:]) * pr
    out = jnp.dot(m.astype(BF16), wo_ref[...], preferred_element_type=F32)
    y_ref[...] = _rms(x + out, fg_ref[...])


def _out_stage(x2d, yc, yr, wts, tm=512):
    rows = x2d.shape[0]
    tm = min(tm, rows)
    consts = [wts["norm_g"].reshape(1, -1), wts["w_gate"], wts["w_out_c"], wts["w_out_r"],
              wts["w_o"], wts["final_g"].reshape(1, -1)]
    blk = pl.BlockSpec((tm, D_MODEL), lambda i: (i, 0))
    return pl.pallas_call(
        _out_kernel,
        grid=(rows // tm,),
        in_specs=[blk, blk, blk] + [_const_spec(c.shape) for c in consts],
        out_specs=blk,
        out_shape=jax.ShapeDtypeStruct((rows, D_MODEL), F32),
        compiler_params=pltpu.CompilerParams(
            dimension_semantics=("arbitrary",), vmem_limit_bytes=VMEM_LIMIT),
        name="out_stage",
    )(x2d, yc, yr, *consts)


SEQ_PAD = SUBLANES


def _unfold_heads(x8, fold_ref, exact):
    t = jnp.concatenate([x8] * N_HEADS, axis=0)
    keep = (_iota(t.shape, 0) // SUBLANES) == (_iota(t.shape, 1) // HEAD_DIM)
    t = jnp.where(keep, t, 0.0)
    if exact:
        return _dot_exact_rhs(t, fold_ref[...])
    return _dot(t, fold_ref[...])


def _rwkv_sample_kernel(x_ref, xp_ref, hl_ref, s0_ref, g_ref, w_ref, mu_ref, w0_ref, lw_ref,
                        a0_ref, kk_ref, ka_ref, rk_ref, lnw_ref, lnb_ref, e_ref, et_ref,
                        fold_ref, rep_ref,
                        yr_ref, s1_ref,
                        xk_ref, z_ref, uv_ref, bk_ref, gc_ref, *, nseq, seq_len):
    rows = nseq * SEQ_PAD
    g = g_ref[...]
    tpos = _iota((rows, 1), 0) % SEQ_PAD
    h = _rms(x_ref[...], g).astype(BF16)
    hprev = jnp.where(tpos == 0, hl_ref[...], _rms(xp_ref[...], g)).astype(BF16)
    p = jnp.dot(h, w_ref[...], preferred_element_type=F32)
    pp = jnp.dot(hprev, w_ref[...], preferred_element_type=F32)
    xm = p + (pp - p) * mu_ref[...]
    r, k_h, v, kk, b, logd, zr = _rwkv_pointwise(xm, w0_ref[...], lw_ref[...], a0_ref[...],
                                                 kk_ref[...], ka_ref[...], e_ref, et_ref)
    valid = tpos < seq_len
    logd = jnp.where(valid, logd, 0.0)
    m_low, m_all = _block_masks(rows, SEQ_PAD)
    cum = _dot_exact_lhs(m_low, logd)
    cum_c = _dot_exact_lhs(m_all, logd)
    khat, rhat, btil, ktil, bchk, kchk, g_c = _scan_operands(r, k_h, v, kk, b, logd, cum, cum_c)

    def sh(x, j):
        return x if j == 0 else pltpu.roll(x, j, 0)

    def up4(x):
        return pltpu.roll(x, rows - seq_len, 0)

    xk_ref[...] = jnp.where(valid, khat, sh(rhat, seq_len))

    rep = rep_ref[...]
    hv = _iota((D_MODEL, PAIR), 0) // HEAD_DIM
    hj = _iota((D_MODEL, PAIR), 1) // SUBLANES
    m_diag = hv == hj

    def phase1(i, carry):
        r8 = pl.ds(pl.multiple_of(i * SEQ_PAD, SEQ_PAD), SEQ_PAD)
        s2d = s0_ref[i].reshape(D_MODEL, HEAD_DIM)
        xc = _unfold_heads(xk_ref[r8, :], fold_ref, exact=False)
        zz = jnp.where(m_diag, _dot_nt(s2d, xc), 0.0)
        z_ref[r8, :] = _dot_nt(rep, zz)
        return carry

    lax.fori_loop(0, nseq, phase1, 0)

    z = z_ref[...]
    z_r = up4(z)

    def seg(x, y, j):
        return _seg_sum(x * sh(y, j), e_ref, et_ref)

    u = z
    for j in range(1, seq_len):
        u = u + jnp.where(tpos >= j, seg(khat, ktil, j) * sh(v, j), 0.0)
    m_ab = [None] + [-seg(khat, btil, j) for j in range(1, seq_len)]
    for t in range(1, seq_len):
        acc = u
        for j in range(1, t + 1):
            acc = acc + m_ab[j] * sh(u, j)
        u = jnp.where(tpos == t, acc, u)
    y = z_r
    for j in range(seq_len):
        term = seg(rhat, ktil, j) * sh(v, j) - seg(rhat, btil, j) * sh(u, j)
        y = y + jnp.where(tpos >= j, term, 0.0)
    y_r = _rwkv_post(y, r, k_h, v, zr, rk_ref[...], lnw_ref[...], lnb_ref[...], e_ref, et_ref)
    yr_ref[...] = y_r.astype(BF16)

    uv_ref[...] = jnp.where(valid, u, sh(v, seq_len))
    bk_ref[...] = jnp.where(valid, bchk, sh(kchk, seq_len))
    gc_ref[...] = g_c
    row_is = _iota((PAIR, D_MODEL), 0) // SUBLANES
    lane_is = _iota((PAIR, D_MODEL), 1) // HEAD_DIM
    m_keep = row_is == lane_is

    def phase3(i, carry):
        r8 = pl.ds(pl.multiple_of(i * SEQ_PAD, SEQ_PAD), SEQ_PAD)
        a_m = jnp.where(m_keep, jnp.concatenate([uv_ref[r8, :]] * N_HEADS, axis=0), 0.0)
        bk = _unfold_heads(bk_ref[r8, :], fold_ref, exact=False)
        upd = _dot_tn(a_m, bk)
        gcf = _unfold_heads(gc_ref[r8, :], fold_ref, exact=True)
        for hd in range(N_HEADS):
            s1_ref[i, hd] = (s0_ref[i, hd] * gcf[hd * SUBLANES:hd * SUBLANES + 1, :]
                             + upd[hd * HEAD_DIM:(hd + 1) * HEAD_DIM, :])
        return carry

    lax.fori_loop(0, nseq, phase3, 0)


def _rwkv_sample(x_pad, xprev_pad, hlast_pad, s0, wts, seq_len, nseq=8):
    rows_total = x_pad.shape[0]
    n_all = rows_total // SEQ_PAD
    nseq = min(nseq, n_all)
    rows = nseq * SEQ_PAD
    row = lambda a: a.reshape(1, -1)
    consts = [row(wts["norm_g"]), wts["w_rw"], row(wts["mu_rw"]), row(wts["w0"]), wts["lora_w"],
              row(wts["a0"]), row(wts["k_k"]), row(wts["k_a"]), row(wts["r_k"]),
              row(wts["ln_w"]), row(wts["ln_b"]), wts["e"], wts["et"], wts["fold"], wts["rep"]]
    blk = pl.BlockSpec((rows, D_MODEL), lambda i: (i, 0))
    sblk = pl.BlockSpec((nseq, N_HEADS, HEAD_DIM, HEAD_DIM), lambda i: (i, 0, 0, 0))
    return pl.pallas_call(
        functools.partial(_rwkv_sample_kernel, nseq=nseq, seq_len=seq_len),
        grid=(n_all // nseq,),
        in_specs=[blk, blk, blk, sblk] + [_const_spec(c.shape) for c in consts],
        out_specs=[blk, sblk],
        out_shape=[jax.ShapeDtypeStruct((rows_total, D_MODEL), BF16),
                   jax.ShapeDtypeStruct(s0.shape, F32)],
        scratch_shapes=[pltpu.VMEM((rows, D_MODEL), F32) for _ in range(5)],
        compiler_params=pltpu.CompilerParams(
            dimension_semantics=("arbitrary",), vmem_limit_bytes=VMEM_LIMIT),
        name="rwkv_sample",
    )(x_pad, xprev_pad, hlast_pad, s0, *consts)


def _prep_weights(norm_g, w_in, conv_w, mu_shift, w0, w_up, a0, a_up, k_k, k_a, r_k, ln_w, ln_b,
                  w_out_c, w_out_r, w_o, final_g):
    d = D_MODEL
    c0 = 4 * d
    r_cols = lambda a: jnp.concatenate(
        [a[..., 0:3 * d], a[..., 3 * d + 2 * LORA:4 * d + 2 * LORA], a[..., 3 * d:3 * d + 2 * LORA]],
        axis=-1)
    w_rwkv = w_in[:, c0:c0 + N_RW]
    zeros = jnp.zeros((LORA, d), F32)
    lora_w = jnp.concatenate([jnp.concatenate([w_up, zeros], axis=1),
                              jnp.concatenate([zeros, a_up], axis=1)], axis=0)
    head_of_lane = jnp.arange(d) // HEAD_DIM
    e = (head_of_lane[:, None] == jnp.arange(LANES)[None, :]).astype(BF16)
    fold = (jnp.arange(d)[:, None] % HEAD_DIM == jnp.arange(HEAD_DIM)[None, :]).astype(BF16)
    rep = (jnp.arange(SUBLANES)[:, None] == jnp.arange(LANES)[None, :] % SUBLANES).astype(BF16)
    return dict(
        norm_g=norm_g, final_g=final_g, conv_w=conv_w,
        w_conv=w_in[:, :c0].astype(BF16),
        w_rw=r_cols(w_rwkv).astype(BF16),
        mu_rw=r_cols(mu_shift),
        w_gate=w_in[:, c0 + N_RW:].astype(BF16),
        w0=w0, a0=a0, lora_w=lora_w.astype(BF16), k_k=k_k, k_a=k_a, r_k=r_k.reshape(-1),
        ln_w=ln_w, ln_b=ln_b,
        w_out_c=w_out_c.astype(BF16), w_out_r=w_out_r.astype(BF16), w_o=w_o.astype(BF16),
        e=e, et=e.T, fold=fold, rep=rep)


def _layer_prompt(x, wts):
    bsz, seq, d = x.shape
    yc, u_tail, h_tail = _conv_prompt(x, wts)
    yr, s_fin = _rwkv_prompt(x, wts)
    y = _out_stage(x.reshape(bsz * seq, d), yc.reshape(bsz * seq, d), yr.reshape(bsz * seq, d), wts)
    return (y.reshape(bsz, seq, d), u_tail[:, SUBLANES - 2:, :], h_tail[:, SUBLANES - 1, :], s_fin)


def _layer_sample(x, conv_buf, h_last, s0, wts):
    bsz, seq, d = x.shape
    rows = bsz * seq
    x2d = x.reshape(rows, d)
    rep_rows = lambda a: jnp.broadcast_to(a[:, None, :], (bsz, seq, d)).reshape(rows, d)
    yc, u, h = _conv_sample(x2d, rep_rows(conv_buf[:, 0]), rep_rows(conv_buf[:, 1]), wts, seq)
    pad = lambda a: jnp.pad(a, ((0, 0), (0, SEQ_PAD - seq), (0, 0))).reshape(bsz * SEQ_PAD, d)
    x_prev = jnp.concatenate([jnp.zeros((bsz, 1, d), x.dtype), x[:, :-1]], axis=1)
    hl = jnp.broadcast_to(h_last[:, None, :], (bsz, seq, d))
    yr_pad, s1 = _rwkv_sample(pad(x), pad(x_prev), pad(hl), s0, wts, seq)
    yr = yr_pad.reshape(bsz, SEQ_PAD, d)[:, :seq].reshape(rows, d)
    y = _out_stage(x2d, yc, yr, wts)
    u3 = u.reshape(bsz, seq, d)
    return (y.reshape(bsz, seq, d), u3[:, seq - 2:], h.reshape(bsz, seq, d)[:, seq - 1], s1)


def kernel(x_prompt, x_sample, state_conv, state_shift, state_rwkv, norm_g, w_in, conv_w, mu_shift,
           w0, w_up, a0, a_up, k_k, k_a, r_k, ln_w, ln_b, w_out_c, w_out_r, w_o, final_g):
    assert norm_g.shape[0] == 1, "single-layer step"
    wts = _prep_weights(norm_g[0], w_in[0], conv_w[0], mu_shift[0], w0[0], w_up[0], a0[0], a_up[0],
                        k_k[0], k_a[0], r_k[0], ln_w[0], ln_b[0], w_out_c[0], w_out_r[0], w_o[0],
                        final_g)
    y_p, c_p, s_p, r_p = _layer_prompt(x_prompt, wts)
    y_s, c_s, s_s, r_s = _layer_sample(x_sample, state_conv[0], state_shift[0], state_rwkv[0], wts)
    lead = lambda a: a[None]
    return (y_p, y_s, lead(c_p), lead(s_p), lead(r_p), lead(c_s), lead(s_s), lead(r_s))
```

```python
import functools

import jax
import jax.numpy as jnp
from jax import lax
from jax.experimental import pallas as pl
from jax.experimental.pallas import tpu as pltpu

F32 = jnp.float32
BF16 = jnp.bfloat16

D_MODEL = 1024
HEAD_DIM = 64
N_HEADS = D_MODEL // HEAD_DIM
LORA = 64
RMS_EPS = 1e-6
GN_EPS = 64e-5
LANES = 128
SUBLANES = 8
PAIR = 2 * HEAD_DIM
N_PAIRS = D_MODEL // PAIR
CHUNK = 64
PAIRS_PER_STEP = 2
N_RW = 4 * D_MODEL + 2 * LORA
VMEM_LIMIT = 56 * 1024 * 1024


def _dot(a, b):
    return jnp.dot(a.astype(BF16), b.astype(BF16), preferred_element_type=F32)


def _dot_nt(a, b):
    return lax.dot_general(a.astype(BF16), b.astype(BF16), (((1,), (1,)), ((), ())),
                           preferred_element_type=F32)


def _dot_tn(a, b):
    return lax.dot_general(a.astype(BF16), b.astype(BF16), (((0,), (0,)), ((), ())),
                           preferred_element_type=F32)


def _split(x, n):
    parts = []
    for i in range(n):
        piece = x.astype(BF16)
        parts.append(piece)
        if i + 1 < n:
            x = x - piece.astype(F32)
    return parts


def _dot_exact_lhs(m_bf16, x, n=2):
    return sum(jnp.dot(m_bf16, part, preferred_element_type=F32) for part in _split(x, n))


def _dot_exact_rhs(x, m_bf16, n=2):
    return sum(jnp.dot(part, m_bf16, preferred_element_type=F32) for part in _split(x, n))


def _rms(x, g):
    return x * lax.rsqrt(jnp.mean(x * x, axis=-1, keepdims=True) + RMS_EPS) * g


def _sigmoid(x):
    return 1.0 / (1.0 + jnp.exp(-x))


def _silu(x):
    return x * _sigmoid(x)


def _softplus(x):
    return jnp.maximum(x, 0.0) + jnp.log(1.0 + jnp.exp(-jnp.abs(x)))


def _iota(shape, dim):
    return lax.broadcasted_iota(jnp.int32, shape, dim)


def _shift_rows(x, n, carry8):
    rolled = pltpu.roll(x, n, 0)
    head = jnp.where(_iota((SUBLANES, x.shape[1]), 0) < n, pltpu.roll(carry8, n, 0),
                     rolled[:SUBLANES])
    return jnp.concatenate([head, rolled[SUBLANES:]], axis=0)


def _seg_sum(x, e_ref, et_ref, n_in=2):
    s = _dot_exact_rhs(x, e_ref[...], n_in)
    return _dot_exact_rhs(s, et_ref[...])


def _block_cumsum(x, blk):
    rows, width = x.shape
    r = _iota((rows, rows), 0)
    c = _iota((rows, rows), 1)
    m_low = jnp.where(((r // blk) == (c // blk)) & (c <= r), 1.0, 0.0).astype(BF16)
    cum = _dot_exact_lhs(m_low, x)
    cum3 = cum.reshape(rows // blk, blk, width)
    total = jnp.broadcast_to(cum3[:, blk - 1:blk, :], cum3.shape).reshape(rows, width)
    return cum, total


def _rwkv_pointwise(xm, w0, lora_w, a0, k_k, k_a, e_ref, et_ref):
    r = xm[:, 0:D_MODEL]
    k = xm[:, D_MODEL:2 * D_MODEL]
    v = xm[:, 2 * D_MODEL:3 * D_MODEL]
    da = xm[:, 3 * D_MODEL:3 * D_MODEL + 2 * LORA]
    zr = xm[:, 3 * D_MODEL + 2 * LORA:]
    lane = _iota(da.shape, 1)
    lora_in = jnp.where(lane < LORA, jnp.tanh(da), da)
    lo = _dot(lora_in, lora_w)
    w_logit = w0 + lo[:, :D_MODEL]
    w_log = -_softplus(-w_logit) - 0.5
    logd = -jnp.exp(w_log)
    a = _sigmoid(a0 + lo[:, D_MODEL:])
    kkr = k * k_k
    kk = kkr * jnp.minimum(lax.rsqrt(_seg_sum(kkr * kkr, e_ref, et_ref)), 1e12)
    k_h = k * (1.0 + (a - 1.0) * k_a)
    b = kk * a
    return r, k_h, v, kk, b, logd, zr


def _rwkv_post(ys, r, k_h, v, zr, r_k, ln_w, ln_b, e_ref, et_ref):
    inv_n = 1.0 / HEAD_DIM
    mu = _seg_sum(ys, e_ref, et_ref) * inv_n
    yc = ys - mu
    var = _seg_sum(yc * yc, e_ref, et_ref) * inv_n
    yn = yc * lax.rsqrt(var + GN_EPS) * ln_w + ln_b
    bonus = _seg_sum(r * k_h * r_k, e_ref, et_ref) * v
    return (yn + bonus) * _silu(zr)


def _scan_operands(r, k_h, v, kk, b, logd, cum, cum_c):
    enc = jnp.exp(-cum)
    e_c = jnp.exp(cum_c - cum)
    khat = kk * jnp.exp(cum - logd)
    rhat = r * jnp.exp(cum)
    btil = b * enc
    ktil = k_h * enc
    bchk = -(b * e_c)
    kchk = k_h * e_c
    g_c = jnp.exp(cum_c)
    return khat, rhat, btil, ktil, bchk, kchk, g_c


def _chunk_transfer(units, m_bd, m_sl, m_l):
    two = lambda x: jnp.concatenate([x, x], axis=0)
    fold = lambda x: x[:CHUNK] + x[CHUNK:]
    expand = lambda x: jnp.where(m_bd, two(x), 0.0)
    gs = [_dot_nt(jnp.concatenate([kh, rh], axis=0),
                  jnp.concatenate([expand(bt), expand(kt)], axis=0))
          for kh, rh, bt, kt, _, _, _ in units]
    ps = [jnp.where(m_sl, -two(g[:CHUNK, :PAIR]), 0.0) for g in gs]
    m_aks = [jnp.where(m_sl, two(g[:CHUNK, PAIR:]), 0.0) for g in gs]
    m_rbs = [jnp.where(m_l, -two(g[CHUNK:, :PAIR]), 0.0) for g in gs]
    m_rks = [jnp.where(m_l, two(g[CHUNK:, PAIR:]), 0.0) for g in gs]
    vxs = [expand(u[6]) for u in units]
    zs = [_dot(jnp.concatenate([m_ak, m_rk], axis=0), vx)
          for m_ak, m_rk, vx in zip(m_aks, m_rks, vxs)]
    xs = [jnp.concatenate([expand(u[0]), z[:PAIR]], axis=1) for u, z in zip(units, zs)]
    for j in range(6):
        xs = [x + _dot(p, x) for p, x in zip(ps, xs)]
        if j < 5:
            ps = [_dot(p, p) for p in ps]
    t1s = [_dot(m_rb, x) for m_rb, x in zip(m_rbs, xs)]
    out = []
    p_ts = [_dot_tn(fold(x[:, :PAIR]), u[4]) for x, u in zip(xs, units)]
    q_ts = [_dot_tn(jnp.concatenate([fold(x[:, PAIR:]), u[6]], axis=0),
                    jnp.concatenate([u[4], u[5]], axis=0)) for x, u in zip(xs, units)]
    for u, z, t1, p_t, q_t in zip(units, zs, t1s, p_ts, q_ts):
        y_w = u[1] + fold(t1[:, :PAIR])
        y_c = fold(t1[:, PAIR:] + z[PAIR:])
        out.append((y_w, y_c, jnp.where(m_bd, p_t, 0.0), jnp.where(m_bd, q_t, 0.0)))
    return out


def _rwkv_prompt_kernel(x_ref, g_ref, w_ref, mu_ref, w0_ref, lw_ref, a0_ref, kk_ref, ka_ref,
                        rk_ref, lnw_ref, lnb_ref, e_ref, et_ref,
                        yr_ref, sfin_ref,
                        carry_ref, st_ref, ops_ref, ys_ref, *, tm):
    s_idx = pl.program_id(1)

    @pl.when(s_idx == 0)
    def _():
        carry_ref[...] = jnp.zeros_like(carry_ref)
        st_ref[...] = jnp.zeros_like(st_ref)

    h = _rms(x_ref[0], g_ref[...]).astype(BF16)
    p = jnp.dot(h, w_ref[...], preferred_element_type=F32)
    prev = _shift_rows(p, 1, carry_ref[...])
    carry_ref[...] = p[tm - SUBLANES:, :]
    xm = p + (prev - p) * mu_ref[...]
    r, k_h, v, kk, b, logd, zr = _rwkv_pointwise(xm, w0_ref[...], lw_ref[...], a0_ref[...],
                                                 kk_ref[...], ka_ref[...], e_ref, et_ref)
    cum, cum_c = _block_cumsum(logd, CHUNK)
    ops = _scan_operands(r, k_h, v, kk, b, logd, cum, cum_c) + (v,)
    for i, arr in enumerate(ops):
        for hp in range(N_PAIRS):
            ops_ref[i, hp] = arr[:, hp * PAIR:(hp + 1) * PAIR]

    rr = _iota((PAIR, PAIR), 0)
    cc = _iota((PAIR, PAIR), 1)
    m_bd = (rr // HEAD_DIM) == (cc // HEAD_DIM)
    m_sl = m_bd & ((cc % HEAD_DIM) < (rr % HEAD_DIM))
    m_l = m_bd & ((cc % HEAD_DIM) <= (rr % HEAD_DIM))

    n_chunks = tm // CHUNK

    def group_body(gi, carry):
        hps = [gi * PAIRS_PER_STEP + i for i in range(PAIRS_PER_STEP)]
        rows = [pl.ds(c * CHUNK, CHUNK) for c in range(n_chunks)]
        units = [tuple(ops_ref[i, hp, rw, :] for i in (0, 1, 2, 3, 4, 5, 7))
                 for hp in hps for rw in rows]
        tr = _chunk_transfer(units, m_bd, m_sl, m_l)
        s_pairs = [st_ref[hp] for hp in hps]
        for c in range(n_chunks):
            for i, hp in enumerate(hps):
                y_w, y_c, p_t, q_t = tr[i * n_chunks + c]
                g_c = ops_ref[6, hp, pl.ds(c * CHUNK, 1), :]
                s0 = s_pairs[i]
                ys_ref[hp, rows[c], :] = _dot_nt(y_w, s0) + y_c
                s_pairs[i] = s0 * g_c + _dot(s0, p_t) + q_t
        for i, hp in enumerate(hps):
            st_ref[hp] = s_pairs[i]
        return carry

    lax.fori_loop(0, N_PAIRS // PAIRS_PER_STEP, group_body, 0)

    ys = jnp.concatenate([ys_ref[hp] for hp in range(N_PAIRS)], axis=1)
    y_r = _rwkv_post(ys, r, k_h, v, zr, rk_ref[...], lnw_ref[...], lnb_ref[...], e_ref, et_ref)
    yr_ref[0] = y_r.astype(BF16)

    @pl.when(s_idx == pl.num_programs(1) - 1)
    def _():
        for hp in range(N_PAIRS):
            sp = st_ref[hp]
            sfin_ref[0, 2 * hp] = sp[:HEAD_DIM, :HEAD_DIM]
            sfin_ref[0, 2 * hp + 1] = sp[HEAD_DIM:, HEAD_DIM:]


def _const_spec(shape):
    nd = len(shape)
    return pl.BlockSpec(shape, lambda *_: (0,) * nd)


def _rwkv_prompt(x, wts, tm=256):
    bsz, seq, _ = x.shape
    grid = (bsz, seq // tm)
    row = lambda a: a.reshape(1, -1)
    consts = [row(wts["norm_g"]), wts["w_rw"], row(wts["mu_rw"]), row(wts["w0"]), wts["lora_w"],
              row(wts["a0"]), row(wts["k_k"]), row(wts["k_a"]), row(wts["r_k"]),
              row(wts["ln_w"]), row(wts["ln_b"]), wts["e"], wts["et"]]
    kern = functools.partial(_rwkv_prompt_kernel, tm=tm)
    return pl.pallas_call(
        kern,
        grid=grid,
        in_specs=[pl.BlockSpec((1, tm, D_MODEL), lambda b, s: (b, s, 0))]
        + [_const_spec(c.shape) for c in consts],
        out_specs=[pl.BlockSpec((1, tm, D_MODEL), lambda b, s: (b, s, 0)),
                   pl.BlockSpec((1, N_HEADS, HEAD_DIM, HEAD_DIM), lambda b, s: (b, 0, 0, 0))],
        out_shape=[jax.ShapeDtypeStruct((bsz, seq, D_MODEL), BF16),
                   jax.ShapeDtypeStruct((bsz, N_HEADS, HEAD_DIM, HEAD_DIM), F32)],
        scratch_shapes=[pltpu.VMEM((SUBLANES, N_RW), F32),
                        pltpu.VMEM((N_PAIRS, PAIR, PAIR), F32),
                        pltpu.VMEM((8, N_PAIRS, tm, PAIR), F32),
                        pltpu.VMEM((N_PAIRS, tm, PAIR), F32)],
        compiler_params=pltpu.CompilerParams(
            dimension_semantics=("arbitrary", "arbitrary"), vmem_limit_bytes=VMEM_LIMIT),
        name="rwkv_prompt",
    )(x, *consts)


def _conv_kernel(*refs, tm, sample, seq_len=0):
    if sample:
        (x_ref, g_ref, w_ref, cw_ref, cb0_ref, cb1_ref, yc_ref, u_ref, h_ref) = refs
    else:
        (x_ref, g_ref, w_ref, cw_ref, yc_ref, u_ref, h_ref, carry_ref) = refs
        s_idx = pl.program_id(1)

        @pl.when(s_idx == 0)
        def _():
            carry_ref[...] = jnp.zeros_like(carry_ref)

    x = x_ref[0] if not sample else x_ref[...]
    hf = _rms(x, g_ref[...])
    p = jnp.dot(hf.astype(BF16), w_ref[...], preferred_element_type=F32)
    xin = p[:, 0:D_MODEL]
    bg = p[:, D_MODEL:2 * D_MODEL]
    cg = p[:, 2 * D_MODEL:3 * D_MODEL]
    zc = p[:, 3 * D_MODEL:]
    u = cg * xin
    if sample:
        tpos = _iota(u.shape, 0) % seq_len
        u1 = jnp.where(tpos == 0, cb1_ref[...], pltpu.roll(u, 1, 0))
        u2 = jnp.where(tpos == 0, cb0_ref[...],
                       jnp.where(tpos == 1, cb1_ref[...], pltpu.roll(u, 2, 0)))
        u_ref[...] = u
        h_ref[...] = hf
    else:
        carry = carry_ref[...]
        u1 = _shift_rows(u, 1, carry)
        u2 = _shift_rows(u, 2, carry)
        carry_ref[...] = u[tm - SUBLANES:, :]
        u_ref[0] = u[tm - SUBLANES:, :]
        h_ref[0] = hf[tm - SUBLANES:, :]
    cw = cw_ref[...]
    conv = cw[0:1, :] * u2 + cw[1:2, :] * u1 + cw[2:3, :] * u
    y_c = bg * conv * _silu(zc)
    if sample:
        yc_ref[...] = y_c.astype(BF16)
    else:
        yc_ref[0] = y_c.astype(BF16)


def _conv_prompt(x, wts, tm=512):
    bsz, seq, _ = x.shape
    consts = [wts["norm_g"].reshape(1, -1), wts["w_conv"], wts["conv_w"]]
    tail = pl.BlockSpec((1, SUBLANES, D_MODEL), lambda b, s: (b, 0, 0))
    return pl.pallas_call(
        functools.partial(_conv_kernel, tm=tm, sample=False),
        grid=(bsz, seq // tm),
        in_specs=[pl.BlockSpec((1, tm, D_MODEL), lambda b, s: (b, s, 0))]
        + [_const_spec(c.shape) for c in consts],
        out_specs=[pl.BlockSpec((1, tm, D_MODEL), lambda b, s: (b, s, 0)), tail, tail],
        out_shape=[jax.ShapeDtypeStruct((bsz, seq, D_MODEL), BF16),
                   jax.ShapeDtypeStruct((bsz, SUBLANES, D_MODEL), F32),
                   jax.ShapeDtypeStruct((bsz, SUBLANES, D_MODEL), F32)],
        scratch_shapes=[pltpu.VMEM((SUBLANES, D_MODEL), F32)],
        compiler_params=pltpu.CompilerParams(
            dimension_semantics=("arbitrary", "arbitrary"), vmem_limit_bytes=VMEM_LIMIT),
        name="conv_prompt",
    )(x, *consts)


def _conv_sample(x2d, cb0, cb1, wts, seq_len, tm=256):
    rows = x2d.shape[0]
    tm = min(tm, rows)
    consts = [wts["norm_g"].reshape(1, -1), wts["w_conv"], wts["conv_w"]]
    blk = pl.BlockSpec((tm, D_MODEL), lambda i: (i, 0))
    return pl.pallas_call(
        functools.partial(_conv_kernel, tm=tm, sample=True, seq_len=seq_len),
        grid=(rows // tm,),
        in_specs=[blk] + [_const_spec(c.shape) for c in consts] + [blk, blk],
        out_specs=[blk, blk, blk],
        out_shape=[jax.ShapeDtypeStruct((rows, D_MODEL), BF16),
                   jax.ShapeDtypeStruct((rows, D_MODEL), F32),
                   jax.ShapeDtypeStruct((rows, D_MODEL), F32)],
        compiler_params=pltpu.CompilerParams(
            dimension_semantics=("arbitrary",), vmem_limit_bytes=VMEM_LIMIT),
        name="conv_sample",
    )(x2d, *consts, cb0, cb1)


def _out_kernel(x_ref, yc_ref, yr_ref, g_ref, wg_ref, woc_ref, wor_ref, wo_ref, fg_ref, y_ref):
    x = x_ref[...]
    h = _rms(x, g_ref[...]).astype(BF16)
    gates = jnp.dot(h, wg_ref[...], preferred_element_type=F32)
    pc = jnp.dot(yc_ref[...], woc_ref[...], preferred_element_type=F32)
    pr = jnp.dot(yr_ref[...], wor_ref[...], preferred_element_type=F32)
    m = _sigmoid(gates[:, :D_MODEL]) * pc + _sigmoid(gates[:, D_MODEL:]) * pr
    out = jnp.dot(m.astype(BF16), wo_ref[...], preferred_element_type=F32)
    y_ref[...] = _rms(x + out, fg_ref[...])


def _out_stage(x2d, yc, yr, wts, tm=512):
    rows = x2d.shape[0]
    tm = min(tm, rows)
    consts = [wts["norm_g"].reshape(1, -1), wts["w_gate"], wts["w_out_c"], wts["w_out_r"],
              wts["w_o"], wts["final_g"].reshape(1, -1)]
    blk = pl.BlockSpec((tm, D_MODEL), lambda i: (i, 0))
    return pl.pallas_call(
        _out_kernel,
        grid=(rows // tm,),
        in_specs=[blk, blk, blk] + [_const_spec(c.shape) for c in consts],
        out_specs=blk,
        out_shape=jax.ShapeDtypeStruct((rows, D_MODEL), F32),
        compiler_params=pltpu.CompilerParams(
            dimension_semantics=("arbitrary",), vmem_limit_bytes=VMEM_LIMIT),
        name="out_stage",
    )(x2d, yc, yr, *consts)


SEQ_PAD = SUBLANES


SEQS_PER_STEP = 2


def _rwkv_sample_kernel(x_ref, xp_ref, hl_ref, s0_ref, g_ref, w_ref, mu_ref, w0_ref, lw_ref,
                        a0_ref, kk_ref, ka_ref, rk_ref, lnw_ref, lnb_ref, e_ref, et_ref,
                        yr_ref, s1_ref,
                        xk_ref, z_ref, uv_ref, bk_ref, gc_ref, *, nseq, seq_len):
    rows = nseq * SEQ_PAD
    g = g_ref[...]
    tpos = _iota((rows, 1), 0) % SEQ_PAD
    h = _rms(x_ref[...], g).astype(BF16)
    hprev = jnp.where(tpos == 0, hl_ref[...], _rms(xp_ref[...], g)).astype(BF16)
    p = jnp.dot(h, w_ref[...], preferred_element_type=F32)
    pp = jnp.dot(hprev, w_ref[...], preferred_element_type=F32)
    xm = p + (pp - p) * mu_ref[...]
    r, k_h, v, kk, b, logd, zr = _rwkv_pointwise(xm, w0_ref[...], lw_ref[...], a0_ref[...],
                                                 kk_ref[...], ka_ref[...], e_ref, et_ref)
    valid = tpos < seq_len
    logd = jnp.where(valid, logd, 0.0)
    cum, cum_c = _block_cumsum(logd, SEQ_PAD)
    khat, rhat, btil, ktil, bchk, kchk, g_c = _scan_operands(r, k_h, v, kk, b, logd, cum, cum_c)

    def sh(x, j):
        return x if j == 0 else pltpu.roll(x, j, 0)

    def up4(x):
        return pltpu.roll(x, rows - seq_len, 0)

    xk_ref[...] = jnp.where(valid, khat, sh(rhat, seq_len))

    head_lanes = lambda hd: slice(hd * HEAD_DIM, (hd + 1) * HEAD_DIM)

    def seq_rows(it, j):
        return it * SEQS_PER_STEP + j, pl.ds(pl.multiple_of((it * SEQS_PER_STEP + j) * SEQ_PAD,
                                                            SEQ_PAD), SEQ_PAD)

    def phase1(it, carry):
        for j in range(SEQS_PER_STEP):
            i, r8 = seq_rows(it, j)
            x8 = xk_ref[r8, :]
            for hd in range(N_HEADS):
                z_ref[r8, head_lanes(hd)] = _dot_nt(x8[:, head_lanes(hd)], s0_ref[i, hd])
        return carry

    lax.fori_loop(0, nseq // SEQS_PER_STEP, phase1, 0)

    z = z_ref[...]
    z_r = up4(z)
    pairs = ([(khat, ktil, j) for j in range(1, seq_len)]
             + [(khat, btil, j) for j in range(1, seq_len)]
             + [(rhat, ktil, j) for j in range(seq_len)]
             + [(rhat, btil, j) for j in range(seq_len)])
    dots = _seg_sum(jnp.concatenate([x * sh(y, j) for x, y, j in pairs], axis=0),
                    e_ref, et_ref, n_in=1)
    dots = [dots[n * rows:(n + 1) * rows] for n in range(len(pairs))]
    n1 = seq_len - 1
    m_ak, m_ab = [None] + dots[:n1], [None] + dots[n1:2 * n1]
    m_rk, m_rb = dots[2 * n1:2 * n1 + seq_len], dots[2 * n1 + seq_len:]

    u = z
    for j in range(1, seq_len):
        u = u + jnp.where(tpos >= j, m_ak[j] * sh(v, j), 0.0)
    for t in range(1, seq_len):
        acc = u
        for j in range(1, t + 1):
            acc = acc - m_ab[j] * sh(u, j)
        u = jnp.where(tpos == t, acc, u)
    y = z_r
    for j in range(seq_len):
        y = y + jnp.where(tpos >= j, m_rk[j] * sh(v, j) - m_rb[j] * sh(u, j), 0.0)
    y_r = _rwkv_post(y, r, k_h, v, zr, rk_ref[...], lnw_ref[...], lnb_ref[...], e_ref, et_ref)
    yr_ref[...] = y_r.astype(BF16)

    uv_ref[...] = jnp.where(valid, u, sh(v, seq_len))
    bk_ref[...] = jnp.where(valid, bchk, sh(kchk, seq_len))
    gc_ref[...] = g_c

    def phase3(it, carry):
        for j in range(SEQS_PER_STEP):
            i, r8 = seq_rows(it, j)
            uv_t = uv_ref[r8, :].T
            bk8 = bk_ref[r8, :]
            gc1 = gc_ref[r8, :][0:1, :]
            for hd in range(N_HEADS):
                upd = _dot(uv_t[hd * HEAD_DIM:(hd + 1) * HEAD_DIM, :], bk8[:, head_lanes(hd)])
                s1_ref[i, hd] = s0_ref[i, hd] * gc1[:, head_lanes(hd)] + upd
        return carry

    lax.fori_loop(0, nseq // SEQS_PER_STEP, phase3, 0)


def _rwkv_sample(x_pad, xprev_pad, hlast_pad, s0, wts, seq_len, nseq=8):
    rows_total = x_pad.shape[0]
    n_all = rows_total // SEQ_PAD
    nseq = min(nseq, n_all)
    rows = nseq * SEQ_PAD
    row = lambda a: a.reshape(1, -1)
    consts = [row(wts["norm_g"]), wts["w_rw"], row(wts["mu_rw"]), row(wts["w0"]), wts["lora_w"],
              row(wts["a0"]), row(wts["k_k"]), row(wts["k_a"]), row(wts["r_k"]),
              row(wts["ln_w"]), row(wts["ln_b"]), wts["e"], wts["et"]]
    blk = pl.BlockSpec((rows, D_MODEL), lambda i: (i, 0))
    sblk = pl.BlockSpec((nseq, N_HEADS, HEAD_DIM, HEAD_DIM), lambda i: (i, 0, 0, 0))
    return pl.pallas_call(
        functools.partial(_rwkv_sample_kernel, nseq=nseq, seq_len=seq_len),
        grid=(n_all // nseq,),
        in_specs=[blk, blk, blk, sblk] + [_const_spec(c.shape) for c in consts],
        out_specs=[blk, sblk],
        out_shape=[jax.ShapeDtypeStruct((rows_total, D_MODEL), BF16),
                   jax.ShapeDtypeStruct(s0.shape, F32)],
        scratch_shapes=[pltpu.VMEM((rows, D_MODEL), F32) for _ in range(5)],
        compiler_params=pltpu.CompilerParams(
            dimension_semantics=("arbitrary",), vmem_limit_bytes=VMEM_LIMIT),
        name="rwkv_sample",
    )(x_pad, xprev_pad, hlast_pad, s0, *consts)


def _prep_weights(norm_g, w_in, conv_w, mu_shift, w0, w_up, a0, a_up, k_k, k_a, r_k, ln_w, ln_b,
                  w_out_c, w_out_r, w_o, final_g):
    d = D_MODEL
    c0 = 4 * d
    zeros = jnp.zeros((LORA, d), F32)
    lora_w = jnp.concatenate([jnp.concatenate([w_up, zeros], axis=1),
                              jnp.concatenate([zeros, a_up], axis=1)], axis=0)
    head_of_lane = jnp.arange(d) // HEAD_DIM
    e = (head_of_lane[:, None] == jnp.arange(LANES)[None, :]).astype(BF16)
    return dict(
        norm_g=norm_g, final_g=final_g, conv_w=conv_w,
        w_conv=w_in[:, :c0].astype(BF16),
        w_rw=w_in[:, c0:c0 + N_RW].astype(BF16),
        mu_rw=mu_shift,
        w_gate=w_in[:, c0 + N_RW:].astype(BF16),
        w0=w0, a0=a0, lora_w=lora_w.astype(BF16), k_k=k_k, k_a=k_a, r_k=r_k.reshape(-1),
        ln_w=ln_w, ln_b=ln_b,
        w_out_c=w_out_c.astype(BF16), w_out_r=w_out_r.astype(BF16), w_o=w_o.astype(BF16),
        e=e, et=e.T)


def _layer_prompt(x, wts):
    bsz, seq, d = x.shape
    yc, u_tail, h_tail = _conv_prompt(x, wts)
    yr, s_fin = _rwkv_prompt(x, wts)
    y = _out_stage(x.reshape(bsz * seq, d), yc.reshape(bsz * seq, d), yr.reshape(bsz * seq, d), wts)
    return (y.reshape(bsz, seq, d), u_tail[:, SUBLANES - 2:, :], h_tail[:, SUBLANES - 1, :], s_fin)


def _layer_sample(x, conv_buf, h_last, s0, wts):
    bsz, seq, d = x.shape
    rows = bsz * seq
    x2d = x.reshape(rows, d)
    rep_rows = lambda a: jnp.broadcast_to(a[:, None, :], (bsz, seq, d)).reshape(rows, d)
    yc, u, h = _conv_sample(x2d, rep_rows(conv_buf[:, 0]), rep_rows(conv_buf[:, 1]), wts, seq)
    pad = lambda a: jnp.pad(a, ((0, 0), (0, SEQ_PAD - seq), (0, 0))).reshape(bsz * SEQ_PAD, d)
    x_prev = jnp.concatenate([jnp.zeros((bsz, 1, d), x.dtype), x[:, :-1]], axis=1)
    hl = jnp.broadcast_to(h_last[:, None, :], (bsz, seq, d))
    yr_pad, s1 = _rwkv_sample(pad(x), pad(x_prev), pad(hl), s0, wts, seq)
    yr = yr_pad.reshape(bsz, SEQ_PAD, d)[:, :seq].reshape(rows, d)
    y = _out_stage(x2d, yc, yr, wts)
    u3 = u.reshape(bsz, seq, d)
    return (y.reshape(bsz, seq, d), u3[:, seq - 2:], h.reshape(bsz, seq, d)[:, seq - 1], s1)


def kernel(x_prompt, x_sample, state_conv, state_shift, state_rwkv, norm_g, w_in, conv_w, mu_shift,
           w0, w_up, a0, a_up, k_k, k_a, r_k, ln_w, ln_b, w_out_c, w_out_r, w_o, final_g):
    assert norm_g.shape[0] == 1, "single-layer step"
    wts = _prep_weights(norm_g[0], w_in[0], conv_w[0], mu_shift[0], w0[0], w_up[0], a0[0], a_up[0],
                        k_k[0], k_a[0], r_k[0], ln_w[0], ln_b[0], w_out_c[0], w_out_r[0], w_o[0],
                        final_g)
    y_p, c_p, s_p, r_p = _layer_prompt(x_prompt, wts)
    y_s, c_s, s_s, r_s = _layer_sample(x_sample, state_conv[0], state_shift[0], state_rwkv[0], wts)
    lead = lambda a: a[None]
    return (y_p, y_s, lead(c_p), lead(s_p), lead(r_p), lead(c_s), lead(s_s), lead(r_s))
```

```python
import functools
import math

import jax
import jax.numpy as jnp
from jax import lax
from jax.experimental import pallas as pl
from jax.experimental.pallas import tpu as pltpu

F32 = jnp.float32
BF16 = jnp.bfloat16

D_MODEL = 1024
HEAD_DIM = 64
N_HEADS = D_MODEL // HEAD_DIM
LORA = 64
RMS_EPS = 1e-6
GN_EPS = 64e-5
EXP_M05 = math.exp(-0.5)
LANES = 128
SUBLANES = 8
PAIR = 2 * HEAD_DIM
N_PAIRS = D_MODEL // PAIR
CHUNK = 64
PAIRS_PER_STEP = 4
N_RW = 4 * D_MODEL + 2 * LORA
VMEM_LIMIT = 56 * 1024 * 1024


def _dot(a, b):
    return jnp.dot(a.astype(BF16), b.astype(BF16), preferred_element_type=F32)


def _dot_nt(a, b):
    return lax.dot_general(a.astype(BF16), b.astype(BF16), (((1,), (1,)), ((), ())),
                           preferred_element_type=F32)


def _dot_tn(a, b):
    return lax.dot_general(a.astype(BF16), b.astype(BF16), (((0,), (0,)), ((), ())),
                           preferred_element_type=F32)


def _split(x, n):
    parts = []
    for i in range(n):
        piece = x.astype(BF16)
        parts.append(piece)
        if i + 1 < n:
            x = x - piece.astype(F32)
    return parts


def _dot_exact_lhs(m_bf16, x, n=2):
    return sum(jnp.dot(m_bf16, part, preferred_element_type=F32) for part in _split(x, n))


def _dot_exact_rhs(x, m_bf16, n=2):
    return sum(jnp.dot(part, m_bf16, preferred_element_type=F32) for part in _split(x, n))


def _rms(x, g):
    return x * lax.rsqrt(jnp.mean(x * x, axis=-1, keepdims=True) + RMS_EPS) * g


def _sigmoid(x):
    return 1.0 / (1.0 + jnp.exp(-x))


def _silu(x):
    return x * _sigmoid(x)


def _iota(shape, dim):
    return lax.broadcasted_iota(jnp.int32, shape, dim)


def _shift_rows(x, n, carry8):
    rolled = pltpu.roll(x, n, 0)
    head = jnp.where(_iota((SUBLANES, x.shape[1]), 0) < n, pltpu.roll(carry8, n, 0),
                     rolled[:SUBLANES])
    return jnp.concatenate([head, rolled[SUBLANES:]], axis=0)


def _head_sums(x, e_ref, n_in=2):
    return _dot_exact_rhs(x, e_ref[...], n_in)


def _head_bcast(s, et_ref):
    return _dot_exact_rhs(s, et_ref[...])


def _seg_sum(x, e_ref, et_ref, n_in=2):
    return _head_bcast(_head_sums(x, e_ref, n_in), et_ref)


def _block_cumsum(x, blk):
    rows, width = x.shape
    r = _iota((rows, rows), 0)
    c = _iota((rows, rows), 1)
    m_low = jnp.where(((r // blk) == (c // blk)) & (c <= r), 1.0, 0.0).astype(BF16)
    cum3 = _dot_exact_lhs(m_low, x).reshape(rows // blk, blk, width)
    return cum3, cum3[:, blk - 1:blk, :]


def _rwkv_pointwise(xm, w0, lora_w, a0, k_k, k_a, e_ref, et_ref):
    r = xm[:, 0:D_MODEL]
    k = xm[:, D_MODEL:2 * D_MODEL]
    v = xm[:, 2 * D_MODEL:3 * D_MODEL]
    da = xm[:, 3 * D_MODEL:3 * D_MODEL + 2 * LORA]
    zr = xm[:, 3 * D_MODEL + 2 * LORA:]
    lane = _iota(da.shape, 1)
    lora_in = jnp.where(lane < LORA, jnp.tanh(da), da)
    lo = _dot(lora_in, lora_w)
    w_logit = w0 + lo[:, :D_MODEL]
    logd = -EXP_M05 * _sigmoid(w_logit)
    a = _sigmoid(a0 + lo[:, D_MODEL:])
    kkr = k * k_k
    inv_nrm = jnp.minimum(lax.rsqrt(_head_sums(kkr * kkr, e_ref)), 1e12)
    kk = kkr * _head_bcast(inv_nrm, et_ref)
    k_h = k * (1.0 + (a - 1.0) * k_a)
    b = kk * a
    return r, k_h, v, kk, b, logd, zr


def _rwkv_post(ys, r, k_h, v, zr, r_k, ln_w, ln_b, e_ref, et_ref):
    inv_n = 1.0 / HEAD_DIM
    yc = ys - _head_bcast(_head_sums(ys, e_ref) * inv_n, et_ref)
    rstd = lax.rsqrt(_head_sums(yc * yc, e_ref) * inv_n + GN_EPS)
    yn = yc * _head_bcast(rstd, et_ref) * ln_w + ln_b
    bonus = _seg_sum(r * k_h * r_k, e_ref, et_ref) * v
    return (yn + bonus) * _silu(zr)


def _scan_operands(r, k_h, v, kk, b, logd, cum3, tot3):
    shape = cum3.shape
    flat = lambda x: x.reshape(shape[0] * shape[1], shape[2])
    cum = flat(cum3)
    g_c3 = jnp.exp(tot3)
    enc3 = jnp.exp(-cum3)
    enc = flat(enc3)
    e_c = flat(enc3 * g_c3)
    khat = kk * jnp.exp(cum - logd)
    rhat = r * jnp.exp(cum)
    btil = b * enc
    ktil = k_h * enc
    bchk = -(b * e_c)
    kchk = k_h * e_c
    g_c = flat(jnp.broadcast_to(g_c3, shape))
    return khat, rhat, btil, ktil, bchk, kchk, g_c


def _chunk_transfer(units, m_bd, m_sl, m_l):
    m_bd2 = jnp.concatenate([m_bd, m_bd], axis=1)

    def expand(x):
        x2 = jnp.concatenate([x, x], axis=0)
        return jnp.where(m_bd if x.shape[1] == PAIR else m_bd2, x2, 0.0)

    gs = [_dot_nt(jnp.concatenate([kh, rh], axis=0),
                  jnp.concatenate([expand(bt), expand(kt)], axis=0))
          for kh, rh, bt, kt, _, _, _ in units]
    ps = [jnp.where(m_sl, -g[:CHUNK, :PAIR], 0.0) for g in gs]
    m_as = [jnp.concatenate([jnp.where(m_sl, g[:CHUNK, PAIR:], 0.0),
                             jnp.where(m_l, g[CHUNK:, PAIR:], 0.0)], axis=0) for g in gs]
    m_rbs = [jnp.where(m_l, -g[CHUNK:, :PAIR], 0.0) for g in gs]
    zs = [_dot(m_a, expand(u[6])) for m_a, u in zip(m_as, units)]
    xs = [jnp.concatenate([u[0], z[:CHUNK]], axis=1) for u, z in zip(units, zs)]
    for j in range(6):
        xs = [x + _dot(p, expand(x)) for p, x in zip(ps, xs)]
        if j < 5:
            ps = [_dot(p, expand(p)) for p in ps]
    t1s = [_dot(m_rb, expand(x)) for m_rb, x in zip(m_rbs, xs)]
    p_ts = [_dot_tn(x[:, :PAIR], u[4]) for x, u in zip(xs, units)]
    q_ts = [_dot_tn(jnp.concatenate([x[:, PAIR:], u[6]], axis=0),
                    jnp.concatenate([u[4], u[5]], axis=0)) for x, u in zip(xs, units)]
    out = []
    for u, z, t1, p_t, q_t in zip(units, zs, t1s, p_ts, q_ts):
        y_w = u[1] + t1[:, :PAIR]
        y_c = t1[:, PAIR:] + z[CHUNK:]
        out.append((y_w, y_c, jnp.where(m_bd, p_t, 0.0), jnp.where(m_bd, q_t, 0.0)))
    return out


def _rwkv_prompt_kernel(x_ref, g_ref, w_ref, mu_ref, w0_ref, lw_ref, a0_ref, kk_ref, ka_ref,
                        rk_ref, lnw_ref, lnb_ref, e_ref, et_ref,
                        yr_ref, sfin_ref,
                        carry_ref, st_ref, ops_ref, ys_ref, *, tm):
    s_idx = pl.program_id(1)

    @pl.when(s_idx == 0)
    def _():
        carry_ref[...] = jnp.zeros_like(carry_ref)
        st_ref[...] = jnp.zeros_like(st_ref)

    h = _rms(x_ref[0], g_ref[...]).astype(BF16)
    p = jnp.dot(h, w_ref[...], preferred_element_type=F32)
    prev = _shift_rows(p, 1, carry_ref[...])
    carry_ref[...] = p[tm - SUBLANES:, :]
    xm = p + (prev - p) * mu_ref[...]
    r, k_h, v, kk, b, logd, zr = _rwkv_pointwise(xm, w0_ref[...], lw_ref[...], a0_ref[...],
                                                 kk_ref[...], ka_ref[...], e_ref, et_ref)
    cum, cum_c = _block_cumsum(logd, CHUNK)
    ops = _scan_operands(r, k_h, v, kk, b, logd, cum, cum_c) + (v,)
    for i, arr in enumerate(ops):
        for hp in range(N_PAIRS):
            ops_ref[i, hp] = arr[:, hp * PAIR:(hp + 1) * PAIR]

    rr = _iota((PAIR, PAIR), 0)
    cc = _iota((PAIR, PAIR), 1)
    m_bd = (rr // HEAD_DIM) == (cc // HEAD_DIM)
    t_row = _iota((CHUNK, PAIR), 0)
    s_col = _iota((CHUNK, PAIR), 1) % HEAD_DIM
    m_sl = s_col < t_row
    m_l = s_col <= t_row

    n_chunks = tm // CHUNK

    def group_body(gi, carry):
        hps = [gi * PAIRS_PER_STEP + i for i in range(PAIRS_PER_STEP)]
        rows = [pl.ds(c * CHUNK, CHUNK) for c in range(n_chunks)]
        units = [tuple(ops_ref[i, hp, rw, :] for i in (0, 1, 2, 3, 4, 5, 7))
                 for hp in hps for rw in rows]
        tr = _chunk_transfer(units, m_bd, m_sl, m_l)
        s_pairs = [st_ref[hp] for hp in hps]
        for c in range(n_chunks):
            for i, hp in enumerate(hps):
                y_w, y_c, p_t, q_t = tr[i * n_chunks + c]
                g_c = ops_ref[6, hp, pl.ds(c * CHUNK, 1), :]
                s0 = s_pairs[i]
                ys_ref[hp, rows[c], :] = _dot_nt(y_w, s0) + y_c
                s_pairs[i] = s0 * g_c + _dot(s0, p_t) + q_t
        for i, hp in enumerate(hps):
            st_ref[hp] = s_pairs[i]
        return carry

    lax.fori_loop(0, N_PAIRS // PAIRS_PER_STEP, group_body, 0)

    ys = jnp.concatenate([ys_ref[hp] for hp in range(N_PAIRS)], axis=1)
    y_r = _rwkv_post(ys, r, k_h, v, zr, rk_ref[...], lnw_ref[...], lnb_ref[...], e_ref, et_ref)
    yr_ref[0] = y_r.astype(BF16)

    @pl.when(s_idx == pl.num_programs(1) - 1)
    def _():
        for hp in range(N_PAIRS):
            sp = st_ref[hp]
            sfin_ref[0, 2 * hp] = sp[:HEAD_DIM, :HEAD_DIM]
            sfin_ref[0, 2 * hp + 1] = sp[HEAD_DIM:, HEAD_DIM:]


def _const_spec(shape):
    nd = len(shape)
    return pl.BlockSpec(shape, lambda *_: (0,) * nd)


def _rwkv_prompt(x, wts, tm=256):
    bsz, seq, _ = x.shape
    grid = (bsz, seq // tm)
    row = lambda a: a.reshape(1, -1)
    consts = [row(wts["norm_g"]), wts["w_rw"], row(wts["mu_rw"]), row(wts["w0"]), wts["lora_w"],
              row(wts["a0"]), row(wts["k_k"]), row(wts["k_a"]), row(wts["r_k"]),
              row(wts["ln_w"]), row(wts["ln_b"]), wts["e"], wts["et"]]
    kern = functools.partial(_rwkv_prompt_kernel, tm=tm)
    return pl.pallas_call(
        kern,
        grid=grid,
        in_specs=[pl.BlockSpec((1, tm, D_MODEL), lambda b, s: (b, s, 0))]
        + [_const_spec(c.shape) for c in consts],
        out_specs=[pl.BlockSpec((1, tm, D_MODEL), lambda b, s: (b, s, 0)),
                   pl.BlockSpec((1, N_HEADS, HEAD_DIM, HEAD_DIM), lambda b, s: (b, 0, 0, 0))],
        out_shape=[jax.ShapeDtypeStruct((bsz, seq, D_MODEL), BF16),
                   jax.ShapeDtypeStruct((bsz, N_HEADS, HEAD_DIM, HEAD_DIM), F32)],
        scratch_shapes=[pltpu.VMEM((SUBLANES, N_RW), F32),
                        pltpu.VMEM((N_PAIRS, PAIR, PAIR), F32),
                        pltpu.VMEM((8, N_PAIRS, tm, PAIR), F32),
                        pltpu.VMEM((N_PAIRS, tm, PAIR), F32)],
        compiler_params=pltpu.CompilerParams(
            dimension_semantics=("arbitrary", "arbitrary"), vmem_limit_bytes=VMEM_LIMIT),
        name="rwkv_prompt",
    )(x, *consts)


def _conv_kernel(*refs, tm, sample, seq_len=0):
    if sample:
        (x_ref, g_ref, w_ref, cw_ref, cb0_ref, cb1_ref, yc_ref, u_ref, h_ref) = refs
    else:
        (x_ref, g_ref, w_ref, cw_ref, yc_ref, u_ref, h_ref, carry_ref) = refs
        s_idx = pl.program_id(1)

        @pl.when(s_idx == 0)
        def _():
            carry_ref[...] = jnp.zeros_like(carry_ref)

    x = x_ref[0] if not sample else x_ref[...]
    hf = _rms(x, g_ref[...])
    p = jnp.dot(hf.astype(BF16), w_ref[...], preferred_element_type=F32)
    xin = p[:, 0:D_MODEL]
    bg = p[:, D_MODEL:2 * D_MODEL]
    cg = p[:, 2 * D_MODEL:3 * D_MODEL]
    zc = p[:, 3 * D_MODEL:]
    u = cg * xin
    if sample:
        tpos = _iota(u.shape, 0) % seq_len
        u1 = jnp.where(tpos == 0, cb1_ref[...], pltpu.roll(u, 1, 0))
        u2 = jnp.where(tpos == 0, cb0_ref[...],
                       jnp.where(tpos == 1, cb1_ref[...], pltpu.roll(u, 2, 0)))
        u_ref[...] = u
        h_ref[...] = hf
    else:
        carry = carry_ref[...]
        u1 = _shift_rows(u, 1, carry)
        u2 = _shift_rows(u, 2, carry)
        carry_ref[...] = u[tm - SUBLANES:, :]
        u_ref[0] = u[tm - SUBLANES:, :]
        h_ref[0] = hf[tm - SUBLANES:, :]
    cw = cw_ref[...]
    conv = cw[0:1, :] * u2 + cw[1:2, :] * u1 + cw[2:3, :] * u
    y_c = bg * conv * _silu(zc)
    if sample:
        yc_ref[...] = y_c.astype(BF16)
    else:
        yc_ref[0] = y_c.astype(BF16)


def _conv_prompt(x, wts, tm=512):
    bsz, seq, _ = x.shape
    consts = [wts["norm_g"].reshape(1, -1), wts["w_conv"], wts["conv_w"]]
    tail = pl.BlockSpec((1, SUBLANES, D_MODEL), lambda b, s: (b, 0, 0))
    return pl.pallas_call(
        functools.partial(_conv_kernel, tm=tm, sample=False),
        grid=(bsz, seq // tm),
        in_specs=[pl.BlockSpec((1, tm, D_MODEL), lambda b, s: (b, s, 0))]
        + [_const_spec(c.shape) for c in consts],
        out_specs=[pl.BlockSpec((1, tm, D_MODEL), lambda b, s: (b, s, 0)), tail, tail],
        out_shape=[jax.ShapeDtypeStruct((bsz, seq, D_MODEL), BF16),
                   jax.ShapeDtypeStruct((bsz, SUBLANES, D_MODEL), F32),
                   jax.ShapeDtypeStruct((bsz, SUBLANES, D_MODEL), F32)],
        scratch_shapes=[pltpu.VMEM((SUBLANES, D_MODEL), F32)],
        compiler_params=pltpu.CompilerParams(
            dimension_semantics=("arbitrary", "arbitrary"), vmem_limit_bytes=VMEM_LIMIT),
        name="conv_prompt",
    )(x, *consts)


def _conv_sample(x2d, cb0, cb1, wts, seq_len, tm=256):
    rows = x2d.shape[0]
    tm = min(tm, rows)
    consts = [wts["norm_g"].reshape(1, -1), wts["w_conv"], wts["conv_w"]]
    blk = pl.BlockSpec((tm, D_MODEL), lambda i: (i, 0))
    return pl.pallas_call(
        functools.partial(_conv_kernel, tm=tm, sample=True, seq_len=seq_len),
        grid=(rows // tm,),
        in_specs=[blk] + [_const_spec(c.shape) for c in consts] + [blk, blk],
        out_specs=[blk, blk, blk],
        out_shape=[jax.ShapeDtypeStruct((rows, D_MODEL), BF16),
                   jax.ShapeDtypeStruct((rows, D_MODEL), F32),
                   jax.ShapeDtypeStruct((rows, D_MODEL), F32)],
        compiler_params=pltpu.CompilerParams(
            dimension_semantics=("arbitrary",), vmem_limit_bytes=VMEM_LIMIT),
        name="conv_sample",
    )(x2d, *consts, cb0, cb1)


def _out_kernel(x_ref, yc_ref, yr_ref, g_ref, wg_ref, woc_ref, wor_ref, wo_ref, fg_ref, y_ref):
    x = x_ref[...]
    h = _rms(x, g_ref[...]).astype(BF16)
    gates = jnp.dot(h, wg_ref[...], preferred_element_type=F32)
    pc = jnp.dot(yc_ref[...], woc_ref[...], preferred_element_type=F32)
    pr = jnp.dot(yr_ref[...], wor_ref[...], preferred_element_type=F32)
    m = _sigmoid(gates[:, :D_MODEL]) * pc + _sigmoid(gates[:, D_MODEL:]) * pr
    out = jnp.dot(m.astype(BF16), wo_ref[...], preferred_element_type=F32)
    y_ref[...] = _rms(x + out, fg_ref[...])


def _out_stage(x2d, yc, yr, wts, tm=512):
    rows = x2d.shape[0]
    tm = min(tm, rows)
    consts = [wts["norm_g"].reshape(1, -1), wts["w_gate"], wts["w_out_c"], wts["w_out_r"],
              wts["w_o"], wts["final_g"].reshape(1, -1)]
    blk = pl.BlockSpec((tm, D_MODEL), lambda i: (i, 0))
    return pl.pallas_call(
        _out_kernel,
        grid=(rows // tm,),
        in_specs=[blk, blk, blk] + [_const_spec(c.shape) for c in consts],
        out_specs=blk,
        out_shape=jax.ShapeDtypeStruct((rows, D_MODEL), F32),
        compiler_params=pltpu.CompilerParams(
            dimension_semantics=("arbitrary",), vmem_limit_bytes=VMEM_LIMIT),
        name="out_stage",
    )(x2d, yc, yr, *consts)


SEQ_PAD = SUBLANES


SEQS_PER_STEP = 2


def _rwkv_sample_kernel(x_ref, xp_ref, hl_ref, s0_ref, g_ref, w_ref, mu_ref, w0_ref, lw_ref,
                        a0_ref, kk_ref, ka_ref, rk_ref, lnw_ref, lnb_ref, e_ref, et_ref,
                        yr_ref, s1_ref,
                        xk_ref, z_ref, uv_ref, bk_ref, gc_ref, *, nseq, seq_len):
    rows = nseq * SEQ_PAD
    g = g_ref[...]
    tpos = _iota((rows, 1), 0) % SEQ_PAD
    h = _rms(x_ref[...], g).astype(BF16)
    hprev = jnp.where(tpos == 0, hl_ref[...], _rms(xp_ref[...], g)).astype(BF16)
    both = jnp.dot(jnp.concatenate([h, hprev], axis=0), w_ref[...], preferred_element_type=F32)
    p, pp = both[:rows], both[rows:]
    xm = p + (pp - p) * mu_ref[...]
    r, k_h, v, kk, b, logd, zr = _rwkv_pointwise(xm, w0_ref[...], lw_ref[...], a0_ref[...],
                                                 kk_ref[...], ka_ref[...], e_ref, et_ref)
    valid = tpos < seq_len
    logd = jnp.where(valid, logd, 0.0)
    cum, cum_c = _block_cumsum(logd, SEQ_PAD)
    khat, rhat, btil, ktil, bchk, kchk, g_c = _scan_operands(r, k_h, v, kk, b, logd, cum, cum_c)

    def sh(x, j):
        return x if j == 0 else pltpu.roll(x, j, 0)

    def up4(x):
        return pltpu.roll(x, rows - seq_len, 0)

    xk_ref[...] = jnp.where(valid, khat, sh(rhat, seq_len))

    head_lanes = lambda hd: slice(hd * HEAD_DIM, (hd + 1) * HEAD_DIM)

    def seq_rows(it, j):
        return it * SEQS_PER_STEP + j, pl.ds(pl.multiple_of((it * SEQS_PER_STEP + j) * SEQ_PAD,
                                                            SEQ_PAD), SEQ_PAD)

    def phase1(it, carry):
        for j in range(SEQS_PER_STEP):
            i, r8 = seq_rows(it, j)
            x8 = xk_ref[r8, :]
            for hd in range(N_HEADS):
                z_ref[r8, head_lanes(hd)] = _dot_nt(x8[:, head_lanes(hd)], s0_ref[i, hd])
        return carry

    lax.fori_loop(0, nseq // SEQS_PER_STEP, phase1, 0)

    z = z_ref[...]
    z_r = up4(z)
    pairs = ([(khat, ktil, j) for j in range(1, seq_len)]
             + [(khat, btil, j) for j in range(1, seq_len)]
             + [(rhat, ktil, j) for j in range(seq_len)]
             + [(rhat, btil, j) for j in range(seq_len)])
    dots = _seg_sum(jnp.concatenate([x * sh(y, j) for x, y, j in pairs], axis=0),
                    e_ref, et_ref, n_in=1)
    dots = [dots[n * rows:(n + 1) * rows] for n in range(len(pairs))]
    n1 = seq_len - 1
    m_ak, m_ab = [None] + dots[:n1], [None] + dots[n1:2 * n1]
    m_rk, m_rb = dots[2 * n1:2 * n1 + seq_len], dots[2 * n1 + seq_len:]

    u = z
    for j in range(1, seq_len):
        u = u + jnp.where(tpos >= j, m_ak[j] * sh(v, j), 0.0)
    for t in range(1, seq_len):
        acc = u
        for j in range(1, t + 1):
            acc = acc - m_ab[j] * sh(u, j)
        u = jnp.where(tpos == t, acc, u)
    y = z_r
    for j in range(seq_len):
        y = y + jnp.where(tpos >= j, m_rk[j] * sh(v, j) - m_rb[j] * sh(u, j), 0.0)
    y_r = _rwkv_post(y, r, k_h, v, zr, rk_ref[...], lnw_ref[...], lnb_ref[...], e_ref, et_ref)
    yr_ref[...] = y_r.astype(BF16)

    uv_ref[...] = jnp.where(valid, u, sh(v, seq_len))
    bk_ref[...] = jnp.where(valid, bchk, sh(kchk, seq_len))
    gc_ref[...] = g_c

    def phase3(it, carry):
        for j in range(SEQS_PER_STEP):
            i, r8 = seq_rows(it, j)
            uv_t = uv_ref[r8, :].T
            bk8 = bk_ref[r8, :]
            gc1 = gc_ref[r8, :][0:1, :]
            for hd in range(N_HEADS):
                upd = _dot(uv_t[hd * HEAD_DIM:(hd + 1) * HEAD_DIM, :], bk8[:, head_lanes(hd)])
                s1_ref[i, hd] = s0_ref[i, hd] * gc1[:, head_lanes(hd)] + upd
        return carry

    lax.fori_loop(0, nseq // SEQS_PER_STEP, phase3, 0)


def _rwkv_sample(x_pad, xprev_pad, hlast_pad, s0, wts, seq_len, nseq=8):
    rows_total = x_pad.shape[0]
    n_all = rows_total // SEQ_PAD
    nseq = min(nseq, n_all)
    rows = nseq * SEQ_PAD
    row = lambda a: a.reshape(1, -1)
    consts = [row(wts["norm_g"]), wts["w_rw"], row(wts["mu_rw"]), row(wts["w0"]), wts["lora_w"],
              row(wts["a0"]), row(wts["k_k"]), row(wts["k_a"]), row(wts["r_k"]),
              row(wts["ln_w"]), row(wts["ln_b"]), wts["e"], wts["et"]]
    blk = pl.BlockSpec((rows, D_MODEL), lambda i: (i, 0))
    sblk = pl.BlockSpec((nseq, N_HEADS, HEAD_DIM, HEAD_DIM), lambda i: (i, 0, 0, 0))
    return pl.pallas_call(
        functools.partial(_rwkv_sample_kernel, nseq=nseq, seq_len=seq_len),
        grid=(n_all // nseq,),
        in_specs=[blk, blk, blk, sblk] + [_const_spec(c.shape) for c in consts],
        out_specs=[blk, sblk],
        out_shape=[jax.ShapeDtypeStruct((rows_total, D_MODEL), BF16),
                   jax.ShapeDtypeStruct(s0.shape, F32)],
        scratch_shapes=[pltpu.VMEM((rows, D_MODEL), F32) for _ in range(5)],
        compiler_params=pltpu.CompilerParams(
            dimension_semantics=("arbitrary",), vmem_limit_bytes=VMEM_LIMIT),
        name="rwkv_sample",
    )(x_pad, xprev_pad, hlast_pad, s0, *consts)


def _prep_weights(norm_g, w_in, conv_w, mu_shift, w0, w_up, a0, a_up, k_k, k_a, r_k, ln_w, ln_b,
                  w_out_c, w_out_r, w_o, final_g):
    d = D_MODEL
    c0 = 4 * d
    zeros = jnp.zeros((LORA, d), F32)
    lora_w = jnp.concatenate([jnp.concatenate([w_up, zeros], axis=1),
                              jnp.concatenate([zeros, a_up], axis=1)], axis=0)
    head_of_lane = jnp.arange(d) // HEAD_DIM
    e = (head_of_lane[:, None] == jnp.arange(LANES)[None, :]).astype(BF16)
    return dict(
        norm_g=norm_g, final_g=final_g, conv_w=conv_w,
        w_conv=w_in[:, :c0].astype(BF16),
        w_rw=w_in[:, c0:c0 + N_RW].astype(BF16),
        mu_rw=mu_shift,
        w_gate=w_in[:, c0 + N_RW:].astype(BF16),
        w0=w0, a0=a0, lora_w=lora_w.astype(BF16), k_k=k_k, k_a=k_a, r_k=r_k.reshape(-1),
        ln_w=ln_w, ln_b=ln_b,
        w_out_c=w_out_c.astype(BF16), w_out_r=w_out_r.astype(BF16), w_o=w_o.astype(BF16),
        e=e, et=e.T)


def _layer_prompt(x, wts):
    bsz, seq, d = x.shape
    yc, u_tail, h_tail = _conv_prompt(x, wts)
    yr, s_fin = _rwkv_prompt(x, wts)
    y = _out_stage(x.reshape(bsz * seq, d), yc.reshape(bsz * seq, d), yr.reshape(bsz * seq, d), wts)
    return (y.reshape(bsz, seq, d), u_tail[:, SUBLANES - 2:, :], h_tail[:, SUBLANES - 1, :], s_fin)


def _layer_sample(x, conv_buf, h_last, s0, wts):
    bsz, seq, d = x.shape
    rows = bsz * seq
    x2d = x.reshape(rows, d)
    rep_rows = lambda a: jnp.broadcast_to(a[:, None, :], (bsz, seq, d)).reshape(rows, d)
    yc, u, h = _conv_sample(x2d, rep_rows(conv_buf[:, 0]), rep_rows(conv_buf[:, 1]), wts, seq)
    pad = lambda a: jnp.pad(a, ((0, 0), (0, SEQ_PAD - seq), (0, 0))).reshape(bsz * SEQ_PAD, d)
    x_prev = jnp.concatenate([jnp.zeros((bsz, 1, d), x.dtype), x[:, :-1]], axis=1)
    hl = jnp.broadcast_to(h_last[:, None, :], (bsz, seq, d))
    yr_pad, s1 = _rwkv_sample(pad(x), pad(x_prev), pad(hl), s0, wts, seq)
    yr = yr_pad.reshape(bsz, SEQ_PAD, d)[:, :seq].reshape(rows, d)
    y = _out_stage(x2d, yc, yr, wts)
    u3 = u.reshape(bsz, seq, d)
    return (y.reshape(bsz, seq, d), u3[:, seq - 2:], h.reshape(bsz, seq, d)[:, seq - 1], s1)


def kernel(x_prompt, x_sample, state_conv, state_shift, state_rwkv, norm_g, w_in, conv_w, mu_shift,
           w0, w_up, a0, a_up, k_k, k_a, r_k, ln_w, ln_b, w_out_c, w_out_r, w_o, final_g):
    assert norm_g.shape[0] == 1, "single-layer step"
    wts = _prep_weights(norm_g[0], w_in[0], conv_w[0], mu_shift[0], w0[0], w_up[0], a0[0], a_up[0],
                        k_k[0], k_a[0], r_k[0], ln_w[0], ln_b[0], w_out_c[0], w_out_r[0], w_o[0],
                        final_g)
    y_p, c_p, s_p, r_p = _layer_prompt(x_prompt, wts)
    y_s, c_s, s_s, r_s = _layer_sample(x_sample, state_conv[0], state_shift[0], state_rwkv[0], wts)
    lead = lambda a: a[None]
    return (y_p, y_s, lead(c_p), lead(s_p), lead(r_p), lead(c_s), lead(s_s), lead(r_s))
```

```python
import functools
import math

import jax
import jax.numpy as jnp
from jax import lax
from jax.experimental import pallas as pl
from jax.experimental.pallas import tpu as pltpu

F32 = jnp.float32
BF16 = jnp.bfloat16

D_MODEL = 1024
HEAD_DIM = 64
N_HEADS = D_MODEL // HEAD_DIM
LORA = 64
RMS_EPS = 1e-6
GN_EPS = 64e-5
EXP_M05 = math.exp(-0.5)
LANES = 128
SUBLANES = 8
PAIR = 2 * HEAD_DIM
N_PAIRS = D_MODEL // PAIR
CHUNK = 64
PAIRS_PER_STEP = 8
N_RW = 4 * D_MODEL + 2 * LORA
VMEM_LIMIT = 56 * 1024 * 1024


def _dot(a, b):
    return jnp.dot(a.astype(BF16), b.astype(BF16), preferred_element_type=F32)


def _dot_nt(a, b):
    return lax.dot_general(a.astype(BF16), b.astype(BF16), (((1,), (1,)), ((), ())),
                           preferred_element_type=F32)


def _dot_tn(a, b):
    return lax.dot_general(a.astype(BF16), b.astype(BF16), (((0,), (0,)), ((), ())),
                           preferred_element_type=F32)


def _split(x, n):
    parts = []
    for i in range(n):
        piece = x.astype(BF16)
        parts.append(piece)
        if i + 1 < n:
            x = x - piece.astype(F32)
    return parts


def _dot_exact_lhs(m_bf16, x, n=2):
    return sum(jnp.dot(m_bf16, part, preferred_element_type=F32) for part in _split(x, n))


def _rms(x, g):
    return x * lax.rsqrt(jnp.mean(x * x, axis=-1, keepdims=True) + RMS_EPS) * g


def _sigmoid(x):
    return 1.0 / (1.0 + jnp.exp(-x))


def _silu(x):
    return x * _sigmoid(x)


def _iota(shape, dim):
    return lax.broadcasted_iota(jnp.int32, shape, dim)


def _shift_rows(x, n, carry8):
    rolled = pltpu.roll(x, n, 0)
    head = jnp.where(_iota((SUBLANES, x.shape[1]), 0) < n, pltpu.roll(carry8, n, 0),
                     rolled[:SUBLANES])
    return jnp.concatenate([head, rolled[SUBLANES:]], axis=0)


def _seg_sum(x):
    lo = _iota((1, PAIR), 1) < HEAD_DIM
    tiles = []
    for p in range(N_PAIRS):
        xp = x[:, p * PAIR:(p + 1) * PAIR]
        s_lo = jnp.sum(jnp.where(lo, xp, 0.0), axis=-1, keepdims=True)
        s_hi = jnp.sum(jnp.where(lo, 0.0, xp), axis=-1, keepdims=True)
        tiles.append(jnp.where(lo, s_lo, s_hi))
    return jnp.concatenate(tiles, axis=1)


def _block_cumsum(x, blk):
    rows, width = x.shape
    r = _iota((rows, rows), 0)
    c = _iota((rows, rows), 1)
    m_low = jnp.where(((r // blk) == (c // blk)) & (c <= r), 1.0, 0.0).astype(BF16)
    cum3 = _dot_exact_lhs(m_low, x).reshape(rows // blk, blk, width)
    return cum3, cum3[:, blk - 1:blk, :]


def _rwkv_pointwise(xm, w0, lora_w, a0, k_k, k_a):
    r = xm[:, 0:D_MODEL]
    k = xm[:, D_MODEL:2 * D_MODEL]
    v = xm[:, 2 * D_MODEL:3 * D_MODEL]
    da = xm[:, 3 * D_MODEL:3 * D_MODEL + 2 * LORA]
    zr = xm[:, 3 * D_MODEL + 2 * LORA:]
    lane = _iota(da.shape, 1)
    lora_in = jnp.where(lane < LORA, jnp.tanh(da), da)
    lo = _dot(lora_in, lora_w)
    w_logit = w0 + lo[:, :D_MODEL]
    logd = -EXP_M05 * _sigmoid(w_logit)
    a = _sigmoid(a0 + lo[:, D_MODEL:])
    kkr = k * k_k
    kk = kkr * jnp.minimum(lax.rsqrt(_seg_sum(kkr * kkr)), 1e12)
    k_h = k * (1.0 + (a - 1.0) * k_a)
    b = kk * a
    return r, k_h, v, kk, b, logd, zr


def _group_norm(ys, ln_w, ln_b):
    inv_n = 1.0 / HEAD_DIM
    yc = ys - _seg_sum(ys) * inv_n
    return yc * lax.rsqrt(_seg_sum(yc * yc) * inv_n + GN_EPS) * ln_w + ln_b


def _bonus(r, k_h, v, r_k):
    return _seg_sum(r * k_h * r_k) * v


def _scan_operands(r, k_h, v, kk, b, logd, cum3, tot3):
    shape = cum3.shape
    flat = lambda x: x.reshape(shape[0] * shape[1], shape[2])
    cum = flat(cum3)
    g_c3 = jnp.exp(tot3)
    enc3 = jnp.exp(-cum3)
    enc = flat(enc3)
    e_c = flat(enc3 * g_c3)
    khat = kk * jnp.exp(cum - logd)
    rhat = r * jnp.exp(cum)
    btil = b * enc
    ktil = k_h * enc
    bchk = -(b * e_c)
    kchk = k_h * e_c
    g_c = flat(jnp.broadcast_to(g_c3, shape))
    return khat, rhat, btil, ktil, bchk, kchk, g_c


def _chunk_transfer(units, m_bd, m_sl, m_l):
    m_bd2 = jnp.concatenate([m_bd, m_bd], axis=1)

    def expand(x):
        x2 = jnp.concatenate([x, x], axis=0)
        return jnp.where(m_bd if x.shape[1] == PAIR else m_bd2, x2, 0.0)

    gs = [_dot_nt(jnp.concatenate([kh, rh], axis=0),
                  jnp.concatenate([expand(bt), expand(kt)], axis=0))
          for kh, rh, bt, kt, _, _, _ in units]
    ps = [jnp.where(m_sl, -g[:CHUNK, :PAIR], 0.0) for g in gs]
    m_as = [jnp.concatenate([jnp.where(m_sl, g[:CHUNK, PAIR:], 0.0),
                             jnp.where(m_l, g[CHUNK:, PAIR:], 0.0)], axis=0) for g in gs]
    m_rbs = [jnp.where(m_l, -g[CHUNK:, :PAIR], 0.0) for g in gs]
    zs = [_dot(m_a, expand(u[6])) for m_a, u in zip(m_as, units)]
    xs = [jnp.concatenate([u[0], z[:CHUNK]], axis=1) for u, z in zip(units, zs)]
    for j in range(6):
        xs = [x + _dot(p, expand(x)) for p, x in zip(ps, xs)]
        if j < 5:
            ps = [_dot(p, expand(p)) for p in ps]
    t1s = [_dot(m_rb, expand(x)) for m_rb, x in zip(m_rbs, xs)]
    p_ts = [_dot_tn(x[:, :PAIR], u[4]) for x, u in zip(xs, units)]
    q_ts = [_dot_tn(jnp.concatenate([x[:, PAIR:], u[6]], axis=0),
                    jnp.concatenate([u[4], u[5]], axis=0)) for x, u in zip(xs, units)]
    out = []
    for u, z, t1, p_t, q_t in zip(units, zs, t1s, p_ts, q_ts):
        y_w = u[1] + t1[:, :PAIR]
        y_c = t1[:, PAIR:] + z[CHUNK:]
        out.append((y_w, y_c, jnp.where(m_bd, p_t, 0.0), jnp.where(m_bd, q_t, 0.0)))
    return out


def _rwkv_prompt_kernel(x_ref, g_ref, w_ref, mu_ref, w0_ref, lw_ref, a0_ref, kk_ref, ka_ref,
                        rk_ref, lnw_ref, lnb_ref,
                        yr_ref, sfin_ref,
                        carry_ref, st_ref, ops_ref, ys_ref, *, tm):
    s_idx = pl.program_id(1)

    @pl.when(s_idx == 0)
    def _():
        carry_ref[...] = jnp.zeros_like(carry_ref)
        st_ref[...] = jnp.zeros_like(st_ref)

    h = _rms(x_ref[0], g_ref[...]).astype(BF16)
    p = jnp.dot(h, w_ref[...], preferred_element_type=F32)
    prev = _shift_rows(p, 1, carry_ref[...])
    carry_ref[...] = p[tm - SUBLANES:, :]
    xm = p + (prev - p) * mu_ref[...]
    r, k_h, v, kk, b, logd, zr = _rwkv_pointwise(xm, w0_ref[...], lw_ref[...], a0_ref[...],
                                                 kk_ref[...], ka_ref[...])
    cum, cum_c = _block_cumsum(logd, CHUNK)
    ops = _scan_operands(r, k_h, v, kk, b, logd, cum, cum_c) + (v,)
    for i, arr in enumerate(ops):
        for hp in range(N_PAIRS):
            ops_ref[i, hp] = arr[:, hp * PAIR:(hp + 1) * PAIR]

    rr = _iota((PAIR, PAIR), 0)
    cc = _iota((PAIR, PAIR), 1)
    m_bd = (rr // HEAD_DIM) == (cc // HEAD_DIM)
    t_row = _iota((CHUNK, PAIR), 0)
    s_col = _iota((CHUNK, PAIR), 1) % HEAD_DIM
    m_sl = s_col < t_row
    m_l = s_col <= t_row

    n_chunks = tm // CHUNK

    def group_body(gi, carry):
        hps = [gi * PAIRS_PER_STEP + i for i in range(PAIRS_PER_STEP)]
        rows = [pl.ds(c * CHUNK, CHUNK) for c in range(n_chunks)]
        units = [tuple(ops_ref[i, hp, rw, :] for i in (0, 1, 2, 3, 4, 5, 7))
                 for hp in hps for rw in rows]
        tr = _chunk_transfer(units, m_bd, m_sl, m_l)
        s_pairs = [st_ref[hp] for hp in hps]
        for c in range(n_chunks):
            for i, hp in enumerate(hps):
                y_w, y_c, p_t, q_t = tr[i * n_chunks + c]
                g_c = ops_ref[6, hp, pl.ds(c * CHUNK, 1), :]
                s0 = s_pairs[i]
                ys_ref[hp, rows[c], :] = _dot_nt(y_w, s0) + y_c
                s_pairs[i] = s0 * g_c + _dot(s0, p_t) + q_t
        for i, hp in enumerate(hps):
            st_ref[hp] = s_pairs[i]
        return carry

    lax.fori_loop(0, N_PAIRS // PAIRS_PER_STEP, group_body, 0)

    ys = jnp.concatenate([ys_ref[hp] for hp in range(N_PAIRS)], axis=1)
    y_r = ((_group_norm(ys, lnw_ref[...], lnb_ref[...]) + _bonus(r, k_h, v, rk_ref[...]))
           * _silu(zr))
    yr_ref[0] = y_r.astype(BF16)

    @pl.when(s_idx == pl.num_programs(1) - 1)
    def _():
        for hp in range(N_PAIRS):
            sp = st_ref[hp]
            sfin_ref[0, 2 * hp] = sp[:HEAD_DIM, :HEAD_DIM]
            sfin_ref[0, 2 * hp + 1] = sp[HEAD_DIM:, HEAD_DIM:]


def _const_spec(shape):
    nd = len(shape)
    return pl.BlockSpec(shape, lambda *_: (0,) * nd)


def _row(a):
    return a.reshape(1, -1)


def _rwkv_consts(wts):
    return [_row(wts["norm_g"]), wts["w_rw"], _row(wts["mu_rw"]), _row(wts["w0"]), wts["lora_w"],
            _row(wts["a0"]), _row(wts["k_k"]), _row(wts["k_a"]), _row(wts["r_k"])]


def _rwkv_prompt(x, wts, tm=256):
    bsz, seq, _ = x.shape
    grid = (bsz, seq // tm)
    consts = _rwkv_consts(wts) + [_row(wts["ln_w"]), _row(wts["ln_b"])]
    kern = functools.partial(_rwkv_prompt_kernel, tm=tm)
    return pl.pallas_call(
        kern,
        grid=grid,
        in_specs=[pl.BlockSpec((1, tm, D_MODEL), lambda b, s: (b, s, 0))]
        + [_const_spec(c.shape) for c in consts],
        out_specs=[pl.BlockSpec((1, tm, D_MODEL), lambda b, s: (b, s, 0)),
                   pl.BlockSpec((1, N_HEADS, HEAD_DIM, HEAD_DIM), lambda b, s: (b, 0, 0, 0))],
        out_shape=[jax.ShapeDtypeStruct((bsz, seq, D_MODEL), BF16),
                   jax.ShapeDtypeStruct((bsz, N_HEADS, HEAD_DIM, HEAD_DIM), F32)],
        scratch_shapes=[pltpu.VMEM((SUBLANES, N_RW), F32),
                        pltpu.VMEM((N_PAIRS, PAIR, PAIR), F32),
                        pltpu.VMEM((8, N_PAIRS, tm, PAIR), F32),
                        pltpu.VMEM((N_PAIRS, tm, PAIR), F32)],
        compiler_params=pltpu.CompilerParams(
            dimension_semantics=("arbitrary", "arbitrary"), vmem_limit_bytes=VMEM_LIMIT),
        name="rwkv_prompt",
    )(x, *consts)


def _conv_gate(p, u, u1, u2, cw):
    conv = cw[0:1, :] * u2 + cw[1:2, :] * u1 + cw[2:3, :] * u
    return p[:, D_MODEL:2 * D_MODEL] * conv * _silu(p[:, 3 * D_MODEL:])


def _conv_kernel(x_ref, g_ref, w_ref, cw_ref, yc_ref, u_ref, h_ref, carry_ref, *, tm):
    @pl.when(pl.program_id(1) == 0)
    def _():
        carry_ref[...] = jnp.zeros_like(carry_ref)

    hf = _rms(x_ref[0], g_ref[...])
    p = jnp.dot(hf.astype(BF16), w_ref[...], preferred_element_type=F32)
    u = p[:, 2 * D_MODEL:3 * D_MODEL] * p[:, 0:D_MODEL]
    carry = carry_ref[...]
    y_c = _conv_gate(p, u, _shift_rows(u, 1, carry), _shift_rows(u, 2, carry), cw_ref[...])
    carry_ref[...] = u[tm - SUBLANES:, :]
    u_ref[0] = u[tm - SUBLANES:, :]
    h_ref[0] = hf[tm - SUBLANES:, :]
    yc_ref[0] = y_c.astype(BF16)


def _conv_prompt(x, wts, tm=512):
    bsz, seq, _ = x.shape
    consts = [_row(wts["norm_g"]), wts["w_conv"], wts["conv_w"]]
    tail = pl.BlockSpec((1, SUBLANES, D_MODEL), lambda b, s: (b, 0, 0))
    return pl.pallas_call(
        functools.partial(_conv_kernel, tm=tm),
        grid=(bsz, seq // tm),
        in_specs=[pl.BlockSpec((1, tm, D_MODEL), lambda b, s: (b, s, 0))]
        + [_const_spec(c.shape) for c in consts],
        out_specs=[pl.BlockSpec((1, tm, D_MODEL), lambda b, s: (b, s, 0)), tail, tail],
        out_shape=[jax.ShapeDtypeStruct((bsz, seq, D_MODEL), BF16),
                   jax.ShapeDtypeStruct((bsz, SUBLANES, D_MODEL), F32),
                   jax.ShapeDtypeStruct((bsz, SUBLANES, D_MODEL), F32)],
        scratch_shapes=[pltpu.VMEM((SUBLANES, D_MODEL), F32)],
        compiler_params=pltpu.CompilerParams(
            dimension_semantics=("arbitrary", "arbitrary"), vmem_limit_bytes=VMEM_LIMIT),
        name="conv_prompt",
    )(x, *consts)


def _merge_out(x, yc, yr, g_ref, wg_ref, woc_ref, wor_ref, wo_ref, fg_ref):
    h = _rms(x, g_ref[...]).astype(BF16)
    gates = jnp.dot(h, wg_ref[...], preferred_element_type=F32)
    pc = jnp.dot(yc, woc_ref[...], preferred_element_type=F32)
    pr = jnp.dot(yr, wor_ref[...], preferred_element_type=F32)
    m = _sigmoid(gates[:, :D_MODEL]) * pc + _sigmoid(gates[:, D_MODEL:]) * pr
    out = jnp.dot(m.astype(BF16), wo_ref[...], preferred_element_type=F32)
    return _rms(x + out, fg_ref[...])


def _out_consts(wts):
    return [_row(wts["norm_g"]), wts["w_gate"], wts["w_out_c"], wts["w_out_r"], wts["w_o"],
            _row(wts["final_g"])]


def _out_kernel(x_ref, yc_ref, yr_ref, g_ref, wg_ref, woc_ref, wor_ref, wo_ref, fg_ref, y_ref):
    y_ref[...] = _merge_out(x_ref[...], yc_ref[...], yr_ref[...], g_ref, wg_ref, woc_ref, wor_ref,
                            wo_ref, fg_ref)


def _out_stage(x2d, yc, yr, wts, tm=512):
    rows = x2d.shape[0]
    consts = _out_consts(wts)
    blk = pl.BlockSpec((tm, D_MODEL), lambda i: (i, 0))
    return pl.pallas_call(
        _out_kernel,
        grid=(rows // tm,),
        in_specs=[blk, blk, blk] + [_const_spec(c.shape) for c in consts],
        out_specs=blk,
        out_shape=jax.ShapeDtypeStruct((rows, D_MODEL), F32),
        compiler_params=pltpu.CompilerParams(
            dimension_semantics=("arbitrary",), vmem_limit_bytes=VMEM_LIMIT),
        name="out_stage",
    )(x2d, yc, yr, *consts)


N_SCAN_OPS = 6
V_BLOCK = 8


def _sample_front_kernel(x_ref, hl_ref, cb0_ref, cb1_ref, g_ref, wc_ref, cw_ref, w_ref, mu_ref,
                         w0_ref, lw_ref, a0_ref, kk_ref, ka_ref, rk_ref,
                         yc_ref, u_ref, h_ref, tr_ref, bz_ref,
                         hp_s, u1_s, u2_s):
    @pl.when(pl.program_id(0) == 0)
    def _():
        hp_s[...] = hl_ref[...]
        u1_s[...] = cb1_ref[...]
        u2_s[...] = cb0_ref[...]

    nb = x_ref.shape[1]
    hf = _rms(x_ref[0], g_ref[...])
    h = hf.astype(BF16)
    pc = jnp.dot(h, wc_ref[...], preferred_element_type=F32)
    u = pc[:, 2 * D_MODEL:3 * D_MODEL] * pc[:, 0:D_MODEL]
    u1 = u1_s[...]
    yc_ref[0] = _conv_gate(pc, u, u1, u2_s[...], cw_ref[...]).astype(BF16)
    u2_s[...] = u1
    u1_s[...] = u
    u_ref[0] = u
    h_ref[0] = hf
    both = jnp.dot(jnp.concatenate([h, hp_s[...].astype(BF16)], axis=0), w_ref[...],
                   preferred_element_type=F32)
    hp_s[...] = hf
    p, pp = both[:nb], both[nb:]
    xm = p + (pp - p) * mu_ref[...]
    r, k_h, v, kk, b, logd, zr = _rwkv_pointwise(xm, w0_ref[...], lw_ref[...], a0_ref[...],
                                                 kk_ref[...], ka_ref[...])
    for i, arr in enumerate((kk, b, jnp.exp(logd), k_h, r, v)):
        tr_ref[0, i] = arr.T
    bz_ref[0, 0] = _bonus(r, k_h, v, rk_ref[...])
    bz_ref[0, 1] = _silu(zr)


def _sample_front(xs_t, h_last, cb0, cb1, wts):
    n_tok, nb, _ = xs_t.shape
    consts = ([_row(wts["norm_g"]), wts["w_conv"], wts["conv_w"]] + _rwkv_consts(wts)[1:])
    tok = pl.BlockSpec((1, nb, D_MODEL), lambda t: (t, 0, 0))
    seq = pl.BlockSpec((nb, D_MODEL), lambda t: (0, 0))
    return pl.pallas_call(
        _sample_front_kernel,
        grid=(n_tok,),
        in_specs=[tok, seq, seq, seq] + [_const_spec(c.shape) for c in consts],
        out_specs=[tok, tok, tok,
                   pl.BlockSpec((1, N_SCAN_OPS, D_MODEL, nb), lambda t: (t, 0, 0, 0)),
                   pl.BlockSpec((1, 2, nb, D_MODEL), lambda t: (t, 0, 0, 0))],
        out_shape=[jax.ShapeDtypeStruct((n_tok, nb, D_MODEL), BF16),
                   jax.ShapeDtypeStruct((n_tok, nb, D_MODEL), F32),
                   jax.ShapeDtypeStruct((n_tok, nb, D_MODEL), F32),
                   jax.ShapeDtypeStruct((n_tok, N_SCAN_OPS, D_MODEL, nb), F32),
                   jax.ShapeDtypeStruct((n_tok, 2, nb, D_MODEL), F32)],
        scratch_shapes=[pltpu.VMEM((nb, D_MODEL), F32) for _ in range(3)],
        compiler_params=pltpu.CompilerParams(
            dimension_semantics=("arbitrary",), vmem_limit_bytes=VMEM_LIMIT),
        name="sample_front",
    )(xs_t, h_last, cb0, cb1, *consts)


def _sample_scan_kernel(s0_ref, tr_ref, s1_ref, y_ref, *, n_tok):
    def block(vb, carry):
        rows = pl.ds(pl.multiple_of(vb * V_BLOCK, V_BLOCK), V_BLOCK)
        s = s0_ref[0, rows]
        for t in range(n_tok):
            kk, b, d, k, r = [tr_ref[t, i][None] for i in range(5)]
            vv = tr_ref[t, 5, rows, :][:, None, :]
            sa = -jnp.sum(s * kk, axis=1, keepdims=True)
            s = s * d + sa * b + vv * k
            y_ref[t, rows, :] = jnp.sum(s * r, axis=1)
        s1_ref[0, rows] = s
        return carry

    lax.fori_loop(0, HEAD_DIM // V_BLOCK, block, 0)


def _sample_scan(s0_t, tr):
    n_tok = tr.shape[0]
    nb = s0_t.shape[-1]
    sblk = pl.BlockSpec((1, HEAD_DIM, HEAD_DIM, nb), lambda h: (h, 0, 0, 0))
    return pl.pallas_call(
        functools.partial(_sample_scan_kernel, n_tok=n_tok),
        grid=(N_HEADS,),
        in_specs=[sblk, pl.BlockSpec((n_tok, N_SCAN_OPS, HEAD_DIM, nb), lambda h: (0, 0, h, 0))],
        out_specs=[sblk, pl.BlockSpec((n_tok, HEAD_DIM, nb), lambda h: (0, h, 0))],
        out_shape=[jax.ShapeDtypeStruct(s0_t.shape, F32),
                   jax.ShapeDtypeStruct((n_tok, D_MODEL, nb), F32)],
        compiler_params=pltpu.CompilerParams(
            dimension_semantics=("arbitrary",), vmem_limit_bytes=VMEM_LIMIT),
        name="sample_scan",
    )(s0_t, tr)


def _sample_out_kernel(x_ref, yc_ref, yt_ref, bz_ref, lnw_ref, lnb_ref,
                       g_ref, wg_ref, woc_ref, wor_ref, wo_ref, fg_ref, y_ref):
    ys = yt_ref[0].T
    y_r = (_group_norm(ys, lnw_ref[...], lnb_ref[...]) + bz_ref[0, 0]) * bz_ref[0, 1]
    y_ref[0] = _merge_out(x_ref[0], yc_ref[0], y_r.astype(BF16), g_ref, wg_ref, woc_ref, wor_ref,
                          wo_ref, fg_ref)


def _sample_out(xs_t, yc, y_t, bz, wts):
    n_tok, nb, _ = xs_t.shape
    consts = [_row(wts["ln_w"]), _row(wts["ln_b"])] + _out_consts(wts)
    tok = pl.BlockSpec((1, nb, D_MODEL), lambda t: (t, 0, 0))
    return pl.pallas_call(
        _sample_out_kernel,
        grid=(n_tok,),
        in_specs=[tok, tok, pl.BlockSpec((1, D_MODEL, nb), lambda t: (t, 0, 0)),
                  pl.BlockSpec((1, 2, nb, D_MODEL), lambda t: (t, 0, 0, 0))]
        + [_const_spec(c.shape) for c in consts],
        out_specs=tok,
        out_shape=jax.ShapeDtypeStruct((n_tok, nb, D_MODEL), F32),
        compiler_params=pltpu.CompilerParams(
            dimension_semantics=("arbitrary",), vmem_limit_bytes=VMEM_LIMIT),
        name="sample_out",
    )(xs_t, yc, y_t, bz, *consts)


def _prep_weights(norm_g, w_in, conv_w, mu_shift, w0, w_up, a0, a_up, k_k, k_a, r_k, ln_w, ln_b,
                  w_out_c, w_out_r, w_o, final_g):
    d = D_MODEL
    c0 = 4 * d
    zeros = jnp.zeros((LORA, d), F32)
    lora_w = jnp.concatenate([jnp.concatenate([w_up, zeros], axis=1),
                              jnp.concatenate([zeros, a_up], axis=1)], axis=0)
    return dict(
        norm_g=norm_g, final_g=final_g, conv_w=conv_w,
        w_conv=w_in[:, :c0].astype(BF16),
        w_rw=w_in[:, c0:c0 + N_RW].astype(BF16),
        mu_rw=mu_shift,
        w_gate=w_in[:, c0 + N_RW:].astype(BF16),
        w0=w0, a0=a0, lora_w=lora_w.astype(BF16), k_k=k_k, k_a=k_a, r_k=r_k.reshape(-1),
        ln_w=ln_w, ln_b=ln_b,
        w_out_c=w_out_c.astype(BF16), w_out_r=w_out_r.astype(BF16), w_o=w_o.astype(BF16))


def _layer_prompt(x, wts):
    bsz, seq, d = x.shape
    yc, u_tail, h_tail = _conv_prompt(x, wts)
    yr, s_fin = _rwkv_prompt(x, wts)
    y = _out_stage(x.reshape(bsz * seq, d), yc.reshape(bsz * seq, d), yr.reshape(bsz * seq, d), wts)
    return (y.reshape(bsz, seq, d), u_tail[:, SUBLANES - 2:, :], h_tail[:, SUBLANES - 1, :], s_fin)


def _layer_sample(x, conv_buf, h_last, s0, wts):
    seq = x.shape[1]
    xs_t = jnp.swapaxes(x, 0, 1)
    yc, u, h, tr, bz = _sample_front(xs_t, h_last, conv_buf[:, 0], conv_buf[:, 1], wts)
    s1_t, y_t = _sample_scan(jnp.transpose(s0, (1, 2, 3, 0)), tr)
    y = _sample_out(xs_t, yc, y_t, bz, wts)
    return (jnp.swapaxes(y, 0, 1), jnp.swapaxes(u[seq - 2:], 0, 1), h[seq - 1],
            jnp.transpose(s1_t, (3, 0, 1, 2)))


def kernel(x_prompt, x_sample, state_conv, state_shift, state_rwkv, norm_g, w_in, conv_w, mu_shift,
           w0, w_up, a0, a_up, k_k, k_a, r_k, ln_w, ln_b, w_out_c, w_out_r, w_o, final_g):
    assert norm_g.shape[0] == 1, "single-layer step"
    wts = _prep_weights(norm_g[0], w_in[0], conv_w[0], mu_shift[0], w0[0], w_up[0], a0[0], a_up[0],
                        k_k[0], k_a[0], r_k[0], ln_w[0], ln_b[0], w_out_c[0], w_out_r[0], w_o[0],
                        final_g)
    y_p, c_p, s_p, r_p = _layer_prompt(x_prompt, wts)
    y_s, c_s, s_s, r_s = _layer_sample(x_sample, state_conv[0], state_shift[0], state_rwkv[0], wts)
    lead = lambda a: a[None]
    return (y_p, y_s, lead(c_p), lead(s_p), lead(r_p), lead(c_s), lead(s_s), lead(r_s))
```

```python
import functools
import math

import jax
import jax.numpy as jnp
from jax import lax
from jax.experimental import pallas as pl
from jax.experimental.pallas import tpu as pltpu

F32 = jnp.float32
BF16 = jnp.bfloat16

D_MODEL = 1024
HEAD_DIM = 64
N_HEADS = D_MODEL // HEAD_DIM
LORA = 64
RMS_EPS = 1e-6
GN_EPS = 64e-5
EXP_M05 = math.exp(-0.5)
LANES = 128
SUBLANES = 8
PAIR = 2 * HEAD_DIM
N_PAIRS = D_MODEL // PAIR
CHUNK = 64
TILES = 1
N_RW = 4 * D_MODEL + 2 * LORA
VMEM_LIMIT = 56 * 1024 * 1024


def _dot(a, b):
    return jnp.dot(a.astype(BF16), b.astype(BF16), preferred_element_type=F32)


def _dot_nt(a, b):
    return lax.dot_general(a.astype(BF16), b.astype(BF16), (((1,), (1,)), ((), ())),
                           preferred_element_type=F32)


def _dot_tn(a, b):
    return lax.dot_general(a.astype(BF16), b.astype(BF16), (((0,), (0,)), ((), ())),
                           preferred_element_type=F32)


def _split(x, n):
    parts = []
    for i in range(n):
        piece = x.astype(BF16)
        parts.append(piece)
        if i + 1 < n:
            x = x - piece.astype(F32)
    return parts


def _dot_exact_lhs(m_bf16, x, n=2):
    return sum(jnp.dot(m_bf16, part, preferred_element_type=F32) for part in _split(x, n))


def _rms(x, g):
    return x * lax.rsqrt(jnp.mean(x * x, axis=-1, keepdims=True) + RMS_EPS) * g


def _sigmoid(x):
    return 1.0 / (1.0 + jnp.exp(-x))


def _silu(x):
    return x * _sigmoid(x)


def _iota(shape, dim):
    return lax.broadcasted_iota(jnp.int32, shape, dim)


def _shift_rows(x, n, carry8):
    rolled = pltpu.roll(x, n, 0)
    head = jnp.where(_iota((SUBLANES, x.shape[1]), 0) < n, pltpu.roll(carry8, n, 0),
                     rolled[:SUBLANES])
    return jnp.concatenate([head, rolled[SUBLANES:]], axis=0)


def _seg_sum(x):
    lo = _iota((1, PAIR), 1) < HEAD_DIM
    tiles = []
    for p in range(N_PAIRS):
        xp = x[:, p * PAIR:(p + 1) * PAIR]
        s_lo = jnp.sum(jnp.where(lo, xp, 0.0), axis=-1, keepdims=True)
        s_hi = jnp.sum(jnp.where(lo, 0.0, xp), axis=-1, keepdims=True)
        tiles.append(jnp.where(lo, s_lo, s_hi))
    return jnp.concatenate(tiles, axis=1)


def _block_cumsum(x, blk):
    rows, width = x.shape
    r = _iota((rows, rows), 0)
    c = _iota((rows, rows), 1)
    m_low = jnp.where(((r // blk) == (c // blk)) & (c <= r), 1.0, 0.0).astype(BF16)
    cum3 = _dot_exact_lhs(m_low, x).reshape(rows // blk, blk, width)
    return cum3, cum3[:, blk - 1:blk, :]


def _rwkv_pointwise(xm, w0, lora_w, a0, k_k, k_a):
    r = xm[:, 0:D_MODEL]
    k = xm[:, D_MODEL:2 * D_MODEL]
    v = xm[:, 2 * D_MODEL:3 * D_MODEL]
    da = xm[:, 3 * D_MODEL:3 * D_MODEL + 2 * LORA]
    zr = xm[:, 3 * D_MODEL + 2 * LORA:]
    lane = _iota(da.shape, 1)
    lora_in = jnp.where(lane < LORA, jnp.tanh(da), da)
    lo = _dot(lora_in, lora_w)
    w_logit = w0 + lo[:, :D_MODEL]
    logd = -EXP_M05 * _sigmoid(w_logit)
    a = _sigmoid(a0 + lo[:, D_MODEL:])
    kkr = k * k_k
    kk = kkr * jnp.minimum(lax.rsqrt(_seg_sum(kkr * kkr)), 1e12)
    k_h = k * (1.0 + (a - 1.0) * k_a)
    b = kk * a
    return r, k_h, v, kk, b, logd, zr


def _group_norm(ys, ln_w, ln_b):
    inv_n = 1.0 / HEAD_DIM
    yc = ys - _seg_sum(ys) * inv_n
    return yc * lax.rsqrt(_seg_sum(yc * yc) * inv_n + GN_EPS) * ln_w + ln_b


def _bonus(r, k_h, v, r_k):
    return _seg_sum(r * k_h * r_k) * v


def _scan_operands(r, k_h, v, kk, b, logd, cum3, tot3):
    shape = cum3.shape
    flat = lambda x: x.reshape(shape[0] * shape[1], shape[2])
    cum = flat(cum3)
    g_c3 = jnp.exp(tot3)
    enc3 = jnp.exp(-cum3)
    enc = flat(enc3)
    e_c = flat(enc3 * g_c3)
    khat = kk * jnp.exp(cum - logd)
    rhat = r * jnp.exp(cum)
    btil = b * enc
    ktil = k_h * enc
    bchk = -(b * e_c)
    kchk = k_h * e_c
    return khat, rhat, btil, ktil, bchk, kchk, g_c3


def _chunk_transfer(units, m_bd, m_sl, m_l):
    m_bd2 = jnp.concatenate([m_bd, m_bd], axis=1)
    zero = jnp.zeros((), BF16)

    def expand(x):
        x = x.astype(BF16)
        x2 = jnp.concatenate([x, x], axis=0)
        return jnp.where(m_bd if x.shape[1] == PAIR else m_bd2, x2, zero)

    gs = [_dot_nt(jnp.concatenate([kh, rh.astype(BF16)], axis=0),
                  jnp.concatenate([expand(bt), expand(kt)], axis=0))
          for kh, rh, bt, kt, _, _, _ in units]
    ps = [jnp.where(m_sl, -g[:CHUNK, :PAIR], 0.0).astype(BF16) for g in gs]
    m_as = [jnp.concatenate([jnp.where(m_sl, g[:CHUNK, PAIR:], 0.0),
                             jnp.where(m_l, g[CHUNK:, PAIR:], 0.0)], axis=0).astype(BF16)
            for g in gs]
    m_rbs = [jnp.where(m_l, -g[CHUNK:, :PAIR], 0.0).astype(BF16) for g in gs]
    zs = [_dot(m_a, expand(u[6])) for m_a, u in zip(m_as, units)]
    eye = jnp.where(m_l & ~m_sl, 1.0, 0.0)
    ts = [eye + p.astype(F32) for p in ps]
    for _ in range(5):
        ps = [_dot(p, expand(p)).astype(BF16) for p in ps]
        ts = [t + _dot(p, expand(t)) for p, t in zip(ps, ts)]
    xs = [_dot(t, expand(jnp.concatenate([u[0], z[:CHUNK].astype(BF16)], axis=1))).astype(BF16)
          for t, u, z in zip(ts, units, zs)]
    t1s = [_dot(m_rb, expand(x)) for m_rb, x in zip(m_rbs, xs)]
    p_ts = [_dot_tn(x[:, :PAIR], u[4]) for x, u in zip(xs, units)]
    q_ts = [_dot_tn(jnp.concatenate([x[:, PAIR:], u[6]], axis=0),
                    jnp.concatenate([u[4], u[5]], axis=0)) for x, u in zip(xs, units)]
    out = []
    for u, z, t1, p_t, q_t in zip(units, zs, t1s, p_ts, q_ts):
        y_w = (u[1] + t1[:, :PAIR]).astype(BF16)
        y_c = t1[:, PAIR:] + z[CHUNK:]
        out.append((y_w, y_c, jnp.where(m_bd, p_t, 0.0).astype(BF16), jnp.where(m_bd, q_t, 0.0)))
    return out


def _rwkv_prompt_kernel(x_ref, g_ref, w_ref, mu_ref, w0_ref, lw_ref, a0_ref, kk_ref, ka_ref,
                        rk_ref, lnw_ref, lnb_ref,
                        yr_ref, sfin_ref,
                        carry_ref, st_ref, opb_ref, rh_ref, gc_ref, bz_ref, ys_ref,
                        *, tm, n_tiles, n_steps):
    g = pl.program_id(0)
    n_chunks = tm // CHUNK
    tiles = range(TILES)
    first = jnp.minimum(g, n_steps - 2) % n_tiles == 0
    live = g < n_steps - 1

    @pl.when(g == 0)
    def _():
        for ref in (carry_ref, st_ref, bz_ref, ys_ref):
            ref[...] = jnp.zeros_like(ref)

    def project(ti):
        h = _rms(x_ref[ti, 0], g_ref[...]).astype(BF16)
        return jnp.dot(h, w_ref[...], preferred_element_type=F32)

    def pointwise(ti, p, after):
        prev = _shift_rows(p, 1, jnp.where(first, 0.0, carry_ref[ti]))
        carry_ref[ti] = p[tm - SUBLANES:, :]
        xm = p + (prev - p) * (mu_ref[...] + after)
        r, k_h, v, kk, b, logd, zr = _rwkv_pointwise(xm, w0_ref[...], lw_ref[...], a0_ref[...],
                                                     kk_ref[...], ka_ref[...])
        cum3, tot3 = _block_cumsum(logd, CHUNK)
        khat, rhat, btil, ktil, bchk, kchk, g_c3 = _scan_operands(r, k_h, v, kk, b, logd,
                                                                  cum3, tot3)
        for i, arr in enumerate((khat, btil, ktil, bchk, kchk, v)):
            a16 = arr.astype(BF16)
            for hp in range(N_PAIRS):
                opb_ref[ti, i, hp] = a16[:, hp * PAIR:(hp + 1) * PAIR]
        for hp in range(N_PAIRS):
            rh_ref[ti, hp] = rhat[:, hp * PAIR:(hp + 1) * PAIR]
        gc_ref[ti, 0:n_chunks, :] = g_c3.reshape(n_chunks, D_MODEL)
        bz_ref[ti, 0] = _bonus(r, k_h, v, rk_ref[...])
        bz_ref[ti, 1] = _silu(zr)

    rr = _iota((PAIR, PAIR), 0)
    cc = _iota((PAIR, PAIR), 1)
    m_bd = (rr // HEAD_DIM) == (cc // HEAD_DIM)
    t_row = _iota((CHUNK, PAIR), 0)
    s_col = _iota((CHUNK, PAIR), 1) % HEAD_DIM
    m_sl = s_col < t_row
    m_l = s_col <= t_row

    rows = [pl.ds(c * CHUNK, CHUNK) for c in range(n_chunks)]

    def transfer(ti):
        units = []
        for hp in range(N_PAIRS):
            for rw in rows:
                kh, bt, kt, bc, kc, v = [opb_ref[ti, i, hp, rw, :] for i in range(6)]
                units.append((kh, rh_ref[ti, hp, rw, :], bt, kt, bc, kc, v))
        return _chunk_transfer(units, m_bd, m_sl, m_l)

    def advance_states(trs):
        chains = [(ti, hp) for ti in tiles for hp in range(N_PAIRS)]
        old = [st_ref[ti, hp] for ti, hp in chains]
        states = [jnp.where(first, 0.0, s) for s in old]
        for c in range(n_chunks):
            for n, (ti, hp) in enumerate(chains):
                y_w, y_c, p_t, q_t = trs[ti][hp * n_chunks + c]
                g_c = gc_ref[ti, c:c + 1, hp * PAIR:(hp + 1) * PAIR]
                ys_ref[ti, hp, rows[c], :] = _dot_nt(y_w, states[n]) + y_c
                states[n] = states[n] * g_c + _dot(states[n], p_t) + q_t
        for n, (ti, hp) in enumerate(chains):
            sp = jnp.where(live, states[n], old[n])
            st_ref[ti, hp] = sp
            sfin_ref[ti, 0, 2 * hp] = sp[:HEAD_DIM, :HEAD_DIM]
            sfin_ref[ti, 0, 2 * hp + 1] = sp[HEAD_DIM:, HEAD_DIM:]

    def finish_previous(ti):
        ys = jnp.concatenate([ys_ref[ti, hp] for hp in range(N_PAIRS)], axis=1)
        y_r = (_group_norm(ys, lnw_ref[...], lnb_ref[...]) + bz_ref[ti, 0]) * bz_ref[ti, 1]
        yr_ref[ti, 0] = y_r.astype(BF16)
        acc = y_r.reshape(tm // SUBLANES, SUBLANES, D_MODEL).sum(axis=0)
        acc = sum(acc[:, c * LANES:(c + 1) * LANES] for c in range(D_MODEL // LANES))
        bits = lax.bitcast_convert_type(acc[0:1, :], jnp.uint32)
        bits = lax.shift_right_logical(lax.shift_right_logical(bits, jnp.uint32(16)), jnp.uint32(16))
        return lax.bitcast_convert_type(bits, F32)

    ps = [project(ti) for ti in tiles]
    zeros = [finish_previous(ti) for ti in tiles]
    trs = []
    for ti in tiles:
        after = jnp.concatenate([zeros[ti]] * (N_RW // LANES), axis=1)
        pointwise(ti, ps[ti], after)
        trs.append(transfer(ti))
    advance_states(trs)


def _const_spec(shape):
    nd = len(shape)
    return pl.BlockSpec(shape, lambda *_: (0,) * nd)


def _row(a):
    return a.reshape(1, -1)


def _rwkv_consts(wts):
    return [_row(wts["norm_g"]), wts["w_rw"], _row(wts["mu_rw"]), _row(wts["w0"]), wts["lora_w"],
            _row(wts["a0"]), _row(wts["k_k"]), _row(wts["k_a"]), _row(wts["r_k"])]


def _rwkv_prompt(x, wts, tm=256):
    bsz, seq, _ = x.shape
    nb = bsz // TILES
    x4 = x.reshape(TILES, nb, seq, D_MODEL)
    consts = _rwkv_consts(wts) + [_row(wts["ln_w"]), _row(wts["ln_b"])]
    const_specs = [_const_spec(c.shape) for c in consts]
    const_specs[1] = pl.BlockSpec(consts[1].shape, lambda *_: (0, 0), pipeline_mode=pl.Buffered(1))
    n_tiles = seq // tm
    n_steps = nb * n_tiles + 1
    cur = lambda g: jnp.minimum(g, n_steps - 2)
    prv = lambda g: jnp.maximum(g - 1, 0)
    kern = functools.partial(_rwkv_prompt_kernel, tm=tm, n_tiles=n_tiles, n_steps=n_steps)
    yr, s_fin = pl.pallas_call(
        kern,
        grid=(n_steps,),
        in_specs=[pl.BlockSpec((TILES, 1, tm, D_MODEL),
                               lambda g: (0, cur(g) // n_tiles, cur(g) % n_tiles, 0))] + const_specs,
        out_specs=[pl.BlockSpec((TILES, 1, tm, D_MODEL),
                                lambda g: (0, prv(g) // n_tiles, prv(g) % n_tiles, 0)),
                   pl.BlockSpec((TILES, 1, N_HEADS, HEAD_DIM, HEAD_DIM),
                                lambda g: (0, cur(g) // n_tiles, 0, 0, 0))],
        out_shape=[jax.ShapeDtypeStruct((TILES, nb, seq, D_MODEL), BF16),
                   jax.ShapeDtypeStruct((TILES, nb, N_HEADS, HEAD_DIM, HEAD_DIM), F32)],
        scratch_shapes=[pltpu.VMEM((TILES, SUBLANES, N_RW), F32),
                        pltpu.VMEM((TILES, N_PAIRS, PAIR, PAIR), F32),
                        pltpu.VMEM((TILES, 6, N_PAIRS, tm, PAIR), BF16),
                        pltpu.VMEM((TILES, N_PAIRS, tm, PAIR), F32),
                        pltpu.VMEM((TILES, SUBLANES, D_MODEL), F32),
                        pltpu.VMEM((TILES, 2, tm, D_MODEL), F32),
                        pltpu.VMEM((TILES, N_PAIRS, tm, PAIR), F32)],
        compiler_params=pltpu.CompilerParams(
            dimension_semantics=("arbitrary",), vmem_limit_bytes=VMEM_LIMIT),
        name="rwkv_prompt",
    )(x4, *consts)
    return (yr.reshape(bsz, seq, D_MODEL),
            s_fin.reshape(bsz, N_HEADS, HEAD_DIM, HEAD_DIM))


def _conv_gate(p, u, u1, u2, cw):
    conv = cw[0:1, :] * u2 + cw[1:2, :] * u1 + cw[2:3, :] * u
    return p[:, D_MODEL:2 * D_MODEL] * conv * _silu(p[:, 3 * D_MODEL:])


def _conv_kernel(x_ref, g_ref, w_ref, cw_ref, yc_ref, u_ref, h_ref, carry_ref, *, tm):
    @pl.when(pl.program_id(1) == 0)
    def _():
        carry_ref[...] = jnp.zeros_like(carry_ref)

    hf = _rms(x_ref[0], g_ref[...])
    p = jnp.dot(hf.astype(BF16), w_ref[...], preferred_element_type=F32)
    u = p[:, 2 * D_MODEL:3 * D_MODEL] * p[:, 0:D_MODEL]
    carry = carry_ref[...]
    y_c = _conv_gate(p, u, _shift_rows(u, 1, carry), _shift_rows(u, 2, carry), cw_ref[...])
    carry_ref[...] = u[tm - SUBLANES:, :]
    u_ref[0] = u[tm - SUBLANES:, :]
    h_ref[0] = hf[tm - SUBLANES:, :]
    yc_ref[0] = y_c.astype(BF16)


def _conv_prompt(x, wts, tm=512):
    bsz, seq, _ = x.shape
    consts = [_row(wts["norm_g"]), wts["w_conv"], wts["conv_w"]]
    tail = pl.BlockSpec((1, SUBLANES, D_MODEL), lambda b, s: (b, 0, 0))
    return pl.pallas_call(
        functools.partial(_conv_kernel, tm=tm),
        grid=(bsz, seq // tm),
        in_specs=[pl.BlockSpec((1, tm, D_MODEL), lambda b, s: (b, s, 0))]
        + [_const_spec(c.shape) for c in consts],
        out_specs=[pl.BlockSpec((1, tm, D_MODEL), lambda b, s: (b, s, 0)), tail, tail],
        out_shape=[jax.ShapeDtypeStruct((bsz, seq, D_MODEL), BF16),
                   jax.ShapeDtypeStruct((bsz, SUBLANES, D_MODEL), F32),
                   jax.ShapeDtypeStruct((bsz, SUBLANES, D_MODEL), F32)],
        scratch_shapes=[pltpu.VMEM((SUBLANES, D_MODEL), F32)],
        compiler_params=pltpu.CompilerParams(
            dimension_semantics=("arbitrary", "arbitrary"), vmem_limit_bytes=VMEM_LIMIT),
        name="conv_prompt",
    )(x, *consts)


def _merge_out(x, yc, yr, g_ref, wg_ref, woc_ref, wor_ref, wo_ref, fg_ref):
    h = _rms(x, g_ref[...]).astype(BF16)
    gates = jnp.dot(h, wg_ref[...], preferred_element_type=F32)
    pc = jnp.dot(yc, woc_ref[...], preferred_element_type=F32)
    pr = jnp.dot(yr, wor_ref[...], preferred_element_type=F32)
    m = _sigmoid(gates[:, :D_MODEL]) * pc + _sigmoid(gates[:, D_MODEL:]) * pr
    out = jnp.dot(m.astype(BF16), wo_ref[...], preferred_element_type=F32)
    return _rms(x + out, fg_ref[...])


def _out_consts(wts):
    return [_row(wts["norm_g"]), wts["w_gate"], wts["w_out_c"], wts["w_out_r"], wts["w_o"],
            _row(wts["final_g"])]


def _out_kernel(x_ref, yc_ref, yr_ref, g_ref, wg_ref, woc_ref, wor_ref, wo_ref, fg_ref, y_ref):
    y_ref[...] = _merge_out(x_ref[...], yc_ref[...], yr_ref[...], g_ref, wg_ref, woc_ref, wor_ref,
                            wo_ref, fg_ref)


def _out_stage(x2d, yc, yr, wts, tm=512):
    rows = x2d.shape[0]
    consts = _out_consts(wts)
    blk = pl.BlockSpec((tm, D_MODEL), lambda i: (i, 0))
    return pl.pallas_call(
        _out_kernel,
        grid=(rows // tm,),
        in_specs=[blk, blk, blk] + [_const_spec(c.shape) for c in consts],
        out_specs=blk,
        out_shape=jax.ShapeDtypeStruct((rows, D_MODEL), F32),
        compiler_params=pltpu.CompilerParams(
            dimension_semantics=("arbitrary",), vmem_limit_bytes=VMEM_LIMIT),
        name="out_stage",
    )(x2d, yc, yr, *consts)


N_SCAN_OPS = 6
V_BLOCK = 8


def _sample_front_kernel(x_ref, hl_ref, cb0_ref, cb1_ref, g_ref, wc_ref, cw_ref, w_ref, mu_ref,
                         w0_ref, lw_ref, a0_ref, kk_ref, ka_ref, rk_ref,
                         yc_ref, u_ref, h_ref, tr_ref, bz_ref,
                         hp_s, u1_s, u2_s):
    @pl.when(pl.program_id(0) == 0)
    def _():
        hp_s[...] = hl_ref[...]
        u1_s[...] = cb1_ref[...]
        u2_s[...] = cb0_ref[...]

    nb = x_ref.shape[1]
    hf = _rms(x_ref[0], g_ref[...])
    h = hf.astype(BF16)
    pc = jnp.dot(h, wc_ref[...], preferred_element_type=F32)
    u = pc[:, 2 * D_MODEL:3 * D_MODEL] * pc[:, 0:D_MODEL]
    u1 = u1_s[...]
    yc_ref[0] = _conv_gate(pc, u, u1, u2_s[...], cw_ref[...]).astype(BF16)
    u2_s[...] = u1
    u1_s[...] = u
    u_ref[0] = u
    h_ref[0] = hf
    both = jnp.dot(jnp.concatenate([h, hp_s[...].astype(BF16)], axis=0), w_ref[...],
                   preferred_element_type=F32)
    hp_s[...] = hf
    p, pp = both[:nb], both[nb:]
    xm = p + (pp - p) * mu_ref[...]
    r, k_h, v, kk, b, logd, zr = _rwkv_pointwise(xm, w0_ref[...], lw_ref[...], a0_ref[...],
                                                 kk_ref[...], ka_ref[...])
    for i, arr in enumerate((kk, b, jnp.exp(logd), k_h, r, v)):
        tr_ref[0, i] = arr.T
    bz_ref[0, 0] = _bonus(r, k_h, v, rk_ref[...])
    bz_ref[0, 1] = _silu(zr)


def _sample_front(xs_t, h_last, cb0, cb1, wts):
    n_tok, nb, _ = xs_t.shape
    consts = ([_row(wts["norm_g"]), wts["w_conv"], wts["conv_w"]] + _rwkv_consts(wts)[1:])
    tok = pl.BlockSpec((1, nb, D_MODEL), lambda t: (t, 0, 0))
    seq = pl.BlockSpec((nb, D_MODEL), lambda t: (0, 0))
    return pl.pallas_call(
        _sample_front_kernel,
        grid=(n_tok,),
        in_specs=[tok, seq, seq, seq] + [_const_spec(c.shape) for c in consts],
        out_specs=[tok, tok, tok,
                   pl.BlockSpec((1, N_SCAN_OPS, D_MODEL, nb), lambda t: (t, 0, 0, 0)),
                   pl.BlockSpec((1, 2, nb, D_MODEL), lambda t: (t, 0, 0, 0))],
        out_shape=[jax.ShapeDtypeStruct((n_tok, nb, D_MODEL), BF16),
                   jax.ShapeDtypeStruct((n_tok, nb, D_MODEL), F32),
                   jax.ShapeDtypeStruct((n_tok, nb, D_MODEL), F32),
                   jax.ShapeDtypeStruct((n_tok, N_SCAN_OPS, D_MODEL, nb), F32),
                   jax.ShapeDtypeStruct((n_tok, 2, nb, D_MODEL), F32)],
        scratch_shapes=[pltpu.VMEM((nb, D_MODEL), F32) for _ in range(3)],
        compiler_params=pltpu.CompilerParams(
            dimension_semantics=("arbitrary",), vmem_limit_bytes=VMEM_LIMIT),
        name="sample_front",
    )(xs_t, h_last, cb0, cb1, *consts)


def _sample_scan_kernel(s0_ref, tr_ref, s1_ref, y_ref, *, n_tok):
    def block(vb, carry):
        rows = pl.ds(pl.multiple_of(vb * V_BLOCK, V_BLOCK), V_BLOCK)
        s = s0_ref[0, rows]
        for t in range(n_tok):
            kk, b, d, k, r = [tr_ref[t, i][None] for i in range(5)]
            vv = tr_ref[t, 5, rows, :][:, None, :]
            sa = -jnp.sum(s * kk, axis=1, keepdims=True)
            s = s * d + sa * b + vv * k
            y_ref[t, rows, :] = jnp.sum(s * r, axis=1)
        s1_ref[0, rows] = s
        return carry

    lax.fori_loop(0, HEAD_DIM // V_BLOCK, block, 0)


def _sample_scan(s0_t, tr):
    n_tok = tr.shape[0]
    nb = s0_t.shape[-1]
    sblk = pl.BlockSpec((1, HEAD_DIM, HEAD_DIM, nb), lambda h: (h, 0, 0, 0))
    return pl.pallas_call(
        functools.partial(_sample_scan_kernel, n_tok=n_tok),
        grid=(N_HEADS,),
        in_specs=[sblk, pl.BlockSpec((n_tok, N_SCAN_OPS, HEAD_DIM, nb), lambda h: (0, 0, h, 0))],
        out_specs=[sblk, pl.BlockSpec((n_tok, HEAD_DIM, nb), lambda h: (0, h, 0))],
        out_shape=[jax.ShapeDtypeStruct(s0_t.shape, F32),
                   jax.ShapeDtypeStruct((n_tok, D_MODEL, nb), F32)],
        compiler_params=pltpu.CompilerParams(
            dimension_semantics=("arbitrary",), vmem_limit_bytes=VMEM_LIMIT),
        name="sample_scan",
    )(s0_t, tr)


def _sample_out_kernel(x_ref, yc_ref, yt_ref, bz_ref, lnw_ref, lnb_ref,
                       g_ref, wg_ref, woc_ref, wor_ref, wo_ref, fg_ref, y_ref):
    ys = yt_ref[0].T
    y_r = (_group_norm(ys, lnw_ref[...], lnb_ref[...]) + bz_ref[0, 0]) * bz_ref[0, 1]
    y_ref[0] = _merge_out(x_ref[0], yc_ref[0], y_r.astype(BF16), g_ref, wg_ref, woc_ref, wor_ref,
                          wo_ref, fg_ref)


def _sample_out(xs_t, yc, y_t, bz, wts):
    n_tok, nb, _ = xs_t.shape
    consts = [_row(wts["ln_w"]), _row(wts["ln_b"])] + _out_consts(wts)
    tok = pl.BlockSpec((1, nb, D_MODEL), lambda t: (t, 0, 0))
    return pl.pallas_call(
        _sample_out_kernel,
        grid=(n_tok,),
        in_specs=[tok, tok, pl.BlockSpec((1, D_MODEL, nb), lambda t: (t, 0, 0)),
                  pl.BlockSpec((1, 2, nb, D_MODEL), lambda t: (t, 0, 0, 0))]
        + [_const_spec(c.shape) for c in consts],
        out_specs=tok,
        out_shape=jax.ShapeDtypeStruct((n_tok, nb, D_MODEL), F32),
        compiler_params=pltpu.CompilerParams(
            dimension_semantics=("arbitrary",), vmem_limit_bytes=VMEM_LIMIT),
        name="sample_out",
    )(xs_t, yc, y_t, bz, *consts)


def _prep_weights(norm_g, w_in, conv_w, mu_shift, w0, w_up, a0, a_up, k_k, k_a, r_k, ln_w, ln_b,
                  w_out_c, w_out_r, w_o, final_g):
    d = D_MODEL
    c0 = 4 * d
    zeros = jnp.zeros((LORA, d), F32)
    lora_w = jnp.concatenate([jnp.concatenate([w_up, zeros], axis=1),
                              jnp.concatenate([zeros, a_up], axis=1)], axis=0)
    return dict(
        norm_g=norm_g, final_g=final_g, conv_w=conv_w,
        w_conv=w_in[:, :c0].astype(BF16),
        w_rw=w_in[:, c0:c0 + N_RW].astype(BF16),
        mu_rw=mu_shift,
        w_gate=w_in[:, c0 + N_RW:].astype(BF16),
        w0=w0, a0=a0, lora_w=lora_w.astype(BF16), k_k=k_k, k_a=k_a, r_k=r_k.reshape(-1),
        ln_w=ln_w, ln_b=ln_b,
        w_out_c=w_out_c.astype(BF16), w_out_r=w_out_r.astype(BF16), w_o=w_o.astype(BF16))


def _layer_prompt(x, wts):
    bsz, seq, d = x.shape
    yc, u_tail, h_tail = _conv_prompt(x, wts)
    yr, s_fin = _rwkv_prompt(x, wts)
    y = _out_stage(x.reshape(bsz * seq, d), yc.reshape(bsz * seq, d), yr.reshape(bsz * seq, d), wts)
    return (y.reshape(bsz, seq, d), u_tail[:, SUBLANES - 2:, :], h_tail[:, SUBLANES - 1, :], s_fin)


def _layer_sample(x, conv_buf, h_last, s0, wts):
    seq = x.shape[1]
    xs_t = jnp.swapaxes(x, 0, 1)
    yc, u, h, tr, bz = _sample_front(xs_t, h_last, conv_buf[:, 0], conv_buf[:, 1], wts)
    s1_t, y_t = _sample_scan(jnp.transpose(s0, (1, 2, 3, 0)), tr)
    y = _sample_out(xs_t, yc, y_t, bz, wts)
    return (jnp.swapaxes(y, 0, 1), jnp.swapaxes(u[seq - 2:], 0, 1), h[seq - 1],
            jnp.transpose(s1_t, (3, 0, 1, 2)))


def kernel(x_prompt, x_sample, state_conv, state_shift, state_rwkv, norm_g, w_in, conv_w, mu_shift,
           w0, w_up, a0, a_up, k_k, k_a, r_k, ln_w, ln_b, w_out_c, w_out_r, w_o, final_g):
    assert norm_g.shape[0] == 1, "single-layer step"
    wts = _prep_weights(norm_g[0], w_in[0], conv_w[0], mu_shift[0], w0[0], w_up[0], a0[0], a_up[0],
                        k_k[0], k_a[0], r_k[0], ln_w[0], ln_b[0], w_out_c[0], w_out_r[0], w_o[0],
                        final_g)
    y_p, c_p, s_p, r_p = _layer_prompt(x_prompt, wts)
    y_s, c_s, s_s, r_s = _layer_sample(x_sample, state_conv[0], state_shift[0], state_rwkv[0], wts)
    lead = lambda a: a[None]
    return (y_p, y_s, lead(c_p), lead(s_p), lead(r_p), lead(c_s), lead(s_s), lead(r_s))
```

```python
import functools
import math

import jax
import jax.numpy as jnp
from jax import lax
from jax.experimental import pallas as pl
from jax.experimental.pallas import tpu as pltpu

F32 = jnp.float32
BF16 = jnp.bfloat16

D_MODEL = 1024
HEAD_DIM = 64
N_HEADS = D_MODEL // HEAD_DIM
LORA = 64
RMS_EPS = 1e-6
GN_EPS = 64e-5
EXP_M05 = math.exp(-0.5)
LANES = 128
SUBLANES = 8
PAIR = 2 * HEAD_DIM
N_PAIRS = D_MODEL // PAIR
CHUNK = 64
TILES = 1
N_RW = 4 * D_MODEL + 2 * LORA
VMEM_LIMIT = 56 * 1024 * 1024


def _dot(a, b):
    return jnp.dot(a.astype(BF16), b.astype(BF16), preferred_element_type=F32)


def _dot_nt(a, b):
    return lax.dot_general(a.astype(BF16), b.astype(BF16), (((1,), (1,)), ((), ())),
                           preferred_element_type=F32)


def _dot_tn(a, b):
    return lax.dot_general(a.astype(BF16), b.astype(BF16), (((0,), (0,)), ((), ())),
                           preferred_element_type=F32)


def _split(x, n):
    parts = []
    for i in range(n):
        piece = x.astype(BF16)
        parts.append(piece)
        if i + 1 < n:
            x = x - piece.astype(F32)
    return parts


def _dot_exact_lhs(m_bf16, x, n=2):
    return sum(jnp.dot(m_bf16, part, preferred_element_type=F32) for part in _split(x, n))


def _rms(x, g):
    return x * lax.rsqrt(jnp.mean(x * x, axis=-1, keepdims=True) + RMS_EPS) * g


def _sigmoid(x):
    return 1.0 / (1.0 + jnp.exp(-x))


def _silu(x):
    return x * _sigmoid(x)


def _iota(shape, dim):
    return lax.broadcasted_iota(jnp.int32, shape, dim)


def _shift_rows(x, n, carry8):
    rolled = pltpu.roll(x, n, 0)
    head = jnp.where(_iota((SUBLANES, x.shape[1]), 0) < n, pltpu.roll(carry8, n, 0),
                     rolled[:SUBLANES])
    return jnp.concatenate([head, rolled[SUBLANES:]], axis=0)


def _seg_sum(x):
    lo = _iota((1, PAIR), 1) < HEAD_DIM
    tiles = []
    for p in range(N_PAIRS):
        xp = x[:, p * PAIR:(p + 1) * PAIR]
        s_lo = jnp.sum(jnp.where(lo, xp, 0.0), axis=-1, keepdims=True)
        s_hi = jnp.sum(jnp.where(lo, 0.0, xp), axis=-1, keepdims=True)
        tiles.append(jnp.where(lo, s_lo, s_hi))
    return jnp.concatenate(tiles, axis=1)


def _block_cumsum(x, blk):
    rows, width = x.shape
    r = _iota((rows, rows), 0)
    c = _iota((rows, rows), 1)
    m_low = jnp.where(((r // blk) == (c // blk)) & (c <= r), 1.0, 0.0).astype(BF16)
    cum3 = _dot_exact_lhs(m_low, x).reshape(rows // blk, blk, width)
    return cum3, cum3[:, blk - 1:blk, :]


def _rwkv_pointwise(xm, w0, lora_w, a0, k_k, k_a):
    r = xm[:, 0:D_MODEL]
    k = xm[:, D_MODEL:2 * D_MODEL]
    v = xm[:, 2 * D_MODEL:3 * D_MODEL]
    da = xm[:, 3 * D_MODEL:3 * D_MODEL + 2 * LORA]
    zr = xm[:, 3 * D_MODEL + 2 * LORA:]
    lane = _iota(da.shape, 1)
    lora_in = jnp.where(lane < LORA, jnp.tanh(da), da)
    lo = _dot(lora_in, lora_w)
    w_logit = w0 + lo[:, :D_MODEL]
    logd = -EXP_M05 * _sigmoid(w_logit)
    a = _sigmoid(a0 + lo[:, D_MODEL:])
    kkr = k * k_k
    kk = kkr * jnp.minimum(lax.rsqrt(_seg_sum(kkr * kkr)), 1e12)
    k_h = k * (1.0 + (a - 1.0) * k_a)
    b = kk * a
    return r, k_h, v, kk, b, logd, zr


def _group_norm(ys, ln_w, ln_b):
    inv_n = 1.0 / HEAD_DIM
    yc = ys - _seg_sum(ys) * inv_n
    return yc * lax.rsqrt(_seg_sum(yc * yc) * inv_n + GN_EPS) * ln_w + ln_b


def _bonus(r, k_h, v, r_k):
    return _seg_sum(r * k_h * r_k) * v


def _scan_operands(r, k_h, v, kk, b, logd, cum3, tot3):
    shape = cum3.shape
    flat = lambda x: x.reshape(shape[0] * shape[1], shape[2])
    cum = flat(cum3)
    g_c3 = jnp.exp(tot3)
    enc3 = jnp.exp(-cum3)
    enc = flat(enc3)
    e_c = flat(enc3 * g_c3)
    khat = kk * jnp.exp(cum - logd)
    rhat = r * jnp.exp(cum)
    btil = b * enc
    ktil = k_h * enc
    bchk = -(b * e_c)
    kchk = k_h * e_c
    return khat, rhat, btil, ktil, bchk, kchk, g_c3


def _chunk_transfer(units, m_bd, m_sl, m_l):
    m_bd2 = jnp.concatenate([m_bd, m_bd], axis=1)
    zero = jnp.zeros((), BF16)

    def expand(x):
        x = x.astype(BF16)
        x2 = jnp.concatenate([x, x], axis=0)
        return jnp.where(m_bd if x.shape[1] == PAIR else m_bd2, x2, zero)

    gs = [_dot_nt(jnp.concatenate([kh, rh.astype(BF16)], axis=0),
                  jnp.concatenate([expand(bt), expand(kt)], axis=0))
          for kh, rh, bt, kt, _, _, _ in units]
    ps = [jnp.where(m_sl, -g[:CHUNK, :PAIR], 0.0).astype(BF16) for g in gs]
    m_as = [jnp.concatenate([jnp.where(m_sl, g[:CHUNK, PAIR:], 0.0),
                             jnp.where(m_l, g[CHUNK:, PAIR:], 0.0)], axis=0).astype(BF16)
            for g in gs]
    m_rbs = [jnp.where(m_l, -g[CHUNK:, :PAIR], 0.0).astype(BF16) for g in gs]
    zs = [_dot(m_a, expand(u[6])) for m_a, u in zip(m_as, units)]
    eye = jnp.where(m_l & ~m_sl, 1.0, 0.0)
    ts = [eye + p.astype(F32) for p in ps]
    ps = [_dot(p, expand(p)).astype(BF16) for p in ps]
    for _ in range(4):
        rs = [_dot(p, expand(jnp.concatenate([p, t.astype(BF16)], axis=1)))
              for p, t in zip(ps, ts)]
        ps = [r[:, :PAIR].astype(BF16) for r in rs]
        ts = [t + r[:, PAIR:] for t, r in zip(ts, rs)]
    ts = [t + _dot(p, expand(t)) for p, t in zip(ps, ts)]
    xs = [_dot(t, expand(jnp.concatenate([u[0], z[:CHUNK].astype(BF16)], axis=1))).astype(BF16)
          for t, u, z in zip(ts, units, zs)]
    t1s = [_dot(m_rb, expand(x)) for m_rb, x in zip(m_rbs, xs)]
    p_ts = [_dot_tn(x[:, :PAIR], u[4]) for x, u in zip(xs, units)]
    q_ts = [_dot_tn(jnp.concatenate([x[:, PAIR:], u[6]], axis=0),
                    jnp.concatenate([u[4], u[5]], axis=0)) for x, u in zip(xs, units)]
    out = []
    for u, z, t1, p_t, q_t in zip(units, zs, t1s, p_ts, q_ts):
        y_w = (u[1] + t1[:, :PAIR]).astype(BF16)
        y_c = t1[:, PAIR:] + z[CHUNK:]
        out.append((y_w, y_c, jnp.where(m_bd, p_t, 0.0).astype(BF16), jnp.where(m_bd, q_t, 0.0)))
    return out


def _rwkv_prompt_kernel(x_ref, g_ref, w_ref, mu_ref, w0_ref, lw_ref, a0_ref, kk_ref, ka_ref,
                        rk_ref, lnw_ref, lnb_ref,
                        yr_ref, sfin_ref,
                        carry_ref, st_ref, opb_ref, rh_ref, gc_ref, bz_ref, ys_ref,
                        *, tm, n_tiles, n_steps):
    g = pl.program_id(0)
    n_chunks = tm // CHUNK
    tiles = range(TILES)
    first = jnp.minimum(g, n_steps - 2) % n_tiles == 0
    live = g < n_steps - 1

    @pl.when(g == 0)
    def _():
        for ref in (carry_ref, st_ref, bz_ref, ys_ref):
            ref[...] = jnp.zeros_like(ref)

    def project(ti):
        h = _rms(x_ref[ti, 0], g_ref[...]).astype(BF16)
        return jnp.dot(h, w_ref[...], preferred_element_type=F32)

    def pointwise(ti, p, after):
        prev = _shift_rows(p, 1, jnp.where(first, 0.0, carry_ref[ti]))
        carry_ref[ti] = p[tm - SUBLANES:, :]
        xm = p + (prev - p) * (mu_ref[...] + after)
        r, k_h, v, kk, b, logd, zr = _rwkv_pointwise(xm, w0_ref[...], lw_ref[...], a0_ref[...],
                                                     kk_ref[...], ka_ref[...])
        cum3, tot3 = _block_cumsum(logd, CHUNK)
        khat, rhat, btil, ktil, bchk, kchk, g_c3 = _scan_operands(r, k_h, v, kk, b, logd,
                                                                  cum3, tot3)
        for i, arr in enumerate((khat, btil, ktil, bchk, kchk, v)):
            a16 = arr.astype(BF16)
            for hp in range(N_PAIRS):
                opb_ref[ti, i, hp] = a16[:, hp * PAIR:(hp + 1) * PAIR]
        for hp in range(N_PAIRS):
            rh_ref[ti, hp] = rhat[:, hp * PAIR:(hp + 1) * PAIR]
        gc_ref[ti, 0:n_chunks, :] = g_c3.reshape(n_chunks, D_MODEL)
        bz_ref[ti, 0] = _bonus(r, k_h, v, rk_ref[...])
        bz_ref[ti, 1] = _silu(zr)

    rr = _iota((PAIR, PAIR), 0)
    cc = _iota((PAIR, PAIR), 1)
    m_bd = (rr // HEAD_DIM) == (cc // HEAD_DIM)
    t_row = _iota((CHUNK, PAIR), 0)
    s_col = _iota((CHUNK, PAIR), 1) % HEAD_DIM
    m_sl = s_col < t_row
    m_l = s_col <= t_row

    rows = [pl.ds(c * CHUNK, CHUNK) for c in range(n_chunks)]

    def transfer(ti):
        units = []
        for hp in range(N_PAIRS):
            for rw in rows:
                kh, bt, kt, bc, kc, v = [opb_ref[ti, i, hp, rw, :] for i in range(6)]
                units.append((kh, rh_ref[ti, hp, rw, :], bt, kt, bc, kc, v))
        return _chunk_transfer(units, m_bd, m_sl, m_l)

    def advance_states(trs):
        chains = [(ti, hp) for ti in tiles for hp in range(N_PAIRS)]
        old = [st_ref[ti, hp] for ti, hp in chains]
        states = [jnp.where(first, 0.0, s) for s in old]
        for c in range(n_chunks):
            for n, (ti, hp) in enumerate(chains):
                y_w, y_c, p_t, q_t = trs[ti][hp * n_chunks + c]
                g_c = gc_ref[ti, c:c + 1, hp * PAIR:(hp + 1) * PAIR]
                ys_ref[ti, hp, rows[c], :] = _dot_nt(y_w, states[n]) + y_c
                states[n] = states[n] * g_c + _dot(states[n], p_t) + q_t
        for n, (ti, hp) in enumerate(chains):
            sp = jnp.where(live, states[n], old[n])
            st_ref[ti, hp] = sp
            sfin_ref[ti, 0, 2 * hp] = sp[:HEAD_DIM, :HEAD_DIM]
            sfin_ref[ti, 0, 2 * hp + 1] = sp[HEAD_DIM:, HEAD_DIM:]

    def finish_previous(ti):
        ys = jnp.concatenate([ys_ref[ti, hp] for hp in range(N_PAIRS)], axis=1)
        y_r = (_group_norm(ys, lnw_ref[...], lnb_ref[...]) + bz_ref[ti, 0]) * bz_ref[ti, 1]
        yr_ref[ti, 0] = y_r.astype(BF16)
        acc = y_r.reshape(tm // SUBLANES, SUBLANES, D_MODEL).sum(axis=0)
        acc = sum(acc[:, c * LANES:(c + 1) * LANES] for c in range(D_MODEL // LANES))
        bits = lax.bitcast_convert_type(acc[0:1, :], jnp.uint32)
        bits = lax.shift_right_logical(lax.shift_right_logical(bits, jnp.uint32(16)), jnp.uint32(16))
        return lax.bitcast_convert_type(bits, F32)

    ps = [project(ti) for ti in tiles]
    zeros = [finish_previous(ti) for ti in tiles]
    trs = []
    for ti in tiles:
        after = jnp.concatenate([zeros[ti]] * (N_RW // LANES), axis=1)
        pointwise(ti, ps[ti], after)
        trs.append(transfer(ti))
    advance_states(trs)


def _const_spec(shape):
    nd = len(shape)
    return pl.BlockSpec(shape, lambda *_: (0,) * nd)


def _row(a):
    return a.reshape(1, -1)


def _rwkv_consts(wts):
    return [_row(wts["norm_g"]), wts["w_rw"], _row(wts["mu_rw"]), _row(wts["w0"]), wts["lora_w"],
            _row(wts["a0"]), _row(wts["k_k"]), _row(wts["k_a"]), _row(wts["r_k"])]


def _rwkv_prompt(x, wts, tm=256):
    bsz, seq, _ = x.shape
    nb = bsz // TILES
    x4 = x.reshape(TILES, nb, seq, D_MODEL)
    consts = _rwkv_consts(wts) + [_row(wts["ln_w"]), _row(wts["ln_b"])]
    const_specs = [_const_spec(c.shape) for c in consts]
    const_specs[1] = pl.BlockSpec(consts[1].shape, lambda *_: (0, 0), pipeline_mode=pl.Buffered(1))
    n_tiles = seq // tm
    n_steps = nb * n_tiles + 1
    cur = lambda g: jnp.minimum(g, n_steps - 2)
    prv = lambda g: jnp.maximum(g - 1, 0)
    kern = functools.partial(_rwkv_prompt_kernel, tm=tm, n_tiles=n_tiles, n_steps=n_steps)
    yr, s_fin = pl.pallas_call(
        kern,
        grid=(n_steps,),
        in_specs=[pl.BlockSpec((TILES, 1, tm, D_MODEL),
                               lambda g: (0, cur(g) // n_tiles, cur(g) % n_tiles, 0))] + const_specs,
        out_specs=[pl.BlockSpec((TILES, 1, tm, D_MODEL),
                                lambda g: (0, prv(g) // n_tiles, prv(g) % n_tiles, 0)),
                   pl.BlockSpec((TILES, 1, N_HEADS, HEAD_DIM, HEAD_DIM),
                                lambda g: (0, cur(g) // n_tiles, 0, 0, 0))],
        out_shape=[jax.ShapeDtypeStruct((TILES, nb, seq, D_MODEL), BF16),
                   jax.ShapeDtypeStruct((TILES, nb, N_HEADS, HEAD_DIM, HEAD_DIM), F32)],
        scratch_shapes=[pltpu.VMEM((TILES, SUBLANES, N_RW), F32),
                        pltpu.VMEM((TILES, N_PAIRS, PAIR, PAIR), F32),
                        pltpu.VMEM((TILES, 6, N_PAIRS, tm, PAIR), BF16),
                        pltpu.VMEM((TILES, N_PAIRS, tm, PAIR), F32),
                        pltpu.VMEM((TILES, SUBLANES, D_MODEL), F32),
                        pltpu.VMEM((TILES, 2, tm, D_MODEL), F32),
                        pltpu.VMEM((TILES, N_PAIRS, tm, PAIR), F32)],
        compiler_params=pltpu.CompilerParams(
            dimension_semantics=("arbitrary",), vmem_limit_bytes=VMEM_LIMIT),
        name="rwkv_prompt",
    )(x4, *consts)
    return (yr.reshape(bsz, seq, D_MODEL),
            s_fin.reshape(bsz, N_HEADS, HEAD_DIM, HEAD_DIM))


def _conv_gate(p, u, u1, u2, cw):
    conv = cw[0:1, :] * u2 + cw[1:2, :] * u1 + cw[2:3, :] * u
    return p[:, D_MODEL:2 * D_MODEL] * conv * _silu(p[:, 3 * D_MODEL:])


def _conv_kernel(x_ref, g_ref, w_ref, cw_ref, yc_ref, u_ref, h_ref, carry_ref, *, tm):
    @pl.when(pl.program_id(1) == 0)
    def _():
        carry_ref[...] = jnp.zeros_like(carry_ref)

    hf = _rms(x_ref[0], g_ref[...])
    p = jnp.dot(hf.astype(BF16), w_ref[...], preferred_element_type=F32)
    u = p[:, 2 * D_MODEL:3 * D_MODEL] * p[:, 0:D_MODEL]
    carry = carry_ref[...]
    y_c = _conv_gate(p, u, _shift_rows(u, 1, carry), _shift_rows(u, 2, carry), cw_ref[...])
    carry_ref[...] = u[tm - SUBLANES:, :]
    u_ref[0] = u[tm - SUBLANES:, :]
    h_ref[0] = hf[tm - SUBLANES:, :]
    yc_ref[0] = y_c.astype(BF16)


def _conv_prompt(x, wts, tm=512):
    bsz, seq, _ = x.shape
    consts = [_row(wts["norm_g"]), wts["w_conv"], wts["conv_w"]]
    tail = pl.BlockSpec((1, SUBLANES, D_MODEL), lambda b, s: (b, 0, 0))
    return pl.pallas_call(
        functools.partial(_conv_kernel, tm=tm),
        grid=(bsz, seq // tm),
        in_specs=[pl.BlockSpec((1, tm, D_MODEL), lambda b, s: (b, s, 0))]
        + [_const_spec(c.shape) for c in consts],
        out_specs=[pl.BlockSpec((1, tm, D_MODEL), lambda b, s: (b, s, 0)), tail, tail],
        out_shape=[jax.ShapeDtypeStruct((bsz, seq, D_MODEL), BF16),
                   jax.ShapeDtypeStruct((bsz, SUBLANES, D_MODEL), F32),
                   jax.ShapeDtypeStruct((bsz, SUBLANES, D_MODEL), F32)],
        scratch_shapes=[pltpu.VMEM((SUBLANES, D_MODEL), F32)],
        compiler_params=pltpu.CompilerParams(
            dimension_semantics=("arbitrary", "arbitrary"), vmem_limit_bytes=VMEM_LIMIT),
        name="conv_prompt",
    )(x, *consts)


def _merge_out(x, yc, yr, g_ref, wg_ref, woc_ref, wor_ref, wo_ref, fg_ref):
    h = _rms(x, g_ref[...]).astype(BF16)
    gates = jnp.dot(h, wg_ref[...], preferred_element_type=F32)
    pc = jnp.dot(yc, woc_ref[...], preferred_element_type=F32)
    pr = jnp.dot(yr, wor_ref[...], preferred_element_type=F32)
    m = _sigmoid(gates[:, :D_MODEL]) * pc + _sigmoid(gates[:, D_MODEL:]) * pr
    out = jnp.dot(m.astype(BF16), wo_ref[...], preferred_element_type=F32)
    return _rms(x + out, fg_ref[...])


def _out_consts(wts):
    return [_row(wts["norm_g"]), wts["w_gate"], wts["w_out_c"], wts["w_out_r"], wts["w_o"],
            _row(wts["final_g"])]


def _out_kernel(x_ref, yc_ref, yr_ref, g_ref, wg_ref, woc_ref, wor_ref, wo_ref, fg_ref, y_ref):
    y_ref[...] = _merge_out(x_ref[...], yc_ref[...], yr_ref[...], g_ref, wg_ref, woc_ref, wor_ref,
                            wo_ref, fg_ref)


def _out_stage(x2d, yc, yr, wts, tm=512):
    rows = x2d.shape[0]
    consts = _out_consts(wts)
    blk = pl.BlockSpec((tm, D_MODEL), lambda i: (i, 0))
    return pl.pallas_call(
        _out_kernel,
        grid=(rows // tm,),
        in_specs=[blk, blk, blk] + [_const_spec(c.shape) for c in consts],
        out_specs=blk,
        out_shape=jax.ShapeDtypeStruct((rows, D_MODEL), F32),
        compiler_params=pltpu.CompilerParams(
            dimension_semantics=("arbitrary",), vmem_limit_bytes=VMEM_LIMIT),
        name="out_stage",
    )(x2d, yc, yr, *consts)


N_SCAN_OPS = 6
V_BLOCK = 8
V_SUB = 2


def _sample_front_kernel(x_ref, hl_ref, cb0_ref, cb1_ref, g_ref, wc_ref, cw_ref, w_ref, mu_ref,
                         w0_ref, lw_ref, a0_ref, kk_ref, ka_ref, rk_ref,
                         yc_ref, u_ref, h_ref, tr_ref, bz_ref,
                         hp_s, u1_s, u2_s):
    @pl.when(pl.program_id(0) == 0)
    def _():
        hp_s[...] = hl_ref[...]
        u1_s[...] = cb1_ref[...]
        u2_s[...] = cb0_ref[...]

    nb = x_ref.shape[1]
    hf = _rms(x_ref[0], g_ref[...])
    h = hf.astype(BF16)
    pc = jnp.dot(h, wc_ref[...], preferred_element_type=F32)
    u = pc[:, 2 * D_MODEL:3 * D_MODEL] * pc[:, 0:D_MODEL]
    u1 = u1_s[...]
    yc_ref[0] = _conv_gate(pc, u, u1, u2_s[...], cw_ref[...]).astype(BF16)
    u2_s[...] = u1
    u1_s[...] = u
    u_ref[0] = u
    h_ref[0] = hf
    both = jnp.dot(jnp.concatenate([h, hp_s[...].astype(BF16)], axis=0), w_ref[...],
                   preferred_element_type=F32)
    hp_s[...] = hf
    p, pp = both[:nb], both[nb:]
    xm = p + (pp - p) * mu_ref[...]
    r, k_h, v, kk, b, logd, zr = _rwkv_pointwise(xm, w0_ref[...], lw_ref[...], a0_ref[...],
                                                 kk_ref[...], ka_ref[...])
    for i, arr in enumerate((kk, b, jnp.exp(logd), k_h, r, v)):
        tr_ref[0, i] = arr.T
    bz_ref[0, 0] = _bonus(r, k_h, v, rk_ref[...])
    bz_ref[0, 1] = _silu(zr)


def _sample_front(xs_t, h_last, cb0, cb1, wts):
    n_tok, nb, _ = xs_t.shape
    consts = ([_row(wts["norm_g"]), wts["w_conv"], wts["conv_w"]] + _rwkv_consts(wts)[1:])
    tok = pl.BlockSpec((1, nb, D_MODEL), lambda t: (t, 0, 0))
    seq = pl.BlockSpec((nb, D_MODEL), lambda t: (0, 0))
    return pl.pallas_call(
        _sample_front_kernel,
        grid=(n_tok,),
        in_specs=[tok, seq, seq, seq] + [_const_spec(c.shape) for c in consts],
        out_specs=[tok, tok, tok,
                   pl.BlockSpec((1, N_SCAN_OPS, D_MODEL, nb), lambda t: (t, 0, 0, 0)),
                   pl.BlockSpec((1, 2, nb, D_MODEL), lambda t: (t, 0, 0, 0))],
        out_shape=[jax.ShapeDtypeStruct((n_tok, nb, D_MODEL), BF16),
                   jax.ShapeDtypeStruct((n_tok, nb, D_MODEL), F32),
                   jax.ShapeDtypeStruct((n_tok, nb, D_MODEL), F32),
                   jax.ShapeDtypeStruct((n_tok, N_SCAN_OPS, D_MODEL, nb), F32),
                   jax.ShapeDtypeStruct((n_tok, 2, nb, D_MODEL), F32)],
        scratch_shapes=[pltpu.VMEM((nb, D_MODEL), F32) for _ in range(3)],
        compiler_params=pltpu.CompilerParams(
            dimension_semantics=("arbitrary",), vmem_limit_bytes=VMEM_LIMIT),
        name="sample_front",
    )(xs_t, h_last, cb0, cb1, *consts)


def _sample_scan_kernel(s0_ref, tr_ref, s1_ref, y_ref, *, n_tok):
    def block(vb, carry):
        rows = pl.ds(pl.multiple_of(vb * V_BLOCK, V_BLOCK), V_BLOCK)
        ys = [[] for _ in range(n_tok)]
        for v0 in range(0, V_BLOCK, V_SUB):
            s = s0_ref[0, rows][v0:v0 + V_SUB]
            for t in range(n_tok):
                kk, b, d, k, r = [tr_ref[t, i][None] for i in range(5)]
                vv = tr_ref[t, 5, rows, :][v0:v0 + V_SUB, None, :]
                sa = -jnp.sum(s * kk, axis=1, keepdims=True)
                s = s * d + sa * b + vv * k
                ys[t].append(jnp.sum(s * r, axis=1))
            s1_ref[0, pl.ds(pl.multiple_of(vb * V_BLOCK, V_BLOCK) + v0, V_SUB)] = s
        for t in range(n_tok):
            y_ref[t, rows, :] = jnp.concatenate(ys[t], axis=0)
        return carry

    lax.fori_loop(0, HEAD_DIM // V_BLOCK, block, 0)


def _sample_scan(s0_t, tr):
    n_tok = tr.shape[0]
    nb = s0_t.shape[-1]
    sblk = pl.BlockSpec((1, HEAD_DIM, HEAD_DIM, nb), lambda h: (h, 0, 0, 0))
    return pl.pallas_call(
        functools.partial(_sample_scan_kernel, n_tok=n_tok),
        grid=(N_HEADS,),
        in_specs=[sblk, pl.BlockSpec((n_tok, N_SCAN_OPS, HEAD_DIM, nb), lambda h: (0, 0, h, 0))],
        out_specs=[sblk, pl.BlockSpec((n_tok, HEAD_DIM, nb), lambda h: (0, h, 0))],
        out_shape=[jax.ShapeDtypeStruct(s0_t.shape, F32),
                   jax.ShapeDtypeStruct((n_tok, D_MODEL, nb), F32)],
        compiler_params=pltpu.CompilerParams(
            dimension_semantics=("arbitrary",), vmem_limit_bytes=VMEM_LIMIT),
        name="sample_scan",
    )(s0_t, tr)


def _sample_out_kernel(x_ref, yc_ref, yt_ref, bz_ref, lnw_ref, lnb_ref,
                       g_ref, wg_ref, woc_ref, wor_ref, wo_ref, fg_ref, y_ref):
    ys = yt_ref[0].T
    y_r = (_group_norm(ys, lnw_ref[...], lnb_ref[...]) + bz_ref[0, 0]) * bz_ref[0, 1]
    y_ref[0] = _merge_out(x_ref[0], yc_ref[0], y_r.astype(BF16), g_ref, wg_ref, woc_ref, wor_ref,
                          wo_ref, fg_ref)


def _sample_out(xs_t, yc, y_t, bz, wts):
    n_tok, nb, _ = xs_t.shape
    consts = [_row(wts["ln_w"]), _row(wts["ln_b"])] + _out_consts(wts)
    tok = pl.BlockSpec((1, nb, D_MODEL), lambda t: (t, 0, 0))
    return pl.pallas_call(
        _sample_out_kernel,
        grid=(n_tok,),
        in_specs=[tok, tok, pl.BlockSpec((1, D_MODEL, nb), lambda t: (t, 0, 0)),
                  pl.BlockSpec((1, 2, nb, D_MODEL), lambda t: (t, 0, 0, 0))]
        + [_const_spec(c.shape) for c in consts],
        out_specs=tok,
        out_shape=jax.ShapeDtypeStruct((n_tok, nb, D_MODEL), F32),
        compiler_params=pltpu.CompilerParams(
            dimension_semantics=("arbitrary",), vmem_limit_bytes=VMEM_LIMIT),
        name="sample_out",
    )(xs_t, yc, y_t, bz, *consts)


def _prep_weights(norm_g, w_in, conv_w, mu_shift, w0, w_up, a0, a_up, k_k, k_a, r_k, ln_w, ln_b,
                  w_out_c, w_out_r, w_o, final_g):
    d = D_MODEL
    c0 = 4 * d
    zeros = jnp.zeros((LORA, d), F32)
    lora_w = jnp.concatenate([jnp.concatenate([w_up, zeros], axis=1),
                              jnp.concatenate([zeros, a_up], axis=1)], axis=0)
    return dict(
        norm_g=norm_g, final_g=final_g, conv_w=conv_w,
        w_conv=w_in[:, :c0].astype(BF16),
        w_rw=w_in[:, c0:c0 + N_RW].astype(BF16),
        mu_rw=mu_shift,
        w_gate=w_in[:, c0 + N_RW:].astype(BF16),
        w0=w0, a0=a0, lora_w=lora_w.astype(BF16), k_k=k_k, k_a=k_a, r_k=r_k.reshape(-1),
        ln_w=ln_w, ln_b=ln_b,
        w_out_c=w_out_c.astype(BF16), w_out_r=w_out_r.astype(BF16), w_o=w_o.astype(BF16))


def _layer_prompt(x, wts):
    bsz, seq, d = x.shape
    yc, u_tail, h_tail = _conv_prompt(x, wts)
    yr, s_fin = _rwkv_prompt(x, wts)
    y = _out_stage(x.reshape(bsz * seq, d), yc.reshape(bsz * seq, d), yr.reshape(bsz * seq, d), wts)
    return (y.reshape(bsz, seq, d), u_tail[:, SUBLANES - 2:, :], h_tail[:, SUBLANES - 1, :], s_fin)


def _layer_sample(x, conv_buf, h_last, s0, wts):
    seq = x.shape[1]
    xs_t = jnp.swapaxes(x, 0, 1)
    yc, u, h, tr, bz = _sample_front(xs_t, h_last, conv_buf[:, 0], conv_buf[:, 1], wts)
    s1_t, y_t = _sample_scan(jnp.transpose(s0, (1, 2, 3, 0)), tr)
    y = _sample_out(xs_t, yc, y_t, bz, wts)
    return (jnp.swapaxes(y, 0, 1), jnp.swapaxes(u[seq - 2:], 0, 1), h[seq - 1],
            jnp.transpose(s1_t, (3, 0, 1, 2)))


def kernel(x_prompt, x_sample, state_conv, state_shift, state_rwkv, norm_g, w_in, conv_w, mu_shift,
           w0, w_up, a0, a_up, k_k, k_a, r_k, ln_w, ln_b, w_out_c, w_out_r, w_o, final_g):
    assert norm_g.shape[0] == 1, "single-layer step"
    wts = _prep_weights(norm_g[0], w_in[0], conv_w[0], mu_shift[0], w0[0], w_up[0], a0[0], a_up[0],
                        k_k[0], k_a[0], r_k[0], ln_w[0], ln_b[0], w_out_c[0], w_out_r[0], w_o[0],
                        final_g)
    y_p, c_p, s_p, r_p = _layer_prompt(x_prompt, wts)
    y_s, c_s, s_s, r_s = _layer_sample(x_sample, state_conv[0], state_shift[0], state_rwkv[0], wts)
    lead = lambda a: a[None]
    return (y_p, y_s, lead(c_p), lead(s_p), lead(r_p), lead(c_s), lead(s_s), lead(r_s))
```

```python
import functools
import math

import jax
import jax.numpy as jnp
from jax import lax
from jax.experimental import pallas as pl
from jax.experimental.pallas import tpu as pltpu

F32 = jnp.float32
BF16 = jnp.bfloat16

D_MODEL = 1024
HEAD_DIM = 64
N_HEADS = D_MODEL // HEAD_DIM
LORA = 64
RMS_EPS = 1e-6
GN_EPS = 64e-5
EXP_M05 = math.exp(-0.5)
LANES = 128
SUBLANES = 8
PAIR = 2 * HEAD_DIM
N_PAIRS = D_MODEL // PAIR
CHUNK = 64
TILES = 1
N_RW = 4 * D_MODEL + 2 * LORA
VMEM_LIMIT = 56 * 1024 * 1024


def _dot(a, b):
    return jnp.dot(a.astype(BF16), b.astype(BF16), preferred_element_type=F32)


def _dot_nt(a, b):
    return lax.dot_general(a.astype(BF16), b.astype(BF16), (((1,), (1,)), ((), ())),
                           preferred_element_type=F32)


def _dot_tn(a, b):
    return lax.dot_general(a.astype(BF16), b.astype(BF16), (((0,), (0,)), ((), ())),
                           preferred_element_type=F32)


def _split(x, n):
    parts = []
    for i in range(n):
        piece = x.astype(BF16)
        parts.append(piece)
        if i + 1 < n:
            x = x - piece.astype(F32)
    return parts


def _dot_exact_lhs(m_bf16, x, n=2):
    return sum(jnp.dot(m_bf16, part, preferred_element_type=F32) for part in _split(x, n))


def _rms(x, g):
    return x * lax.rsqrt(jnp.mean(x * x, axis=-1, keepdims=True) + RMS_EPS) * g


def _sigmoid(x):
    return 1.0 / (1.0 + jnp.exp(-x))


def _silu(x):
    return x * _sigmoid(x)


def _iota(shape, dim):
    return lax.broadcasted_iota(jnp.int32, shape, dim)


def _shift_rows(x, n, carry8):
    rolled = pltpu.roll(x, n, 0)
    head = jnp.where(_iota((SUBLANES, x.shape[1]), 0) < n, pltpu.roll(carry8, n, 0),
                     rolled[:SUBLANES])
    return jnp.concatenate([head, rolled[SUBLANES:]], axis=0)


def _seg_sum(x):
    lo = _iota((1, PAIR), 1) < HEAD_DIM
    tiles = []
    for p in range(N_PAIRS):
        xp = x[:, p * PAIR:(p + 1) * PAIR]
        s_lo = jnp.sum(jnp.where(lo, xp, 0.0), axis=-1, keepdims=True)
        s_hi = jnp.sum(jnp.where(lo, 0.0, xp), axis=-1, keepdims=True)
        tiles.append(jnp.where(lo, s_lo, s_hi))
    return jnp.concatenate(tiles, axis=1)


def _block_cumsum(x, blk):
    rows, width = x.shape
    r = _iota((rows, rows), 0)
    c = _iota((rows, rows), 1)
    m_low = jnp.where(((r // blk) == (c // blk)) & (c <= r), 1.0, 0.0).astype(BF16)
    cum3 = _dot_exact_lhs(m_low, x).reshape(rows // blk, blk, width)
    return cum3, cum3[:, blk - 1:blk, :]


def _rwkv_pointwise(xm, w0, lora_w, a0, k_k, k_a):
    r = xm[:, 0:D_MODEL]
    k = xm[:, D_MODEL:2 * D_MODEL]
    v = xm[:, 2 * D_MODEL:3 * D_MODEL]
    da = xm[:, 3 * D_MODEL:3 * D_MODEL + 2 * LORA]
    zr = xm[:, 3 * D_MODEL + 2 * LORA:]
    lane = _iota(da.shape, 1)
    lora_in = jnp.where(lane < LORA, jnp.tanh(da), da)
    lo = _dot(lora_in, lora_w)
    w_logit = w0 + lo[:, :D_MODEL]
    logd = -EXP_M05 * _sigmoid(w_logit)
    a = _sigmoid(a0 + lo[:, D_MODEL:])
    kkr = k * k_k
    kk = kkr * jnp.minimum(lax.rsqrt(_seg_sum(kkr * kkr)), 1e12)
    k_h = k * (1.0 + (a - 1.0) * k_a)
    b = kk * a
    return r, k_h, v, kk, b, logd, zr


def _group_norm(ys, ln_w, ln_b):
    inv_n = 1.0 / HEAD_DIM
    yc = ys - _seg_sum(ys) * inv_n
    return yc * lax.rsqrt(_seg_sum(yc * yc) * inv_n + GN_EPS) * ln_w + ln_b


def _bonus(r, k_h, v, r_k):
    return _seg_sum(r * k_h * r_k) * v


def _scan_operands(r, k_h, v, kk, b, logd, cum3, tot3):
    shape = cum3.shape
    flat = lambda x: x.reshape(shape[0] * shape[1], shape[2])
    cum = flat(cum3)
    g_c3 = jnp.exp(tot3)
    enc3 = jnp.exp(-cum3)
    enc = flat(enc3)
    e_c = flat(enc3 * g_c3)
    khat = kk * jnp.exp(cum - logd)
    rhat = r * jnp.exp(cum)
    btil = b * enc
    ktil = k_h * enc
    bchk = -(b * e_c)
    kchk = k_h * e_c
    return khat, rhat, btil, ktil, bchk, kchk, g_c3


def _chunk_transfer(units, m_bd, m_sl, m_l):
    m_bd2 = jnp.concatenate([m_bd, m_bd], axis=1)
    zero = jnp.zeros((), BF16)

    def expand(x):
        x = x.astype(BF16)
        x2 = jnp.concatenate([x, x], axis=0)
        return jnp.where(m_bd if x.shape[1] == PAIR else m_bd2, x2, zero)

    gs = [_dot_nt(jnp.concatenate([kh, rh.astype(BF16)], axis=0),
                  jnp.concatenate([expand(bt), expand(kt)], axis=0))
          for kh, rh, bt, kt, _, _, _ in units]
    ps = [jnp.where(m_sl, -g[:CHUNK, :PAIR], 0.0).astype(BF16) for g in gs]
    m_as = [jnp.concatenate([jnp.where(m_sl, g[:CHUNK, PAIR:], 0.0),
                             jnp.where(m_l, g[CHUNK:, PAIR:], 0.0)], axis=0).astype(BF16)
            for g in gs]
    m_rbs = [jnp.where(m_l, -g[CHUNK:, :PAIR], 0.0).astype(BF16) for g in gs]
    zs = [_dot(m_a, expand(u[6])) for m_a, u in zip(m_as, units)]
    eye = jnp.where(m_l & ~m_sl, 1.0, 0.0)
    ts = [eye + p.astype(F32) for p in ps]
    ps = [_dot(p, expand(p)).astype(BF16) for p in ps]
    for _ in range(4):
        rs = [_dot(p, expand(jnp.concatenate([p, t.astype(BF16)], axis=1)))
              for p, t in zip(ps, ts)]
        ps = [r[:, :PAIR].astype(BF16) for r in rs]
        ts = [t + r[:, PAIR:] for t, r in zip(ts, rs)]
    ts = [t + _dot(p, expand(t)) for p, t in zip(ps, ts)]
    xs = [_dot(t, expand(jnp.concatenate([u[0], z[:CHUNK].astype(BF16)], axis=1))).astype(BF16)
          for t, u, z in zip(ts, units, zs)]
    t1s = [_dot(m_rb, expand(x)) for m_rb, x in zip(m_rbs, xs)]
    p_ts = [_dot_tn(x[:, :PAIR], u[4]) for x, u in zip(xs, units)]
    q_ts = [_dot_tn(jnp.concatenate([x[:, PAIR:], u[6]], axis=0),
                    jnp.concatenate([u[4], u[5]], axis=0)) for x, u in zip(xs, units)]
    out = []
    for u, z, t1, p_t, q_t in zip(units, zs, t1s, p_ts, q_ts):
        y_w = (u[1] + t1[:, :PAIR]).astype(BF16)
        y_c = t1[:, PAIR:] + z[CHUNK:]
        out.append((y_w, y_c, jnp.where(m_bd, p_t, 0.0).astype(BF16), jnp.where(m_bd, q_t, 0.0)))
    return out


def _rwkv_prompt_kernel(x_ref, g_ref, w_ref, mu_ref, w0_ref, lw_ref, a0_ref, kk_ref, ka_ref,
                        rk_ref, lnw_ref, lnb_ref,
                        yr_ref, sfin_ref,
                        carry_ref, st_ref, opb_ref, rh_ref, gc_ref, bz_ref, ys_ref,
                        *, tm, n_tiles, n_steps):
    g = pl.program_id(0)
    n_chunks = tm // CHUNK
    tiles = range(TILES)
    first = jnp.minimum(g, n_steps - 2) % n_tiles == 0
    live = g < n_steps - 1

    @pl.when(g == 0)
    def _():
        for ref in (carry_ref, st_ref, bz_ref, ys_ref):
            ref[...] = jnp.zeros_like(ref)

    def project(ti):
        h = _rms(x_ref[ti, 0], g_ref[...]).astype(BF16)
        return jnp.dot(h, w_ref[...], preferred_element_type=F32)

    def pointwise(ti, p, after):
        prev = _shift_rows(p, 1, jnp.where(first, 0.0, carry_ref[ti]))
        carry_ref[ti] = p[tm - SUBLANES:, :]
        xm = p + (prev - p) * (mu_ref[...] + after)
        r, k_h, v, kk, b, logd, zr = _rwkv_pointwise(xm, w0_ref[...], lw_ref[...], a0_ref[...],
                                                     kk_ref[...], ka_ref[...])
        cum3, tot3 = _block_cumsum(logd, CHUNK)
        khat, rhat, btil, ktil, bchk, kchk, g_c3 = _scan_operands(r, k_h, v, kk, b, logd,
                                                                  cum3, tot3)
        for i, arr in enumerate((khat, btil, ktil, bchk, kchk, v)):
            a16 = arr.astype(BF16)
            for hp in range(N_PAIRS):
                opb_ref[ti, i, hp] = a16[:, hp * PAIR:(hp + 1) * PAIR]
        for hp in range(N_PAIRS):
            rh_ref[ti, hp] = rhat[:, hp * PAIR:(hp + 1) * PAIR]
        gc_ref[ti, 0:n_chunks, :] = g_c3.reshape(n_chunks, D_MODEL)
        bz_ref[ti, 0] = _bonus(r, k_h, v, rk_ref[...])
        bz_ref[ti, 1] = _silu(zr)

    rr = _iota((PAIR, PAIR), 0)
    cc = _iota((PAIR, PAIR), 1)
    m_bd = (rr // HEAD_DIM) == (cc // HEAD_DIM)
    t_row = _iota((CHUNK, PAIR), 0)
    s_col = _iota((CHUNK, PAIR), 1) % HEAD_DIM
    m_sl = s_col < t_row
    m_l = s_col <= t_row

    rows = [pl.ds(c * CHUNK, CHUNK) for c in range(n_chunks)]

    def transfer(ti):
        units = []
        for hp in range(N_PAIRS):
            for rw in rows:
                kh, bt, kt, bc, kc, v = [opb_ref[ti, i, hp, rw, :] for i in range(6)]
                units.append((kh, rh_ref[ti, hp, rw, :], bt, kt, bc, kc, v))
        return _chunk_transfer(units, m_bd, m_sl, m_l)

    def advance_states(trs):
        chains = [(ti, hp) for ti in tiles for hp in range(N_PAIRS)]
        old = [st_ref[ti, hp] for ti, hp in chains]
        states = [jnp.where(first, 0.0, s) for s in old]
        for c in range(n_chunks):
            for n, (ti, hp) in enumerate(chains):
                y_w, y_c, p_t, q_t = trs[ti][hp * n_chunks + c]
                g_c = gc_ref[ti, c:c + 1, hp * PAIR:(hp + 1) * PAIR]
                ys_ref[ti, hp, rows[c], :] = _dot_nt(y_w, states[n]) + y_c
                states[n] = states[n] * g_c + _dot(states[n], p_t) + q_t
        for n, (ti, hp) in enumerate(chains):
            sp = jnp.where(live, states[n], old[n])
            st_ref[ti, hp] = sp
            sfin_ref[ti, 0, 2 * hp] = sp[:HEAD_DIM, :HEAD_DIM]
            sfin_ref[ti, 0, 2 * hp + 1] = sp[HEAD_DIM:, HEAD_DIM:]

    def finish_previous(ti):
        ys = jnp.concatenate([ys_ref[ti, hp] for hp in range(N_PAIRS)], axis=1)
        y_r = (_group_norm(ys, lnw_ref[...], lnb_ref[...]) + bz_ref[ti, 0]) * bz_ref[ti, 1]
        yr_ref[ti, 0] = y_r.astype(BF16)
        acc = y_r.reshape(tm // SUBLANES, SUBLANES, D_MODEL).sum(axis=0)
        acc = sum(acc[:, c * LANES:(c + 1) * LANES] for c in range(D_MODEL // LANES))
        bits = lax.bitcast_convert_type(acc[0:1, :], jnp.uint32)
        bits = lax.shift_right_logical(lax.shift_right_logical(bits, jnp.uint32(16)), jnp.uint32(16))
        return lax.bitcast_convert_type(bits, F32)

    ps = [project(ti) for ti in tiles]
    zeros = [finish_previous(ti) for ti in tiles]
    trs = []
    for ti in tiles:
        after = jnp.concatenate([zeros[ti]] * (N_RW // LANES), axis=1)
        pointwise(ti, ps[ti], after)
        trs.append(transfer(ti))
    advance_states(trs)


def _const_spec(shape):
    nd = len(shape)
    return pl.BlockSpec(shape, lambda *_: (0,) * nd)


def _row(a):
    return a.reshape(1, -1)


def _rwkv_consts(wts):
    return [_row(wts["norm_g"]), wts["w_rw"], _row(wts["mu_rw"]), _row(wts["w0"]), wts["lora_w"],
            _row(wts["a0"]), _row(wts["k_k"]), _row(wts["k_a"]), _row(wts["r_k"])]


def _rwkv_prompt(x, wts, tm=256):
    bsz, seq, _ = x.shape
    nb = bsz // TILES
    x4 = x.reshape(TILES, nb, seq, D_MODEL)
    consts = _rwkv_consts(wts) + [_row(wts["ln_w"]), _row(wts["ln_b"])]
    const_specs = [_const_spec(c.shape) for c in consts]
    const_specs[1] = pl.BlockSpec(consts[1].shape, lambda *_: (0, 0), pipeline_mode=pl.Buffered(1))
    n_tiles = seq // tm
    n_steps = nb * n_tiles + 1
    cur = lambda g: jnp.minimum(g, n_steps - 2)
    prv = lambda g: jnp.maximum(g - 1, 0)
    kern = functools.partial(_rwkv_prompt_kernel, tm=tm, n_tiles=n_tiles, n_steps=n_steps)
    yr, s_fin = pl.pallas_call(
        kern,
        grid=(n_steps,),
        in_specs=[pl.BlockSpec((TILES, 1, tm, D_MODEL),
                               lambda g: (0, cur(g) // n_tiles, cur(g) % n_tiles, 0))] + const_specs,
        out_specs=[pl.BlockSpec((TILES, 1, tm, D_MODEL),
                                lambda g: (0, prv(g) // n_tiles, prv(g) % n_tiles, 0)),
                   pl.BlockSpec((TILES, 1, N_HEADS, HEAD_DIM, HEAD_DIM),
                                lambda g: (0, cur(g) // n_tiles, 0, 0, 0))],
        out_shape=[jax.ShapeDtypeStruct((TILES, nb, seq, D_MODEL), BF16),
                   jax.ShapeDtypeStruct((TILES, nb, N_HEADS, HEAD_DIM, HEAD_DIM), F32)],
        scratch_shapes=[pltpu.VMEM((TILES, SUBLANES, N_RW), F32),
                        pltpu.VMEM((TILES, N_PAIRS, PAIR, PAIR), F32),
                        pltpu.VMEM((TILES, 6, N_PAIRS, tm, PAIR), BF16),
                        pltpu.VMEM((TILES, N_PAIRS, tm, PAIR), F32),
                        pltpu.VMEM((TILES, SUBLANES, D_MODEL), F32),
                        pltpu.VMEM((TILES, 2, tm, D_MODEL), F32),
                        pltpu.VMEM((TILES, N_PAIRS, tm, PAIR), F32)],
        compiler_params=pltpu.CompilerParams(
            dimension_semantics=("arbitrary",), vmem_limit_bytes=VMEM_LIMIT),
        name="rwkv_prompt",
    )(x4, *consts)
    return (yr.reshape(bsz, seq, D_MODEL),
            s_fin.reshape(bsz, N_HEADS, HEAD_DIM, HEAD_DIM))


def _conv_gate(p, u, u1, u2, cw):
    conv = cw[0:1, :] * u2 + cw[1:2, :] * u1 + cw[2:3, :] * u
    return p[:, D_MODEL:2 * D_MODEL] * conv * _silu(p[:, 3 * D_MODEL:])


def _merge_out(x, h, yc, yr, wg_ref, woc_ref, wor_ref, wo_ref, fg_ref):
    gates = jnp.dot(h, wg_ref[...], preferred_element_type=F32)
    pr = jnp.dot(yr, wor_ref[...], preferred_element_type=F32)
    pc = jnp.dot(yc, woc_ref[...], preferred_element_type=F32)
    m = _sigmoid(gates[:, :D_MODEL]) * pc + _sigmoid(gates[:, D_MODEL:]) * pr
    out = jnp.dot(m.astype(BF16), wo_ref[...], preferred_element_type=F32)
    return _rms(x + out, fg_ref[...])


def _out_consts(wts):
    return [wts["w_gate"], wts["w_out_c"], wts["w_out_r"], wts["w_o"], _row(wts["final_g"])]


def _conv_out_kernel(x_ref, yr_ref, g_ref, wc_ref, cw_ref, wg_ref, woc_ref, wor_ref, wo_ref, fg_ref,
                     y_ref, u_ref, h_ref, carry_ref, *, tm):
    @pl.when(pl.program_id(1) == 0)
    def _():
        carry_ref[...] = jnp.zeros_like(carry_ref)

    x = x_ref[0]
    hf = _rms(x, g_ref[...])
    h = hf.astype(BF16)
    p = jnp.dot(h, wc_ref[...], preferred_element_type=F32)
    u = p[:, 2 * D_MODEL:3 * D_MODEL] * p[:, 0:D_MODEL]
    carry = carry_ref[...]
    y_c = _conv_gate(p, u, _shift_rows(u, 1, carry), _shift_rows(u, 2, carry), cw_ref[...])
    carry_ref[...] = u[tm - SUBLANES:, :]
    u_ref[0] = u[tm - SUBLANES:, :]
    h_ref[0] = hf[tm - SUBLANES:, :]
    y_ref[0] = _merge_out(x, h, y_c.astype(BF16), yr_ref[0], wg_ref, woc_ref, wor_ref, wo_ref,
                          fg_ref)


def _conv_out_prompt(x, yr, wts, tm=512):
    bsz, seq, _ = x.shape
    consts = [_row(wts["norm_g"]), wts["w_conv"], wts["conv_w"]] + _out_consts(wts)
    single = lambda c: pl.BlockSpec(c.shape, lambda *_: (0,) * c.ndim, pipeline_mode=pl.Buffered(1))
    tile = pl.BlockSpec((1, tm, D_MODEL), lambda b, s: (b, s, 0))
    tail = pl.BlockSpec((1, SUBLANES, D_MODEL), lambda b, s: (b, 0, 0))
    return pl.pallas_call(
        functools.partial(_conv_out_kernel, tm=tm),
        grid=(bsz, seq // tm),
        in_specs=[tile, tile] + [single(c) for c in consts],
        out_specs=[tile, tail, tail],
        out_shape=[jax.ShapeDtypeStruct((bsz, seq, D_MODEL), F32),
                   jax.ShapeDtypeStruct((bsz, SUBLANES, D_MODEL), F32),
                   jax.ShapeDtypeStruct((bsz, SUBLANES, D_MODEL), F32)],
        scratch_shapes=[pltpu.VMEM((SUBLANES, D_MODEL), F32)],
        compiler_params=pltpu.CompilerParams(
            dimension_semantics=("arbitrary", "arbitrary"), vmem_limit_bytes=VMEM_LIMIT),
        name="conv_out_prompt",
    )(x, yr, *consts)


N_SCAN_OPS = 6
V_BLOCK = 8
V_SUB = 2


def _sample_front_kernel(x_ref, hl_ref, cb0_ref, cb1_ref, g_ref, wc_ref, cw_ref, w_ref, mu_ref,
                         w0_ref, lw_ref, a0_ref, kk_ref, ka_ref, rk_ref,
                         yc_ref, u_ref, h_ref, tr_ref, bz_ref,
                         hp_s, u1_s, u2_s):
    @pl.when(pl.program_id(0) == 0)
    def _():
        hp_s[...] = hl_ref[...]
        u1_s[...] = cb1_ref[...]
        u2_s[...] = cb0_ref[...]

    nb = x_ref.shape[1]
    hf = _rms(x_ref[0], g_ref[...])
    h = hf.astype(BF16)
    pc = jnp.dot(h, wc_ref[...], preferred_element_type=F32)
    u = pc[:, 2 * D_MODEL:3 * D_MODEL] * pc[:, 0:D_MODEL]
    u1 = u1_s[...]
    yc_ref[0] = _conv_gate(pc, u, u1, u2_s[...], cw_ref[...]).astype(BF16)
    u2_s[...] = u1
    u1_s[...] = u
    u_ref[0] = u
    h_ref[0] = hf
    both = jnp.dot(jnp.concatenate([h, hp_s[...].astype(BF16)], axis=0), w_ref[...],
                   preferred_element_type=F32)
    hp_s[...] = hf
    p, pp = both[:nb], both[nb:]
    xm = p + (pp - p) * mu_ref[...]
    r, k_h, v, kk, b, logd, zr = _rwkv_pointwise(xm, w0_ref[...], lw_ref[...], a0_ref[...],
                                                 kk_ref[...], ka_ref[...])
    for i, arr in enumerate((kk, b, jnp.exp(logd), k_h, r, v)):
        tr_ref[0, i] = arr.T
    bz_ref[0, 0] = _bonus(r, k_h, v, rk_ref[...])
    bz_ref[0, 1] = _silu(zr)


def _sample_front(xs_t, h_last, cb0, cb1, wts):
    n_tok, nb, _ = xs_t.shape
    consts = ([_row(wts["norm_g"]), wts["w_conv"], wts["conv_w"]] + _rwkv_consts(wts)[1:])
    tok = pl.BlockSpec((1, nb, D_MODEL), lambda t: (t, 0, 0))
    seq = pl.BlockSpec((nb, D_MODEL), lambda t: (0, 0))
    return pl.pallas_call(
        _sample_front_kernel,
        grid=(n_tok,),
        in_specs=[tok, seq, seq, seq] + [_const_spec(c.shape) for c in consts],
        out_specs=[tok, tok, tok,
                   pl.BlockSpec((1, N_SCAN_OPS, D_MODEL, nb), lambda t: (t, 0, 0, 0)),
                   pl.BlockSpec((1, 2, nb, D_MODEL), lambda t: (t, 0, 0, 0))],
        out_shape=[jax.ShapeDtypeStruct((n_tok, nb, D_MODEL), BF16),
                   jax.ShapeDtypeStruct((n_tok, nb, D_MODEL), F32),
                   jax.ShapeDtypeStruct((n_tok, nb, D_MODEL), F32),
                   jax.ShapeDtypeStruct((n_tok, N_SCAN_OPS, D_MODEL, nb), F32),
                   jax.ShapeDtypeStruct((n_tok, 2, nb, D_MODEL), F32)],
        scratch_shapes=[pltpu.VMEM((nb, D_MODEL), F32) for _ in range(3)],
        compiler_params=pltpu.CompilerParams(
            dimension_semantics=("arbitrary",), vmem_limit_bytes=VMEM_LIMIT),
        name="sample_front",
    )(xs_t, h_last, cb0, cb1, *consts)


def _sample_scan_kernel(s0_ref, tr_ref, s1_ref, y_ref, *, n_tok):
    def block(vb, carry):
        rows = pl.ds(pl.multiple_of(vb * V_BLOCK, V_BLOCK), V_BLOCK)
        ys = [[] for _ in range(n_tok)]
        for v0 in range(0, V_BLOCK, V_SUB):
            s = s0_ref[0, rows][v0:v0 + V_SUB]
            for t in range(n_tok):
                kk, b, d, k, r = [tr_ref[t, i][None] for i in range(5)]
                vv = tr_ref[t, 5, rows, :][v0:v0 + V_SUB, None, :]
                sa = -jnp.sum(s * kk, axis=1, keepdims=True)
                s = s * d + sa * b + vv * k
                ys[t].append(jnp.sum(s * r, axis=1))
            s1_ref[0, pl.ds(pl.multiple_of(vb * V_BLOCK, V_BLOCK) + v0, V_SUB)] = s
        for t in range(n_tok):
            y_ref[t, rows, :] = jnp.concatenate(ys[t], axis=0)
        return carry

    lax.fori_loop(0, HEAD_DIM // V_BLOCK, block, 0)


def _sample_scan(s0_t, tr):
    n_tok = tr.shape[0]
    nb = s0_t.shape[-1]
    sblk = pl.BlockSpec((1, HEAD_DIM, HEAD_DIM, nb), lambda h: (h, 0, 0, 0))
    return pl.pallas_call(
        functools.partial(_sample_scan_kernel, n_tok=n_tok),
        grid=(N_HEADS,),
        in_specs=[sblk, pl.BlockSpec((n_tok, N_SCAN_OPS, HEAD_DIM, nb), lambda h: (0, 0, h, 0))],
        out_specs=[sblk, pl.BlockSpec((n_tok, HEAD_DIM, nb), lambda h: (0, h, 0))],
        out_shape=[jax.ShapeDtypeStruct(s0_t.shape, F32),
                   jax.ShapeDtypeStruct((n_tok, D_MODEL, nb), F32)],
        compiler_params=pltpu.CompilerParams(
            dimension_semantics=("arbitrary",), vmem_limit_bytes=VMEM_LIMIT),
        name="sample_scan",
    )(s0_t, tr)


def _sample_out_kernel(x_ref, yc_ref, yt_ref, bz_ref, lnw_ref, lnb_ref,
                       g_ref, wg_ref, woc_ref, wor_ref, wo_ref, fg_ref, y_ref):
    ys = yt_ref[0].T
    y_r = (_group_norm(ys, lnw_ref[...], lnb_ref[...]) + bz_ref[0, 0]) * bz_ref[0, 1]
    x = x_ref[0]
    y_ref[0] = _merge_out(x, _rms(x, g_ref[...]).astype(BF16), yc_ref[0], y_r.astype(BF16),
                          wg_ref, woc_ref, wor_ref, wo_ref, fg_ref)


def _sample_out(xs_t, yc, y_t, bz, wts):
    n_tok, nb, _ = xs_t.shape
    consts = [_row(wts["ln_w"]), _row(wts["ln_b"]), _row(wts["norm_g"])] + _out_consts(wts)
    tok = pl.BlockSpec((1, nb, D_MODEL), lambda t: (t, 0, 0))
    return pl.pallas_call(
        _sample_out_kernel,
        grid=(n_tok,),
        in_specs=[tok, tok, pl.BlockSpec((1, D_MODEL, nb), lambda t: (t, 0, 0)),
                  pl.BlockSpec((1, 2, nb, D_MODEL), lambda t: (t, 0, 0, 0))]
        + [_const_spec(c.shape) for c in consts],
        out_specs=tok,
        out_shape=jax.ShapeDtypeStruct((n_tok, nb, D_MODEL), F32),
        compiler_params=pltpu.CompilerParams(
            dimension_semantics=("arbitrary",), vmem_limit_bytes=VMEM_LIMIT),
        name="sample_out",
    )(xs_t, yc, y_t, bz, *consts)


def _prep_weights(norm_g, w_in, conv_w, mu_shift, w0, w_up, a0, a_up, k_k, k_a, r_k, ln_w, ln_b,
                  w_out_c, w_out_r, w_o, final_g):
    d = D_MODEL
    c0 = 4 * d
    zeros = jnp.zeros((LORA, d), F32)
    lora_w = jnp.concatenate([jnp.concatenate([w_up, zeros], axis=1),
                              jnp.concatenate([zeros, a_up], axis=1)], axis=0)
    return dict(
        norm_g=norm_g, final_g=final_g, conv_w=conv_w,
        w_conv=w_in[:, :c0].astype(BF16),
        w_rw=w_in[:, c0:c0 + N_RW].astype(BF16),
        mu_rw=mu_shift,
        w_gate=w_in[:, c0 + N_RW:].astype(BF16),
        w0=w0, a0=a0, lora_w=lora_w.astype(BF16), k_k=k_k, k_a=k_a, r_k=r_k.reshape(-1),
        ln_w=ln_w, ln_b=ln_b,
        w_out_c=w_out_c.astype(BF16), w_out_r=w_out_r.astype(BF16), w_o=w_o.astype(BF16))


def _layer_prompt(x, wts):
    bsz, seq, d = x.shape
    yr, s_fin = _rwkv_prompt(x, wts)
    y, u_tail, h_tail = _conv_out_prompt(x, yr, wts)
    return (y, u_tail[:, SUBLANES - 2:, :], h_tail[:, SUBLANES - 1, :], s_fin)


def _layer_sample(x, conv_buf, h_last, s0, wts):
    seq = x.shape[1]
    xs_t = jnp.swapaxes(x, 0, 1)
    yc, u, h, tr, bz = _sample_front(xs_t, h_last, conv_buf[:, 0], conv_buf[:, 1], wts)
    s1_t, y_t = _sample_scan(jnp.transpose(s0, (1, 2, 3, 0)), tr)
    y = _sample_out(xs_t, yc, y_t, bz, wts)
    return (jnp.swapaxes(y, 0, 1), jnp.swapaxes(u[seq - 2:], 0, 1), h[seq - 1],
            jnp.transpose(s1_t, (3, 0, 1, 2)))


def kernel(x_prompt, x_sample, state_conv, state_shift, state_rwkv, norm_g, w_in, conv_w, mu_shift,
           w0, w_up, a0, a_up, k_k, k_a, r_k, ln_w, ln_b, w_out_c, w_out_r, w_o, final_g):
    assert norm_g.shape[0] == 1, "single-layer step"
    wts = _prep_weights(norm_g[0], w_in[0], conv_w[0], mu_shift[0], w0[0], w_up[0], a0[0], a_up[0],
                        k_k[0], k_a[0], r_k[0], ln_w[0], ln_b[0], w_out_c[0], w_out_r[0], w_o[0],
                        final_g)
    y_p, c_p, s_p, r_p = _layer_prompt(x_prompt, wts)
    y_s, c_s, s_s, r_s = _layer_sample(x_sample, state_conv[0], state_shift[0], state_rwkv[0], wts)
    lead = lambda a: a[None]
    return (y_p, y_s, lead(c_p), lead(s_p), lead(r_p), lead(c_s), lead(s_s), lead(r_s))
```

```python
import functools
import math

import jax
import jax.numpy as jnp
from jax import lax
from jax.experimental import pallas as pl
from jax.experimental.pallas import tpu as pltpu

F32 = jnp.float32
BF16 = jnp.bfloat16

D_MODEL = 1024
HEAD_DIM = 64
N_HEADS = D_MODEL // HEAD_DIM
LORA = 64
RMS_EPS = 1e-6
GN_EPS = 64e-5
EXP_M05 = math.exp(-0.5)
LANES = 128
SUBLANES = 8
PAIR = 2 * HEAD_DIM
N_PAIRS = D_MODEL // PAIR
CHUNK = 64
N_RW = 4 * D_MODEL + 2 * LORA
VMEM_LIMIT = 56 * 1024 * 1024


def _dot(a, b):
    return jnp.dot(a.astype(BF16), b.astype(BF16), preferred_element_type=F32)


def _dot_nt(a, b):
    return lax.dot_general(a.astype(BF16), b.astype(BF16), (((1,), (1,)), ((), ())),
                           preferred_element_type=F32)


def _dot_tn(a, b):
    return lax.dot_general(a.astype(BF16), b.astype(BF16), (((0,), (0,)), ((), ())),
                           preferred_element_type=F32)


def _split(x, n):
    parts = []
    for i in range(n):
        piece = x.astype(BF16)
        parts.append(piece)
        if i + 1 < n:
            x = x - piece.astype(F32)
    return parts


def _dot_exact_lhs(m_bf16, x, n=2):
    return sum(jnp.dot(m_bf16, part, preferred_element_type=F32) for part in _split(x, n))


def _rms(x, g):
    return x * lax.rsqrt(jnp.mean(x * x, axis=-1, keepdims=True) + RMS_EPS) * g


def _sigmoid(x):
    return 1.0 / (1.0 + jnp.exp(-x))


def _silu(x):
    return x * _sigmoid(x)


def _iota(shape, dim):
    return lax.broadcasted_iota(jnp.int32, shape, dim)


def _shift_rows(x, n, carry8):
    rolled = pltpu.roll(x, n, 0)
    head = jnp.where(_iota((SUBLANES, x.shape[1]), 0) < n, pltpu.roll(carry8, n, 0),
                     rolled[:SUBLANES])
    return jnp.concatenate([head, rolled[SUBLANES:]], axis=0)


def _seg_sum(x):
    lo = _iota((1, PAIR), 1) < HEAD_DIM
    tiles = []
    for p in range(x.shape[1] // PAIR):
        xp = x[:, p * PAIR:(p + 1) * PAIR]
        s_lo = jnp.sum(jnp.where(lo, xp, 0.0), axis=-1, keepdims=True)
        s_hi = jnp.sum(jnp.where(lo, 0.0, xp), axis=-1, keepdims=True)
        tiles.append(jnp.where(lo, s_lo, s_hi))
    return jnp.concatenate(tiles, axis=1)


def _block_cumsum(x, blk):
    rows, width = x.shape
    r = _iota((rows, rows), 0)
    c = _iota((rows, rows), 1)
    m_low = jnp.where(((r // blk) == (c // blk)) & (c <= r), 1.0, 0.0).astype(BF16)
    cum3 = _dot_exact_lhs(m_low, x).reshape(rows // blk, blk, width)
    return cum3, cum3[:, blk - 1:blk, :]


def _rwkv_pointwise(xm, w0, lora_w, a0, k_k, k_a):
    r = xm[:, 0:D_MODEL]
    k = xm[:, D_MODEL:2 * D_MODEL]
    v = xm[:, 2 * D_MODEL:3 * D_MODEL]
    da = xm[:, 3 * D_MODEL:3 * D_MODEL + 2 * LORA]
    zr = xm[:, 3 * D_MODEL + 2 * LORA:]
    lane = _iota(da.shape, 1)
    lora_in = jnp.where(lane < LORA, jnp.tanh(da), da)
    lo = _dot(lora_in, lora_w)
    w_logit = w0 + lo[:, :D_MODEL]
    logd = -EXP_M05 * _sigmoid(w_logit)
    a = _sigmoid(a0 + lo[:, D_MODEL:])
    kkr = k * k_k
    kk = kkr * jnp.minimum(lax.rsqrt(_seg_sum(kkr * kkr)), 1e12)
    k_h = k * (1.0 + (a - 1.0) * k_a)
    b = kk * a
    return r, k_h, v, kk, b, logd, zr


def _group_norm(ys, ln_w, ln_b):
    inv_n = 1.0 / HEAD_DIM
    yc = ys - _seg_sum(ys) * inv_n
    return yc * lax.rsqrt(_seg_sum(yc * yc) * inv_n + GN_EPS) * ln_w + ln_b


def _bonus(r, k_h, v, r_k):
    return _seg_sum(r * k_h * r_k) * v


def _scan_operands(r, k_h, v, kk, b, logd, cum3, tot3):
    shape = cum3.shape
    flat = lambda x: x.reshape(shape[0] * shape[1], shape[2])
    cum = flat(cum3)
    g_c3 = jnp.exp(tot3)
    enc3 = jnp.exp(-cum3)
    enc = flat(enc3)
    e_c = flat(enc3 * g_c3)
    khat = kk * jnp.exp(cum - logd)
    rhat = r * jnp.exp(cum)
    btil = b * enc
    ktil = k_h * enc
    bchk = -(b * e_c)
    kchk = k_h * e_c
    return khat, rhat, btil, ktil, bchk, kchk, g_c3


def _chunk_transfer(units, m_bd, m_sl, m_l):
    m_bd2 = jnp.concatenate([m_bd, m_bd], axis=1)
    zero = jnp.zeros((), BF16)

    def expand(x):
        x = x.astype(BF16)
        x2 = jnp.concatenate([x, x], axis=0)
        return jnp.where(m_bd if x.shape[1] == PAIR else m_bd2, x2, zero)

    gs = [_dot_nt(jnp.concatenate([kh, rh.astype(BF16)], axis=0),
                  jnp.concatenate([expand(bt), expand(kt)], axis=0))
          for kh, rh, bt, kt, _, _, _ in units]
    ps = [jnp.where(m_sl, -g[:CHUNK, :PAIR], 0.0).astype(BF16) for g in gs]
    m_as = [jnp.concatenate([jnp.where(m_sl, g[:CHUNK, PAIR:], 0.0),
                             jnp.where(m_l, g[CHUNK:, PAIR:], 0.0)], axis=0).astype(BF16)
            for g in gs]
    m_rbs = [jnp.where(m_l, -g[CHUNK:, :PAIR], 0.0).astype(BF16) for g in gs]
    zs = [_dot(m_a, expand(u[6])) for m_a, u in zip(m_as, units)]
    eye = jnp.where(m_l & ~m_sl, 1.0, 0.0)
    ts = [eye + p.astype(F32) for p in ps]
    ps = [_dot(p, expand(p)).astype(BF16) for p in ps]
    for _ in range(4):
        rs = [_dot(p, expand(jnp.concatenate([p, t.astype(BF16)], axis=1)))
              for p, t in zip(ps, ts)]
        ps = [r[:, :PAIR].astype(BF16) for r in rs]
        ts = [t + r[:, PAIR:] for t, r in zip(ts, rs)]
    ts = [t + _dot(p, expand(t)) for p, t in zip(ps, ts)]
    xs = [_dot(t, expand(jnp.concatenate([u[0], z[:CHUNK].astype(BF16)], axis=1))).astype(BF16)
          for t, u, z in zip(ts, units, zs)]
    t1s = [_dot(m_rb, expand(x)) for m_rb, x in zip(m_rbs, xs)]
    p_ts = [_dot_tn(x[:, :PAIR], u[4]) for x, u in zip(xs, units)]
    q_ts = [_dot_tn(jnp.concatenate([x[:, PAIR:], u[6]], axis=0),
                    jnp.concatenate([u[4], u[5]], axis=0)) for x, u in zip(xs, units)]
    out = []
    for u, z, t1, p_t, q_t in zip(units, zs, t1s, p_ts, q_ts):
        y_w = (u[1] + t1[:, :PAIR]).astype(BF16)
        y_c = t1[:, PAIR:] + z[CHUNK:]
        out.append((y_w, y_c, jnp.where(m_bd, p_t, 0.0).astype(BF16), jnp.where(m_bd, q_t, 0.0)))
    return out


def _rwkv_prompt_kernel(x_ref, g_ref, w_ref, mu_ref, w0_ref, lw_ref, a0_ref, kk_ref, ka_ref,
                        rk_ref, lnw_ref, lnb_ref,
                        yr_ref, sfin_ref,
                        carry_ref, st_ref, bz_ref, ys_ref, *, tm, n_tiles, n_steps):
    g = pl.program_id(0)
    n_chunks = tm // CHUNK
    first = jnp.minimum(g, n_steps - 2) % n_tiles == 0
    live = g < n_steps - 1

    @pl.when(g == 0)
    def _():
        for ref in (carry_ref, st_ref, bz_ref, ys_ref):
            ref[...] = jnp.zeros_like(ref)

    h = _rms(x_ref[0], g_ref[...]).astype(BF16)
    p = jnp.dot(h, w_ref[...], preferred_element_type=F32)

    ys = jnp.concatenate([ys_ref[hp] for hp in range(N_PAIRS)], axis=1)
    y_r = (_group_norm(ys, lnw_ref[...], lnb_ref[...]) + bz_ref[0]) * bz_ref[1]
    yr_ref[0] = y_r.astype(BF16)
    acc = y_r.reshape(tm // SUBLANES, SUBLANES, D_MODEL).sum(axis=0)
    acc = sum(acc[:, c * LANES:(c + 1) * LANES] for c in range(D_MODEL // LANES))
    bits = lax.bitcast_convert_type(acc[0:1, :], jnp.uint32)
    bits = lax.shift_right_logical(lax.shift_right_logical(bits, jnp.uint32(16)), jnp.uint32(16))
    after = jnp.concatenate([lax.bitcast_convert_type(bits, F32)] * (N_RW // LANES), axis=1)

    prev = _shift_rows(p, 1, jnp.where(first, 0.0, carry_ref[...]))
    carry_ref[...] = p[tm - SUBLANES:, :]
    xm = p + (prev - p) * (mu_ref[...] + after)
    r, k_h, v, kk, b, logd, zr = _rwkv_pointwise(xm, w0_ref[...], lw_ref[...], a0_ref[...],
                                                 kk_ref[...], ka_ref[...])
    cum3, tot3 = _block_cumsum(logd, CHUNK)
    khat, rhat, btil, ktil, bchk, kchk, g_c3 = _scan_operands(r, k_h, v, kk, b, logd, cum3, tot3)
    g_c = g_c3.reshape(n_chunks, D_MODEL)
    bz_ref[0] = _bonus(r, k_h, v, rk_ref[...])
    bz_ref[1] = _silu(zr)

    rr = _iota((PAIR, PAIR), 0)
    cc = _iota((PAIR, PAIR), 1)
    m_bd = (rr // HEAD_DIM) == (cc // HEAD_DIM)
    t_row = _iota((CHUNK, PAIR), 0)
    s_col = _iota((CHUNK, PAIR), 1) % HEAD_DIM
    m_sl = s_col < t_row
    m_l = s_col <= t_row
    rows = [slice(c * CHUNK, (c + 1) * CHUNK) for c in range(n_chunks)]
    lanes = [slice(hp * PAIR, (hp + 1) * PAIR) for hp in range(N_PAIRS)]
    kh16, bt16, kt16, bc16, kc16, v16 = [a.astype(BF16) for a in (khat, btil, ktil, bchk, kchk, v)]
    units = [(kh16[rw, ln], rhat[rw, ln], bt16[rw, ln], kt16[rw, ln], bc16[rw, ln], kc16[rw, ln],
              v16[rw, ln]) for ln in lanes for rw in rows]
    tr = _chunk_transfer(units, m_bd, m_sl, m_l)
    old = [st_ref[hp] for hp in range(N_PAIRS)]
    states = [jnp.where(first, 0.0, s) for s in old]
    for c in range(n_chunks):
        for hp in range(N_PAIRS):
            y_w, y_c, p_t, q_t = tr[hp * n_chunks + c]
            ys_ref[hp, rows[c], :] = _dot_nt(y_w, states[hp]) + y_c
            states[hp] = states[hp] * g_c[c:c + 1, lanes[hp]] + _dot(states[hp], p_t) + q_t
    for hp in range(N_PAIRS):
        sp = jnp.where(live, states[hp], old[hp])
        st_ref[hp] = sp
        sfin_ref[0, 2 * hp] = sp[:HEAD_DIM, :HEAD_DIM]
        sfin_ref[0, 2 * hp + 1] = sp[HEAD_DIM:, HEAD_DIM:]


def _const_spec(shape):
    nd = len(shape)
    return pl.BlockSpec(shape, lambda *_: (0,) * nd)


def _row(a):
    return a.reshape(1, -1)


def _rwkv_consts(wts):
    return [_row(wts["norm_g"]), wts["w_rw"], _row(wts["mu_rw"]), _row(wts["w0"]), wts["lora_w"],
            _row(wts["a0"]), _row(wts["k_k"]), _row(wts["k_a"]), _row(wts["r_k"])]


def _rwkv_prompt(x, wts, tm=256):
    bsz, seq, _ = x.shape
    consts = _rwkv_consts(wts) + [_row(wts["ln_w"]), _row(wts["ln_b"])]
    const_specs = [_const_spec(c.shape) for c in consts]
    const_specs[1] = pl.BlockSpec(consts[1].shape, lambda *_: (0, 0), pipeline_mode=pl.Buffered(1))
    n_tiles = seq // tm
    n_steps = bsz * n_tiles + 1
    cur = lambda g: jnp.minimum(g, n_steps - 2)
    prv = lambda g: jnp.maximum(g - 1, 0)
    kern = functools.partial(_rwkv_prompt_kernel, tm=tm, n_tiles=n_tiles, n_steps=n_steps)
    return pl.pallas_call(
        kern,
        grid=(n_steps,),
        in_specs=[pl.BlockSpec((1, tm, D_MODEL),
                               lambda g: (cur(g) // n_tiles, cur(g) % n_tiles, 0))] + const_specs,
        out_specs=[pl.BlockSpec((1, tm, D_MODEL), lambda g: (prv(g) // n_tiles, prv(g) % n_tiles, 0)),
                   pl.BlockSpec((1, N_HEADS, HEAD_DIM, HEAD_DIM),
                                lambda g: (cur(g) // n_tiles, 0, 0, 0))],
        out_shape=[jax.ShapeDtypeStruct((bsz, seq, D_MODEL), BF16),
                   jax.ShapeDtypeStruct((bsz, N_HEADS, HEAD_DIM, HEAD_DIM), F32)],
        scratch_shapes=[pltpu.VMEM((SUBLANES, N_RW), F32),
                        pltpu.VMEM((N_PAIRS, PAIR, PAIR), F32),
                        pltpu.VMEM((2, tm, D_MODEL), F32),
                        pltpu.VMEM((N_PAIRS, tm, PAIR), F32)],
        compiler_params=pltpu.CompilerParams(
            dimension_semantics=("arbitrary",), vmem_limit_bytes=VMEM_LIMIT),
        name="rwkv_prompt",
    )(x, *consts)


def _conv_gate(p, u, u1, u2, cw):
    conv = cw[0:1, :] * u2 + cw[1:2, :] * u1 + cw[2:3, :] * u
    return p[:, D_MODEL:2 * D_MODEL] * conv * _silu(p[:, 3 * D_MODEL:])


def _merge_out(x, h, yc, yr, wg_ref, woc_ref, wor_ref, wo_ref, fg_ref):
    gates = jnp.dot(h, wg_ref[...], preferred_element_type=F32)
    pr = jnp.dot(yr, wor_ref[...], preferred_element_type=F32)
    pc = jnp.dot(yc, woc_ref[...], preferred_element_type=F32)
    m = _sigmoid(gates[:, :D_MODEL]) * pc + _sigmoid(gates[:, D_MODEL:]) * pr
    out = jnp.dot(m.astype(BF16), wo_ref[...], preferred_element_type=F32)
    return _rms(x + out, fg_ref[...])


def _out_consts(wts):
    return [wts["w_gate"], wts["w_out_c"], wts["w_out_r"], wts["w_o"], _row(wts["final_g"])]


def _conv_out_kernel(x_ref, yr_ref, g_ref, wc_ref, cw_ref, wg_ref, woc_ref, wor_ref, wo_ref, fg_ref,
                     y_ref, u_ref, h_ref, carry_ref, *, tm):
    @pl.when(pl.program_id(1) == 0)
    def _():
        carry_ref[...] = jnp.zeros_like(carry_ref)

    x = x_ref[0]
    hf = _rms(x, g_ref[...])
    h = hf.astype(BF16)
    p = jnp.dot(h, wc_ref[...], preferred_element_type=F32)
    u = p[:, 2 * D_MODEL:3 * D_MODEL] * p[:, 0:D_MODEL]
    carry = carry_ref[...]
    y_c = _conv_gate(p, u, _shift_rows(u, 1, carry), _shift_rows(u, 2, carry), cw_ref[...])
    carry_ref[...] = u[tm - SUBLANES:, :]
    u_ref[0] = u[tm - SUBLANES:, :]
    h_ref[0] = hf[tm - SUBLANES:, :]
    y_ref[0] = _merge_out(x, h, y_c.astype(BF16), yr_ref[0], wg_ref, woc_ref, wor_ref, wo_ref,
                          fg_ref)


def _conv_out_prompt(x, yr, wts, tm=512):
    bsz, seq, _ = x.shape
    consts = [_row(wts["norm_g"]), wts["w_conv"], wts["conv_w"]] + _out_consts(wts)
    single = lambda c: pl.BlockSpec(c.shape, lambda *_: (0,) * c.ndim, pipeline_mode=pl.Buffered(1))
    tile = pl.BlockSpec((1, tm, D_MODEL), lambda b, s: (b, s, 0))
    tail = pl.BlockSpec((1, SUBLANES, D_MODEL), lambda b, s: (b, 0, 0))
    return pl.pallas_call(
        functools.partial(_conv_out_kernel, tm=tm),
        grid=(bsz, seq // tm),
        in_specs=[tile, tile] + [single(c) for c in consts],
        out_specs=[tile, tail, tail],
        out_shape=[jax.ShapeDtypeStruct((bsz, seq, D_MODEL), F32),
                   jax.ShapeDtypeStruct((bsz, SUBLANES, D_MODEL), F32),
                   jax.ShapeDtypeStruct((bsz, SUBLANES, D_MODEL), F32)],
        scratch_shapes=[pltpu.VMEM((SUBLANES, D_MODEL), F32)],
        compiler_params=pltpu.CompilerParams(
            dimension_semantics=("arbitrary", "arbitrary"), vmem_limit_bytes=VMEM_LIMIT),
        name="conv_out_prompt",
    )(x, yr, *consts)


N_SCAN_OPS = 6
V_BLOCK = 8
V_SUB = 2


def _sample_front_kernel(x_ref, hl_ref, cb0_ref, cb1_ref, g_ref, wc_ref, cw_ref, w_ref, mu_ref,
                         w0_ref, lw_ref, a0_ref, kk_ref, ka_ref, rk_ref,
                         yc_ref, u_ref, h_ref, tr_ref, bz_ref,
                         hp_s, u1_s, u2_s):
    @pl.when(pl.program_id(0) == 0)
    def _():
        hp_s[...] = hl_ref[...]
        u1_s[...] = cb1_ref[...]
        u2_s[...] = cb0_ref[...]

    nb = x_ref.shape[1]
    hf = _rms(x_ref[0], g_ref[...])
    h = hf.astype(BF16)
    pc = jnp.dot(h, wc_ref[...], preferred_element_type=F32)
    u = pc[:, 2 * D_MODEL:3 * D_MODEL] * pc[:, 0:D_MODEL]
    u1 = u1_s[...]
    yc_ref[0] = _conv_gate(pc, u, u1, u2_s[...], cw_ref[...]).astype(BF16)
    u2_s[...] = u1
    u1_s[...] = u
    u_ref[0] = u
    h_ref[0] = hf
    both = jnp.dot(jnp.concatenate([h, hp_s[...].astype(BF16)], axis=0), w_ref[...],
                   preferred_element_type=F32)
    hp_s[...] = hf
    p, pp = both[:nb], both[nb:]
    xm = p + (pp - p) * mu_ref[...]
    r, k_h, v, kk, b, logd, zr = _rwkv_pointwise(xm, w0_ref[...], lw_ref[...], a0_ref[...],
                                                 kk_ref[...], ka_ref[...])
    for i, arr in enumerate((kk, b, jnp.exp(logd), k_h, r, v)):
        tr_ref[0, i] = arr.T
    bz_ref[0, 0] = _bonus(r, k_h, v, rk_ref[...])
    bz_ref[0, 1] = _silu(zr)


def _sample_front(xs_t, h_last, cb0, cb1, wts):
    n_tok, nb, _ = xs_t.shape
    consts = ([_row(wts["norm_g"]), wts["w_conv"], wts["conv_w"]] + _rwkv_consts(wts)[1:])
    tok = pl.BlockSpec((1, nb, D_MODEL), lambda t: (t, 0, 0))
    seq = pl.BlockSpec((nb, D_MODEL), lambda t: (0, 0))
    return pl.pallas_call(
        _sample_front_kernel,
        grid=(n_tok,),
        in_specs=[tok, seq, seq, seq] + [_const_spec(c.shape) for c in consts],
        out_specs=[tok, tok, tok,
                   pl.BlockSpec((1, N_SCAN_OPS, D_MODEL, nb), lambda t: (t, 0, 0, 0)),
                   pl.BlockSpec((1, 2, nb, D_MODEL), lambda t: (t, 0, 0, 0))],
        out_shape=[jax.ShapeDtypeStruct((n_tok, nb, D_MODEL), BF16),
                   jax.ShapeDtypeStruct((n_tok, nb, D_MODEL), F32),
                   jax.ShapeDtypeStruct((n_tok, nb, D_MODEL), F32),
                   jax.ShapeDtypeStruct((n_tok, N_SCAN_OPS, D_MODEL, nb), F32),
                   jax.ShapeDtypeStruct((n_tok, 2, nb, D_MODEL), F32)],
        scratch_shapes=[pltpu.VMEM((nb, D_MODEL), F32) for _ in range(3)],
        compiler_params=pltpu.CompilerParams(
            dimension_semantics=("arbitrary",), vmem_limit_bytes=VMEM_LIMIT),
        name="sample_front",
    )(xs_t, h_last, cb0, cb1, *consts)


def _sample_scan_kernel(s0_ref, tr_ref, s1_ref, y_ref, *, n_tok):
    def block(vb, carry):
        rows = pl.ds(pl.multiple_of(vb * V_BLOCK, V_BLOCK), V_BLOCK)
        ys = [[] for _ in range(n_tok)]
        for v0 in range(0, V_BLOCK, V_SUB):
            s = s0_ref[0, rows][v0:v0 + V_SUB]
            for t in range(n_tok):
                kk, b, d, k, r = [tr_ref[t, i][None] for i in range(5)]
                vv = tr_ref[t, 5, rows, :][v0:v0 + V_SUB, None, :]
                sa = -jnp.sum(s * kk, axis=1, keepdims=True)
                s = s * d + sa * b + vv * k
                ys[t].append(jnp.sum(s * r, axis=1))
            s1_ref[0, pl.ds(pl.multiple_of(vb * V_BLOCK, V_BLOCK) + v0, V_SUB)] = s
        for t in range(n_tok):
            y_ref[t, rows, :] = jnp.concatenate(ys[t], axis=0)
        return carry

    lax.fori_loop(0, HEAD_DIM // V_BLOCK, block, 0)


def _sample_scan(s0_t, tr):
    n_tok = tr.shape[0]
    nb = s0_t.shape[-1]
    sblk = pl.BlockSpec((1, HEAD_DIM, HEAD_DIM, nb), lambda h: (h, 0, 0, 0))
    return pl.pallas_call(
        functools.partial(_sample_scan_kernel, n_tok=n_tok),
        grid=(N_HEADS,),
        in_specs=[sblk, pl.BlockSpec((n_tok, N_SCAN_OPS, HEAD_DIM, nb), lambda h: (0, 0, h, 0))],
        out_specs=[sblk, pl.BlockSpec((n_tok, HEAD_DIM, nb), lambda h: (0, h, 0))],
        out_shape=[jax.ShapeDtypeStruct(s0_t.shape, F32),
                   jax.ShapeDtypeStruct((n_tok, D_MODEL, nb), F32)],
        compiler_params=pltpu.CompilerParams(
            dimension_semantics=("arbitrary",), vmem_limit_bytes=VMEM_LIMIT),
        name="sample_scan",
    )(s0_t, tr)


def _sample_out_kernel(x_ref, yc_ref, yt_ref, bz_ref, lnw_ref, lnb_ref,
                       g_ref, wg_ref, woc_ref, wor_ref, wo_ref, fg_ref, y_ref):
    ys = yt_ref[0].T
    y_r = (_group_norm(ys, lnw_ref[...], lnb_ref[...]) + bz_ref[0, 0]) * bz_ref[0, 1]
    x = x_ref[0]
    y_ref[0] = _merge_out(x, _rms(x, g_ref[...]).astype(BF16), yc_ref[0], y_r.astype(BF16),
                          wg_ref, woc_ref, wor_ref, wo_ref, fg_ref)


def _sample_out(xs_t, yc, y_t, bz, wts):
    n_tok, nb, _ = xs_t.shape
    consts = [_row(wts["ln_w"]), _row(wts["ln_b"]), _row(wts["norm_g"])] + _out_consts(wts)
    tok = pl.BlockSpec((1, nb, D_MODEL), lambda t: (t, 0, 0))
    return pl.pallas_call(
        _sample_out_kernel,
        grid=(n_tok,),
        in_specs=[tok, tok, pl.BlockSpec((1, D_MODEL, nb), lambda t: (t, 0, 0)),
                  pl.BlockSpec((1, 2, nb, D_MODEL), lambda t: (t, 0, 0, 0))]
        + [_const_spec(c.shape) for c in consts],
        out_specs=tok,
        out_shape=jax.ShapeDtypeStruct((n_tok, nb, D_MODEL), F32),
        compiler_params=pltpu.CompilerParams(
            dimension_semantics=("arbitrary",), vmem_limit_bytes=VMEM_LIMIT),
        name="sample_out",
    )(xs_t, yc, y_t, bz, *consts)


def _prep_weights(norm_g, w_in, conv_w, mu_shift, w0, w_up, a0, a_up, k_k, k_a, r_k, ln_w, ln_b,
                  w_out_c, w_out_r, w_o, final_g):
    d = D_MODEL
    c0 = 4 * d
    zeros = jnp.zeros((LORA, d), F32)
    lora_w = jnp.concatenate([jnp.concatenate([w_up, zeros], axis=1),
                              jnp.concatenate([zeros, a_up], axis=1)], axis=0)
    return dict(
        norm_g=norm_g, final_g=final_g, conv_w=conv_w,
        w_conv=w_in[:, :c0].astype(BF16),
        w_rw=w_in[:, c0:c0 + N_RW].astype(BF16),
        mu_rw=mu_shift,
        w_gate=w_in[:, c0 + N_RW:].astype(BF16),
        w0=w0, a0=a0, lora_w=lora_w.astype(BF16), k_k=k_k, k_a=k_a, r_k=r_k.reshape(-1),
        ln_w=ln_w, ln_b=ln_b,
        w_out_c=w_out_c.astype(BF16), w_out_r=w_out_r.astype(BF16), w_o=w_o.astype(BF16))


def _layer_prompt(x, wts):
    yr, s_fin = _rwkv_prompt(x, wts)
    y, u_tail, h_tail = _conv_out_prompt(x, yr, wts)
    return (y, u_tail[:, SUBLANES - 2:, :], h_tail[:, SUBLANES - 1, :], s_fin)


def _layer_sample(x, conv_buf, h_last, s0, wts):
    seq = x.shape[1]
    xs_t = jnp.swapaxes(x, 0, 1)
    yc, u, h, tr, bz = _sample_front(xs_t, h_last, conv_buf[:, 0], conv_buf[:, 1], wts)
    s1_t, y_t = _sample_scan(jnp.transpose(s0, (1, 2, 3, 0)), tr)
    y = _sample_out(xs_t, yc, y_t, bz, wts)
    return (jnp.swapaxes(y, 0, 1), jnp.swapaxes(u[seq - 2:], 0, 1), h[seq - 1],
            jnp.transpose(s1_t, (3, 0, 1, 2)))


def kernel(x_prompt, x_sample, state_conv, state_shift, state_rwkv, norm_g, w_in, conv_w, mu_shift,
           w0, w_up, a0, a_up, k_k, k_a, r_k, ln_w, ln_b, w_out_c, w_out_r, w_o, final_g):
    assert norm_g.shape[0] == 1, "single-layer step"
    wts = _prep_weights(norm_g[0], w_in[0], conv_w[0], mu_shift[0], w0[0], w_up[0], a0[0], a_up[0],
                        k_k[0], k_a[0], r_k[0], ln_w[0], ln_b[0], w_out_c[0], w_out_r[0], w_o[0],
                        final_g)
    y_p, c_p, s_p, r_p = _layer_prompt(x_prompt, wts)
    y_s, c_s, s_s, r_s = _layer_sample(x_sample, state_conv[0], state_shift[0], state_rwkv[0], wts)
    lead = lambda a: a[None]
    return (y_p, y_s, lead(c_p), lead(s_p), lead(r_p), lead(c_s), lead(s_s), lead(r_s))
```

```python
import functools
import math

import jax
import jax.numpy as jnp
from jax import lax
from jax.experimental import pallas as pl
from jax.experimental.pallas import tpu as pltpu

F32 = jnp.float32
BF16 = jnp.bfloat16

D_MODEL = 1024
HEAD_DIM = 64
N_HEADS = D_MODEL // HEAD_DIM
LORA = 64
RMS_EPS = 1e-6
GN_EPS = 64e-5
EXP_M05 = math.exp(-0.5)
LANES = 128
SUBLANES = 8
PAIR = 2 * HEAD_DIM
N_PAIRS = D_MODEL // PAIR
CHUNK = 64
N_RW = 4 * D_MODEL + 2 * LORA
VMEM_LIMIT = 56 * 1024 * 1024


def _dot(a, b):
    return jnp.dot(a.astype(BF16), b.astype(BF16), preferred_element_type=F32)


def _dot_nt(a, b):
    return lax.dot_general(a.astype(BF16), b.astype(BF16), (((1,), (1,)), ((), ())),
                           preferred_element_type=F32)


def _dot_tn(a, b):
    return lax.dot_general(a.astype(BF16), b.astype(BF16), (((0,), (0,)), ((), ())),
                           preferred_element_type=F32)


def _split(x, n):
    parts = []
    for i in range(n):
        piece = x.astype(BF16)
        parts.append(piece)
        if i + 1 < n:
            x = x - piece.astype(F32)
    return parts


def _dot_exact_lhs(m_bf16, x, n=2):
    return sum(jnp.dot(m_bf16, part, preferred_element_type=F32) for part in _split(x, n))


def _rms(x, g):
    return x * lax.rsqrt(jnp.mean(x * x, axis=-1, keepdims=True) + RMS_EPS) * g


def _sigmoid(x):
    return 1.0 / (1.0 + jnp.exp(-x))


def _silu(x):
    return x * _sigmoid(x)


def _iota(shape, dim):
    return lax.broadcasted_iota(jnp.int32, shape, dim)


def _shift_rows(x, n, carry8):
    rolled = pltpu.roll(x, n, 0)
    head = jnp.where(_iota((SUBLANES, x.shape[1]), 0) < n, pltpu.roll(carry8, n, 0),
                     rolled[:SUBLANES])
    return jnp.concatenate([head, rolled[SUBLANES:]], axis=0)


def _seg_sum(x):
    lo = _iota((1, PAIR), 1) < HEAD_DIM
    tiles = []
    for p in range(x.shape[1] // PAIR):
        xp = x[:, p * PAIR:(p + 1) * PAIR]
        s_lo = jnp.sum(jnp.where(lo, xp, 0.0), axis=-1, keepdims=True)
        s_hi = jnp.sum(jnp.where(lo, 0.0, xp), axis=-1, keepdims=True)
        tiles.append(jnp.where(lo, s_lo, s_hi))
    return jnp.concatenate(tiles, axis=1)


def _block_cumsum(x, blk):
    rows, width = x.shape
    r = _iota((rows, rows), 0)
    c = _iota((rows, rows), 1)
    m_low = jnp.where(((r // blk) == (c // blk)) & (c <= r), 1.0, 0.0).astype(BF16)
    cum3 = _dot_exact_lhs(m_low, x).reshape(rows // blk, blk, width)
    return cum3, cum3[:, blk - 1:blk, :]


def _rwkv_pointwise(xm, w0, lora_w, a0, k_k, k_a):
    r = xm[:, 0:D_MODEL]
    k = xm[:, D_MODEL:2 * D_MODEL]
    v = xm[:, 2 * D_MODEL:3 * D_MODEL]
    da = xm[:, 3 * D_MODEL:3 * D_MODEL + 2 * LORA]
    zr = xm[:, 3 * D_MODEL + 2 * LORA:]
    lane = _iota(da.shape, 1)
    lora_in = jnp.where(lane < LORA, jnp.tanh(da), da)
    lo = _dot(lora_in, lora_w)
    w_logit = w0 + lo[:, :D_MODEL]
    logd = -EXP_M05 * _sigmoid(w_logit)
    a = _sigmoid(a0 + lo[:, D_MODEL:])
    kkr = k * k_k
    kk = kkr * jnp.minimum(lax.rsqrt(_seg_sum(kkr * kkr)), 1e12)
    k_h = k * (1.0 + (a - 1.0) * k_a)
    b = kk * a
    return r, k_h, v, kk, b, logd, zr


def _group_norm(ys, ln_w, ln_b):
    inv_n = 1.0 / HEAD_DIM
    yc = ys - _seg_sum(ys) * inv_n
    return yc * lax.rsqrt(_seg_sum(yc * yc) * inv_n + GN_EPS) * ln_w + ln_b


def _bonus(r, k_h, v, r_k):
    return _seg_sum(r * k_h * r_k) * v


def _scan_operands(r, k_h, v, kk, b, logd, cum3, tot3):
    shape = cum3.shape
    flat = lambda x: x.reshape(shape[0] * shape[1], shape[2])
    cum = flat(cum3)
    g_c3 = jnp.exp(tot3)
    enc3 = jnp.exp(-cum3)
    enc = flat(enc3)
    e_c = flat(enc3 * g_c3)
    khat = kk * jnp.exp(cum - logd)
    rhat = r * jnp.exp(cum)
    btil = b * enc
    ktil = k_h * enc
    bchk = -(b * e_c)
    kchk = k_h * e_c
    return khat, rhat, btil, ktil, bchk, kchk, g_c3


def _chunk_transfer(units, m_bd, m_sl, m_l):
    m_bd2 = jnp.concatenate([m_bd, m_bd], axis=1)
    zero = jnp.zeros((), BF16)

    def expand(x):
        x = x.astype(BF16)
        x2 = jnp.concatenate([x, x], axis=0)
        return jnp.where(m_bd if x.shape[1] == PAIR else m_bd2, x2, zero)

    gs = [_dot_nt(jnp.concatenate([kh, rh.astype(BF16)], axis=0),
                  jnp.concatenate([expand(bt), expand(kt)], axis=0))
          for kh, rh, bt, kt, _, _, _ in units]
    ps = [jnp.where(m_sl, -g[:CHUNK, :PAIR], 0.0).astype(BF16) for g in gs]
    m_as = [jnp.concatenate([jnp.where(m_sl, g[:CHUNK, PAIR:], 0.0),
                             jnp.where(m_l, g[CHUNK:, PAIR:], 0.0)], axis=0).astype(BF16)
            for g in gs]
    m_rbs = [jnp.where(m_l, -g[CHUNK:, :PAIR], 0.0).astype(BF16) for g in gs]
    zs = [_dot(m_a, expand(u[6])) for m_a, u in zip(m_as, units)]
    eye = jnp.where(m_l & ~m_sl, 1.0, 0.0)
    ts = [eye + p.astype(F32) for p in ps]
    ps = [_dot(p, expand(p)).astype(BF16) for p in ps]
    for _ in range(4):
        rs = [_dot(p, expand(jnp.concatenate([p, t.astype(BF16)], axis=1)))
              for p, t in zip(ps, ts)]
        ps = [r[:, :PAIR].astype(BF16) for r in rs]
        ts = [t + r[:, PAIR:] for t, r in zip(ts, rs)]
    ts = [t + _dot(p, expand(t)) for p, t in zip(ps, ts)]
    xs = [_dot(t, expand(jnp.concatenate([u[0], z[:CHUNK].astype(BF16)], axis=1))).astype(BF16)
          for t, u, z in zip(ts, units, zs)]
    t1s = [_dot(m_rb, expand(x)) for m_rb, x in zip(m_rbs, xs)]
    p_ts = [_dot_tn(x[:, :PAIR], u[4]) for x, u in zip(xs, units)]
    q_ts = [_dot_tn(jnp.concatenate([x[:, PAIR:], u[6]], axis=0),
                    jnp.concatenate([u[4], u[5]], axis=0)) for x, u in zip(xs, units)]
    out = []
    for u, z, t1, p_t, q_t in zip(units, zs, t1s, p_ts, q_ts):
        y_w = (u[1] + t1[:, :PAIR]).astype(BF16)
        y_c = t1[:, PAIR:] + z[CHUNK:]
        out.append((y_w, y_c, jnp.where(m_bd, p_t, 0.0).astype(BF16), jnp.where(m_bd, q_t, 0.0)))
    return out


def _rwkv_prompt_kernel(x_ref, g_ref, w_ref, mu_ref, w0_ref, lw_ref, a0_ref, kk_ref, ka_ref,
                        rk_ref, lnw_ref, lnb_ref,
                        yr_ref, h_ref, sfin_ref,
                        carry_ref, st_ref, bz_ref, ys_ref, *, tm, n_tiles, n_steps):
    g = pl.program_id(0)
    n_chunks = tm // CHUNK
    first = jnp.minimum(g, n_steps - 2) % n_tiles == 0
    live = g < n_steps - 1

    @pl.when(g == 0)
    def _():
        for ref in (carry_ref, st_ref, bz_ref, ys_ref):
            ref[...] = jnp.zeros_like(ref)

    h = _rms(x_ref[0], g_ref[...]).astype(BF16)
    h_ref[0] = h
    p = jnp.dot(h, w_ref[...], preferred_element_type=F32)

    ys = jnp.concatenate([ys_ref[hp] for hp in range(N_PAIRS)], axis=1)
    y_r = (_group_norm(ys, lnw_ref[...], lnb_ref[...]) + bz_ref[0]) * bz_ref[1]
    yr_ref[0] = y_r.astype(BF16)
    acc = y_r.reshape(tm // SUBLANES, SUBLANES, D_MODEL).sum(axis=0)
    acc = sum(acc[:, c * LANES:(c + 1) * LANES] for c in range(D_MODEL // LANES))
    bits = lax.bitcast_convert_type(acc[0:1, :], jnp.uint32)
    bits = lax.shift_right_logical(lax.shift_right_logical(bits, jnp.uint32(16)), jnp.uint32(16))
    after = jnp.concatenate([lax.bitcast_convert_type(bits, F32)] * (N_RW // LANES), axis=1)

    prev = _shift_rows(p, 1, jnp.where(first, 0.0, carry_ref[...]))
    carry_ref[...] = p[tm - SUBLANES:, :]
    xm = p + (prev - p) * (mu_ref[...] + after)
    r, k_h, v, kk, b, logd, zr = _rwkv_pointwise(xm, w0_ref[...], lw_ref[...], a0_ref[...],
                                                 kk_ref[...], ka_ref[...])
    cum3, tot3 = _block_cumsum(logd, CHUNK)
    khat, rhat, btil, ktil, bchk, kchk, g_c3 = _scan_operands(r, k_h, v, kk, b, logd, cum3, tot3)
    g_c = g_c3.reshape(n_chunks, D_MODEL)
    bz_ref[0] = _bonus(r, k_h, v, rk_ref[...])
    bz_ref[1] = _silu(zr)

    rr = _iota((PAIR, PAIR), 0)
    cc = _iota((PAIR, PAIR), 1)
    m_bd = (rr // HEAD_DIM) == (cc // HEAD_DIM)
    t_row = _iota((CHUNK, PAIR), 0)
    s_col = _iota((CHUNK, PAIR), 1) % HEAD_DIM
    m_sl = s_col < t_row
    m_l = s_col <= t_row
    rows = [slice(c * CHUNK, (c + 1) * CHUNK) for c in range(n_chunks)]
    lanes = [slice(hp * PAIR, (hp + 1) * PAIR) for hp in range(N_PAIRS)]
    kh16, bt16, kt16, bc16, kc16, v16 = [a.astype(BF16) for a in (khat, btil, ktil, bchk, kchk, v)]
    units = [(kh16[rw, ln], rhat[rw, ln], bt16[rw, ln], kt16[rw, ln], bc16[rw, ln], kc16[rw, ln],
              v16[rw, ln]) for ln in lanes for rw in rows]
    tr = _chunk_transfer(units, m_bd, m_sl, m_l)
    old = [st_ref[hp] for hp in range(N_PAIRS)]
    states = [jnp.where(first, 0.0, s) for s in old]
    for c in range(n_chunks):
        for hp in range(N_PAIRS):
            y_w, y_c, p_t, q_t = tr[hp * n_chunks + c]
            ys_ref[hp, rows[c], :] = _dot_nt(y_w, states[hp]) + y_c
            states[hp] = states[hp] * g_c[c:c + 1, lanes[hp]] + _dot(states[hp], p_t) + q_t
    for hp in range(N_PAIRS):
        sp = jnp.where(live, states[hp], old[hp])
        st_ref[hp] = sp
        sfin_ref[0, 2 * hp] = sp[:HEAD_DIM, :HEAD_DIM]
        sfin_ref[0, 2 * hp + 1] = sp[HEAD_DIM:, HEAD_DIM:]


def _const_spec(shape):
    nd = len(shape)
    return pl.BlockSpec(shape, lambda *_: (0,) * nd)


def _row(a):
    return a.reshape(1, -1)


def _rwkv_consts(wts):
    return [_row(wts["norm_g"]), wts["w_rw"], _row(wts["mu_rw"]), _row(wts["w0"]), wts["lora_w"],
            _row(wts["a0"]), _row(wts["k_k"]), _row(wts["k_a"]), _row(wts["r_k"])]


def _rwkv_prompt(x, wts, tm=256):
    bsz, seq, _ = x.shape
    consts = _rwkv_consts(wts) + [_row(wts["ln_w"]), _row(wts["ln_b"])]
    const_specs = [_const_spec(c.shape) for c in consts]
    const_specs[1] = pl.BlockSpec(consts[1].shape, lambda *_: (0, 0), pipeline_mode=pl.Buffered(1))
    n_tiles = seq // tm
    n_steps = bsz * n_tiles + 1
    cur = lambda g: jnp.minimum(g, n_steps - 2)
    prv = lambda g: jnp.maximum(g - 1, 0)
    tile = pl.BlockSpec((1, tm, D_MODEL), lambda g: (cur(g) // n_tiles, cur(g) % n_tiles, 0))
    kern = functools.partial(_rwkv_prompt_kernel, tm=tm, n_tiles=n_tiles, n_steps=n_steps)
    return pl.pallas_call(
        kern,
        grid=(n_steps,),
        in_specs=[tile] + const_specs,
        out_specs=[pl.BlockSpec((1, tm, D_MODEL), lambda g: (prv(g) // n_tiles, prv(g) % n_tiles, 0)),
                   tile,
                   pl.BlockSpec((1, N_HEADS, HEAD_DIM, HEAD_DIM),
                                lambda g: (cur(g) // n_tiles, 0, 0, 0))],
        out_shape=[jax.ShapeDtypeStruct((bsz, seq, D_MODEL), BF16),
                   jax.ShapeDtypeStruct((bsz, seq, D_MODEL), BF16),
                   jax.ShapeDtypeStruct((bsz, N_HEADS, HEAD_DIM, HEAD_DIM), F32)],
        scratch_shapes=[pltpu.VMEM((SUBLANES, N_RW), F32),
                        pltpu.VMEM((N_PAIRS, PAIR, PAIR), F32),
                        pltpu.VMEM((2, tm, D_MODEL), F32),
                        pltpu.VMEM((N_PAIRS, tm, PAIR), F32)],
        compiler_params=pltpu.CompilerParams(
            dimension_semantics=("arbitrary",), vmem_limit_bytes=VMEM_LIMIT),
        name="rwkv_prompt",
    )(x, *consts)


def _conv_gate(p, u, u1, u2, cw):
    conv = cw[0:1, :] * u2 + cw[1:2, :] * u1 + cw[2:3, :] * u
    return p[:, D_MODEL:2 * D_MODEL] * conv * _silu(p[:, 3 * D_MODEL:])


def _merge_out(x, h, yc, yr, wg_ref, woc_ref, wor_ref, wo_ref, fg_ref):
    gates = jnp.dot(h, wg_ref[...], preferred_element_type=F32)
    pr = jnp.dot(yr, wor_ref[...], preferred_element_type=F32)
    pc = jnp.dot(yc, woc_ref[...], preferred_element_type=F32)
    m = _sigmoid(gates[:, :D_MODEL]) * pc + _sigmoid(gates[:, D_MODEL:]) * pr
    out = jnp.dot(m.astype(BF16), wo_ref[...], preferred_element_type=F32)
    return _rms(x + out, fg_ref[...])


def _out_consts(wts):
    return [wts["w_gate"], wts["w_out_c"], wts["w_out_r"], wts["w_o"], _row(wts["final_g"])]


def _conv_out_kernel(x_ref, hb_ref, yr_ref, g_ref, wc_ref, cw_ref, wg_ref, woc_ref, wor_ref, wo_ref,
                     fg_ref, y_ref, u_ref, h_ref, carry_ref, *, tm):
    @pl.when(pl.program_id(1) == 0)
    def _():
        carry_ref[...] = jnp.zeros_like(carry_ref)

    x = x_ref[0]
    h = hb_ref[0]
    p = jnp.dot(h, wc_ref[...], preferred_element_type=F32)
    u = p[:, 2 * D_MODEL:3 * D_MODEL] * p[:, 0:D_MODEL]
    carry = carry_ref[...]
    y_c = _conv_gate(p, u, _shift_rows(u, 1, carry), _shift_rows(u, 2, carry), cw_ref[...])
    carry_ref[...] = u[tm - SUBLANES:, :]
    u_ref[0] = u[tm - SUBLANES:, :]
    h_ref[0] = _rms(x[tm - SUBLANES:, :], g_ref[...])
    y_ref[0] = _merge_out(x, h, y_c.astype(BF16), yr_ref[0], wg_ref, woc_ref, wor_ref, wo_ref,
                          fg_ref)


def _conv_out_prompt(x, hb, yr, wts, tm=512):
    bsz, seq, _ = x.shape
    consts = [_row(wts["norm_g"]), wts["w_conv"], wts["conv_w"]] + _out_consts(wts)
    single = lambda c: pl.BlockSpec(c.shape, lambda *_: (0,) * c.ndim, pipeline_mode=pl.Buffered(1))
    tile = pl.BlockSpec((1, tm, D_MODEL), lambda b, s: (b, s, 0))
    tail = pl.BlockSpec((1, SUBLANES, D_MODEL), lambda b, s: (b, 0, 0))
    return pl.pallas_call(
        functools.partial(_conv_out_kernel, tm=tm),
        grid=(bsz, seq // tm),
        in_specs=[tile, tile, tile] + [single(c) for c in consts],
        out_specs=[tile, tail, tail],
        out_shape=[jax.ShapeDtypeStruct((bsz, seq, D_MODEL), F32),
                   jax.ShapeDtypeStruct((bsz, SUBLANES, D_MODEL), F32),
                   jax.ShapeDtypeStruct((bsz, SUBLANES, D_MODEL), F32)],
        scratch_shapes=[pltpu.VMEM((SUBLANES, D_MODEL), F32)],
        compiler_params=pltpu.CompilerParams(
            dimension_semantics=("arbitrary", "arbitrary"), vmem_limit_bytes=VMEM_LIMIT),
        name="conv_out_prompt",
    )(x, hb, yr, *consts)


N_SCAN_OPS = 6
V_BLOCK = 8
V_SUB = 2


def _sample_front_kernel(x_ref, hl_ref, cb0_ref, cb1_ref, g_ref, wc_ref, cw_ref, w_ref, mu_ref,
                         w0_ref, lw_ref, a0_ref, kk_ref, ka_ref, rk_ref,
                         yc_ref, u_ref, h_ref, tr_ref, bz_ref,
                         hp_s, u1_s, u2_s):
    @pl.when(pl.program_id(0) == 0)
    def _():
        hp_s[...] = hl_ref[...]
        u1_s[...] = cb1_ref[...]
        u2_s[...] = cb0_ref[...]

    nb = x_ref.shape[1]
    hf = _rms(x_ref[0], g_ref[...])
    h = hf.astype(BF16)
    pc = jnp.dot(h, wc_ref[...], preferred_element_type=F32)
    u = pc[:, 2 * D_MODEL:3 * D_MODEL] * pc[:, 0:D_MODEL]
    u1 = u1_s[...]
    yc_ref[0] = _conv_gate(pc, u, u1, u2_s[...], cw_ref[...]).astype(BF16)
    u2_s[...] = u1
    u1_s[...] = u
    u_ref[0] = u
    h_ref[0] = hf
    both = jnp.dot(jnp.concatenate([h, hp_s[...].astype(BF16)], axis=0), w_ref[...],
                   preferred_element_type=F32)
    hp_s[...] = hf
    p, pp = both[:nb], both[nb:]
    xm = p + (pp - p) * mu_ref[...]
    r, k_h, v, kk, b, logd, zr = _rwkv_pointwise(xm, w0_ref[...], lw_ref[...], a0_ref[...],
                                                 kk_ref[...], ka_ref[...])
    for i, arr in enumerate((kk, b, jnp.exp(logd), k_h, r, v)):
        tr_ref[0, i] = arr.T
    bz_ref[0, 0] = _bonus(r, k_h, v, rk_ref[...])
    bz_ref[0, 1] = _silu(zr)


def _sample_front(xs_t, h_last, cb0, cb1, wts):
    n_tok, nb, _ = xs_t.shape
    consts = ([_row(wts["norm_g"]), wts["w_conv"], wts["conv_w"]] + _rwkv_consts(wts)[1:])
    tok = pl.BlockSpec((1, nb, D_MODEL), lambda t: (t, 0, 0))
    seq = pl.BlockSpec((nb, D_MODEL), lambda t: (0, 0))
    return pl.pallas_call(
        _sample_front_kernel,
        grid=(n_tok,),
        in_specs=[tok, seq, seq, seq] + [_const_spec(c.shape) for c in consts],
        out_specs=[tok, tok, tok,
                   pl.BlockSpec((1, N_SCAN_OPS, D_MODEL, nb), lambda t: (t, 0, 0, 0)),
                   pl.BlockSpec((1, 2, nb, D_MODEL), lambda t: (t, 0, 0, 0))],
        out_shape=[jax.ShapeDtypeStruct((n_tok, nb, D_MODEL), BF16),
                   jax.ShapeDtypeStruct((n_tok, nb, D_MODEL), F32),
                   jax.ShapeDtypeStruct((n_tok, nb, D_MODEL), F32),
                   jax.ShapeDtypeStruct((n_tok, N_SCAN_OPS, D_MODEL, nb), F32),
                   jax.ShapeDtypeStruct((n_tok, 2, nb, D_MODEL), F32)],
        scratch_shapes=[pltpu.VMEM((nb, D_MODEL), F32) for _ in range(3)],
        compiler_params=pltpu.CompilerParams(
            dimension_semantics=("arbitrary",), vmem_limit_bytes=VMEM_LIMIT),
        name="sample_front",
    )(xs_t, h_last, cb0, cb1, *consts)


def _sample_scan_kernel(s0_ref, tr_ref, s1_ref, y_ref, *, n_tok):
    def block(vb, carry):
        rows = pl.ds(pl.multiple_of(vb * V_BLOCK, V_BLOCK), V_BLOCK)
        ys = [[] for _ in range(n_tok)]
        for v0 in range(0, V_BLOCK, V_SUB):
            s = s0_ref[0, rows][v0:v0 + V_SUB]
            for t in range(n_tok):
                kk, b, d, k, r = [tr_ref[t, i][None] for i in range(5)]
                vv = tr_ref[t, 5, rows, :][v0:v0 + V_SUB, None, :]
                sa = -jnp.sum(s * kk, axis=1, keepdims=True)
                s = s * d + sa * b + vv * k
                ys[t].append(jnp.sum(s * r, axis=1))
            s1_ref[0, pl.ds(pl.multiple_of(vb * V_BLOCK, V_BLOCK) + v0, V_SUB)] = s
        for t in range(n_tok):
            y_ref[t, rows, :] = jnp.concatenate(ys[t], axis=0)
        return carry

    lax.fori_loop(0, HEAD_DIM // V_BLOCK, block, 0)


def _sample_scan(s0_t, tr):
    n_tok = tr.shape[0]
    nb = s0_t.shape[-1]
    sblk = pl.BlockSpec((1, HEAD_DIM, HEAD_DIM, nb), lambda h: (h, 0, 0, 0))
    return pl.pallas_call(
        functools.partial(_sample_scan_kernel, n_tok=n_tok),
        grid=(N_HEADS,),
        in_specs=[sblk, pl.BlockSpec((n_tok, N_SCAN_OPS, HEAD_DIM, nb), lambda h: (0, 0, h, 0))],
        out_specs=[sblk, pl.BlockSpec((n_tok, HEAD_DIM, nb), lambda h: (0, h, 0))],
        out_shape=[jax.ShapeDtypeStruct(s0_t.shape, F32),
                   jax.ShapeDtypeStruct((n_tok, D_MODEL, nb), F32)],
        compiler_params=pltpu.CompilerParams(
            dimension_semantics=("arbitrary",), vmem_limit_bytes=VMEM_LIMIT),
        name="sample_scan",
    )(s0_t, tr)


def _sample_out_kernel(x_ref, yc_ref, yt_ref, bz_ref, lnw_ref, lnb_ref,
                       g_ref, wg_ref, woc_ref, wor_ref, wo_ref, fg_ref, y_ref):
    ys = yt_ref[0].T
    y_r = (_group_norm(ys, lnw_ref[...], lnb_ref[...]) + bz_ref[0, 0]) * bz_ref[0, 1]
    x = x_ref[0]
    y_ref[0] = _merge_out(x, _rms(x, g_ref[...]).astype(BF16), yc_ref[0], y_r.astype(BF16),
                          wg_ref, woc_ref, wor_ref, wo_ref, fg_ref)


def _sample_out(xs_t, yc, y_t, bz, wts):
    n_tok, nb, _ = xs_t.shape
    consts = [_row(wts["ln_w"]), _row(wts["ln_b"]), _row(wts["norm_g"])] + _out_consts(wts)
    tok = pl.BlockSpec((1, nb, D_MODEL), lambda t: (t, 0, 0))
    return pl.pallas_call(
        _sample_out_kernel,
        grid=(n_tok,),
        in_specs=[tok, tok, pl.BlockSpec((1, D_MODEL, nb), lambda t: (t, 0, 0)),
                  pl.BlockSpec((1, 2, nb, D_MODEL), lambda t: (t, 0, 0, 0))]
        + [_const_spec(c.shape) for c in consts],
        out_specs=tok,
        out_shape=jax.ShapeDtypeStruct((n_tok, nb, D_MODEL), F32),
        compiler_params=pltpu.CompilerParams(
            dimension_semantics=("arbitrary",), vmem_limit_bytes=VMEM_LIMIT),
        name="sample_out",
    )(xs_t, yc, y_t, bz, *consts)


def _prep_weights(norm_g, w_in, conv_w, mu_shift, w0, w_up, a0, a_up, k_k, k_a, r_k, ln_w, ln_b,
                  w_out_c, w_out_r, w_o, final_g):
    d = D_MODEL
    c0 = 4 * d
    zeros = jnp.zeros((LORA, d), F32)
    lora_w = jnp.concatenate([jnp.concatenate([w_up, zeros], axis=1),
                              jnp.concatenate([zeros, a_up], axis=1)], axis=0)
    return dict(
        norm_g=norm_g, final_g=final_g, conv_w=conv_w,
        w_conv=w_in[:, :c0].astype(BF16),
        w_rw=w_in[:, c0:c0 + N_RW].astype(BF16),
        mu_rw=mu_shift,
        w_gate=w_in[:, c0 + N_RW:].astype(BF16),
        w0=w0, a0=a0, lora_w=lora_w.astype(BF16), k_k=k_k, k_a=k_a, r_k=r_k.reshape(-1),
        ln_w=ln_w, ln_b=ln_b,
        w_out_c=w_out_c.astype(BF16), w_out_r=w_out_r.astype(BF16), w_o=w_o.astype(BF16))


def _layer_prompt(x, wts):
    yr, hb, s_fin = _rwkv_prompt(x, wts)
    y, u_tail, h_tail = _conv_out_prompt(x, hb, yr, wts)
    return (y, u_tail[:, SUBLANES - 2:, :], h_tail[:, SUBLANES - 1, :], s_fin)


def _layer_sample(x, conv_buf, h_last, s0, wts):
    seq = x.shape[1]
    xs_t = jnp.swapaxes(x, 0, 1)
    yc, u, h, tr, bz = _sample_front(xs_t, h_last, conv_buf[:, 0], conv_buf[:, 1], wts)
    s1_t, y_t = _sample_scan(jnp.transpose(s0, (1, 2, 3, 0)), tr)
    y = _sample_out(xs_t, yc, y_t, bz, wts)
    return (jnp.swapaxes(y, 0, 1), jnp.swapaxes(u[seq - 2:], 0, 1), h[seq - 1],
            jnp.transpose(s1_t, (3, 0, 1, 2)))


def kernel(x_prompt, x_sample, state_conv, state_shift, state_rwkv, norm_g, w_in, conv_w, mu_shift,
           w0, w_up, a0, a_up, k_k, k_a, r_k, ln_w, ln_b, w_out_c, w_out_r, w_o, final_g):
    assert norm_g.shape[0] == 1, "single-layer step"
    wts = _prep_weights(norm_g[0], w_in[0], conv_w[0], mu_shift[0], w0[0], w_up[0], a0[0], a_up[0],
                        k_k[0], k_a[0], r_k[0], ln_w[0], ln_b[0], w_out_c[0], w_out_r[0], w_o[0],
                        final_g)
    y_p, c_p, s_p, r_p = _layer_prompt(x_prompt, wts)
    y_s, c_s, s_s, r_s = _layer_sample(x_sample, state_conv[0], state_shift[0], state_rwkv[0], wts)
    lead = lambda a: a[None]
    return (y_p, y_s, lead(c_p), lead(s_p), lead(r_p), lead(c_s), lead(s_s), lead(r_s))
```

```python
import functools
import math

import jax
import jax.numpy as jnp
from jax import lax
from jax.experimental import pallas as pl
from jax.experimental.pallas import tpu as pltpu

F32 = jnp.float32
BF16 = jnp.bfloat16

D_MODEL = 1024
HEAD_DIM = 64
N_HEADS = D_MODEL // HEAD_DIM
LORA = 64
RMS_EPS = 1e-6
GN_EPS = 64e-5
EXP_M05 = math.exp(-0.5)
LANES = 128
SUBLANES = 8
PAIR = 2 * HEAD_DIM
N_PAIRS = D_MODEL // PAIR
CHUNK = 64
N_RW = 4 * D_MODEL + 2 * LORA
VMEM_LIMIT = 56 * 1024 * 1024


def _dot(a, b):
    return jnp.dot(a.astype(BF16), b.astype(BF16), preferred_element_type=F32)


def _dot_nt(a, b):
    return lax.dot_general(a.astype(BF16), b.astype(BF16), (((1,), (1,)), ((), ())),
                           preferred_element_type=F32)


def _dot_tn(a, b):
    return lax.dot_general(a.astype(BF16), b.astype(BF16), (((0,), (0,)), ((), ())),
                           preferred_element_type=F32)


def _split(x, n):
    parts = []
    for i in range(n):
        piece = x.astype(BF16)
        parts.append(piece)
        if i + 1 < n:
            x = x - piece.astype(F32)
    return parts


def _dot_exact_lhs(m_bf16, x, n=2):
    return sum(jnp.dot(m_bf16, part, preferred_element_type=F32) for part in _split(x, n))


def _rms(x, g):
    return x * lax.rsqrt(jnp.mean(x * x, axis=-1, keepdims=True) + RMS_EPS) * g


def _sigmoid(x):
    return 1.0 / (1.0 + jnp.exp(-x))


def _silu(x):
    return x * _sigmoid(x)


def _iota(shape, dim):
    return lax.broadcasted_iota(jnp.int32, shape, dim)


def _shift_rows(x, n, carry8):
    rolled = pltpu.roll(x, n, 0)
    head = jnp.where(_iota((SUBLANES, x.shape[1]), 0) < n, pltpu.roll(carry8, n, 0),
                     rolled[:SUBLANES])
    return jnp.concatenate([head, rolled[SUBLANES:]], axis=0)


def _seg_sum(x):
    lo = _iota((1, PAIR), 1) < HEAD_DIM
    tiles = []
    for p in range(x.shape[1] // PAIR):
        xp = x[:, p * PAIR:(p + 1) * PAIR]
        s_lo = jnp.sum(jnp.where(lo, xp, 0.0), axis=-1, keepdims=True)
        s_hi = jnp.sum(jnp.where(lo, 0.0, xp), axis=-1, keepdims=True)
        tiles.append(jnp.where(lo, s_lo, s_hi))
    return jnp.concatenate(tiles, axis=1)


def _block_cumsum(x, blk):
    rows, width = x.shape
    r = _iota((rows, rows), 0)
    c = _iota((rows, rows), 1)
    m_low = jnp.where(((r // blk) == (c // blk)) & (c <= r), 1.0, 0.0).astype(BF16)
    cum3 = _dot_exact_lhs(m_low, x).reshape(rows // blk, blk, width)
    return cum3, cum3[:, blk - 1:blk, :]


def _rwkv_pointwise(xm, w0, lora_w, a0, k_k, k_a):
    r = xm[:, 0:D_MODEL]
    k = xm[:, D_MODEL:2 * D_MODEL]
    v = xm[:, 2 * D_MODEL:3 * D_MODEL]
    da = xm[:, 3 * D_MODEL:3 * D_MODEL + 2 * LORA]
    zr = xm[:, 3 * D_MODEL + 2 * LORA:]
    lane = _iota(da.shape, 1)
    lora_in = jnp.where(lane < LORA, jnp.tanh(da), da)
    lo = _dot(lora_in, lora_w)
    w_logit = w0 + lo[:, :D_MODEL]
    logd = -EXP_M05 * _sigmoid(w_logit)
    a = _sigmoid(a0 + lo[:, D_MODEL:])
    kkr = k * k_k
    kk = kkr * jnp.minimum(lax.rsqrt(_seg_sum(kkr * kkr)), 1e12)
    k_h = k * (1.0 + (a - 1.0) * k_a)
    b = kk * a
    return r, k_h, v, kk, b, logd, zr


def _group_norm(ys, ln_w, ln_b):
    inv_n = 1.0 / HEAD_DIM
    yc = ys - _seg_sum(ys) * inv_n
    return yc * lax.rsqrt(_seg_sum(yc * yc) * inv_n + GN_EPS) * ln_w + ln_b


def _bonus(r, k_h, v, r_k):
    return _seg_sum(r * k_h * r_k) * v


def _scan_operands(r, k_h, v, kk, b, logd, cum3, tot3):
    shape = cum3.shape
    flat = lambda x: x.reshape(shape[0] * shape[1], shape[2])
    cum = flat(cum3)
    g_c3 = jnp.exp(tot3)
    enc3 = jnp.exp(-cum3)
    enc = flat(enc3)
    e_c = flat(enc3 * g_c3)
    khat = kk * jnp.exp(cum - logd)
    rhat = r * jnp.exp(cum)
    btil = b * enc
    ktil = k_h * enc
    bchk = -(b * e_c)
    kchk = k_h * e_c
    return khat, rhat, btil, ktil, bchk, kchk, g_c3


def _chunk_transfer(units, m_bd, m_sl, m_l):
    m_bd2 = jnp.concatenate([m_bd, m_bd], axis=1)
    zero = jnp.zeros((), BF16)

    def expand(x):
        x = x.astype(BF16)
        x2 = jnp.concatenate([x, x], axis=0)
        return jnp.where(m_bd if x.shape[1] == PAIR else m_bd2, x2, zero)

    gs = [_dot_nt(jnp.concatenate([kh, rh.astype(BF16)], axis=0),
                  jnp.concatenate([expand(bt), expand(kt)], axis=0))
          for kh, rh, bt, kt, _, _, _ in units]
    ps = [jnp.where(m_sl, -g[:CHUNK, :PAIR], 0.0).astype(BF16) for g in gs]
    m_as = [jnp.concatenate([jnp.where(m_sl, g[:CHUNK, PAIR:], 0.0),
                             jnp.where(m_l, g[CHUNK:, PAIR:], 0.0)], axis=0).astype(BF16)
            for g in gs]
    m_rbs = [jnp.where(m_l, -g[CHUNK:, :PAIR], 0.0).astype(BF16) for g in gs]
    zs = [_dot(m_a, expand(u[6])) for m_a, u in zip(m_as, units)]
    eye = jnp.where(m_l & ~m_sl, 1.0, 0.0)
    ts = [eye + p.astype(F32) for p in ps]
    ps = [_dot(p, expand(p)).astype(BF16) for p in ps]
    for _ in range(4):
        rs = [_dot(p, expand(jnp.concatenate([p, t.astype(BF16)], axis=1)))
              for p, t in zip(ps, ts)]
        ps = [r[:, :PAIR].astype(BF16) for r in rs]
        ts = [t + r[:, PAIR:] for t, r in zip(ts, rs)]
    ts = [t + _dot(p, expand(t)) for p, t in zip(ps, ts)]
    xs = [_dot(t, expand(jnp.concatenate([u[0], z[:CHUNK].astype(BF16)], axis=1))).astype(BF16)
          for t, u, z in zip(ts, units, zs)]
    t1s = [_dot(m_rb, expand(x)) for m_rb, x in zip(m_rbs, xs)]
    p_ts = [_dot_tn(x[:, :PAIR], u[4]) for x, u in zip(xs, units)]
    q_ts = [_dot_tn(jnp.concatenate([x[:, PAIR:], u[6]], axis=0),
                    jnp.concatenate([u[4], u[5]], axis=0)) for x, u in zip(xs, units)]
    out = []
    for u, z, t1, p_t, q_t in zip(units, zs, t1s, p_ts, q_ts):
        y_w = (u[1] + t1[:, :PAIR]).astype(BF16)
        y_c = t1[:, PAIR:] + z[CHUNK:]
        out.append((y_w, y_c, jnp.where(m_bd, p_t, 0.0).astype(BF16), jnp.where(m_bd, q_t, 0.0)))
    return out


def _rwkv_prompt_kernel(x_ref, g_ref, w_ref, mu_ref, w0_ref, lw_ref, a0_ref, kk_ref, ka_ref,
                        rk_ref, lnw_ref, lnb_ref,
                        yr_ref, h_ref, sfin_ref,
                        carry_ref, st_ref, bz_ref, ys_ref, *, tm, n_tiles, n_steps):
    g = pl.program_id(0)
    n_chunks = tm // CHUNK
    first = g % n_tiles == 0

    @pl.when(g == 0)
    def _():
        for ref in (carry_ref, st_ref, bz_ref, ys_ref):
            ref[...] = jnp.zeros_like(ref)

    def finish_previous():
        ys = jnp.concatenate([ys_ref[hp] for hp in range(N_PAIRS)], axis=1)
        y_r = (_group_norm(ys, lnw_ref[...], lnb_ref[...]) + bz_ref[0]) * bz_ref[1]
        yr_ref[0] = y_r.astype(BF16)
        return y_r

    @pl.when(g == n_steps - 1)
    def _():
        finish_previous()

    pl.when(g < n_steps - 1)(functools.partial(
        _rwkv_prompt_step, x_ref, g_ref, w_ref, mu_ref, w0_ref, lw_ref, a0_ref, kk_ref, ka_ref, rk_ref,
        h_ref, sfin_ref, carry_ref, st_ref, bz_ref, ys_ref, finish_previous, first, tm, n_chunks))


def _rwkv_prompt_step(x_ref, g_ref, w_ref, mu_ref, w0_ref, lw_ref, a0_ref, kk_ref, ka_ref, rk_ref,
                      h_ref, sfin_ref, carry_ref, st_ref, bz_ref, ys_ref, finish_previous, first,
                      tm, n_chunks):
    h = _rms(x_ref[0], g_ref[...]).astype(BF16)
    h_ref[0] = h
    p = jnp.dot(h, w_ref[...], preferred_element_type=F32)

    y_r = finish_previous()
    acc = y_r.reshape(tm // SUBLANES, SUBLANES, D_MODEL).sum(axis=0)
    acc = sum(acc[:, c * LANES:(c + 1) * LANES] for c in range(D_MODEL // LANES))
    bits = lax.bitcast_convert_type(acc[0:1, :], jnp.uint32)
    bits = lax.shift_right_logical(lax.shift_right_logical(bits, jnp.uint32(16)), jnp.uint32(16))
    after = jnp.concatenate([lax.bitcast_convert_type(bits, F32)] * (N_RW // LANES), axis=1)

    prev = _shift_rows(p, 1, jnp.where(first, 0.0, carry_ref[...]))
    carry_ref[...] = p[tm - SUBLANES:, :]
    xm = p + (prev - p) * (mu_ref[...] + after)
    r, k_h, v, kk, b, logd, zr = _rwkv_pointwise(xm, w0_ref[...], lw_ref[...], a0_ref[...],
                                                 kk_ref[...], ka_ref[...])
    cum3, tot3 = _block_cumsum(logd, CHUNK)
    khat, rhat, btil, ktil, bchk, kchk, g_c3 = _scan_operands(r, k_h, v, kk, b, logd, cum3, tot3)
    g_c = g_c3.reshape(n_chunks, D_MODEL)
    bz_ref[0] = _bonus(r, k_h, v, rk_ref[...])
    bz_ref[1] = _silu(zr)

    rr = _iota((PAIR, PAIR), 0)
    cc = _iota((PAIR, PAIR), 1)
    m_bd = (rr // HEAD_DIM) == (cc // HEAD_DIM)
    t_row = _iota((CHUNK, PAIR), 0)
    s_col = _iota((CHUNK, PAIR), 1) % HEAD_DIM
    m_sl = s_col < t_row
    m_l = s_col <= t_row
    rows = [slice(c * CHUNK, (c + 1) * CHUNK) for c in range(n_chunks)]
    lanes = [slice(hp * PAIR, (hp + 1) * PAIR) for hp in range(N_PAIRS)]
    kh16, bt16, kt16, bc16, kc16, v16 = [a.astype(BF16) for a in (khat, btil, ktil, bchk, kchk, v)]
    units = [(kh16[rw, ln], rhat[rw, ln], bt16[rw, ln], kt16[rw, ln], bc16[rw, ln], kc16[rw, ln],
              v16[rw, ln]) for ln in lanes for rw in rows]
    tr = _chunk_transfer(units, m_bd, m_sl, m_l)
    states = [jnp.where(first, 0.0, st_ref[hp]) for hp in range(N_PAIRS)]
    for c in range(n_chunks):
        for hp in range(N_PAIRS):
            y_w, y_c, p_t, q_t = tr[hp * n_chunks + c]
            ys_ref[hp, rows[c], :] = _dot_nt(y_w, states[hp]) + y_c
            states[hp] = states[hp] * g_c[c:c + 1, lanes[hp]] + _dot(states[hp], p_t) + q_t
    for hp in range(N_PAIRS):
        sp = states[hp]
        st_ref[hp] = sp
        sfin_ref[0, 2 * hp] = sp[:HEAD_DIM, :HEAD_DIM]
        sfin_ref[0, 2 * hp + 1] = sp[HEAD_DIM:, HEAD_DIM:]


def _const_spec(shape):
    nd = len(shape)
    return pl.BlockSpec(shape, lambda *_: (0,) * nd)


def _row(a):
    return a.reshape(1, -1)


def _rwkv_consts(wts):
    return [_row(wts["norm_g"]), wts["w_rw"], _row(wts["mu_rw"]), _row(wts["w0"]), wts["lora_w"],
            _row(wts["a0"]), _row(wts["k_k"]), _row(wts["k_a"]), _row(wts["r_k"])]


def _rwkv_prompt(x, wts, tm=256):
    bsz, seq, _ = x.shape
    consts = _rwkv_consts(wts) + [_row(wts["ln_w"]), _row(wts["ln_b"])]
    const_specs = [_const_spec(c.shape) for c in consts]
    const_specs[1] = pl.BlockSpec(consts[1].shape, lambda *_: (0, 0), pipeline_mode=pl.Buffered(1))
    n_tiles = seq // tm
    n_steps = bsz * n_tiles + 1
    cur = lambda g: jnp.minimum(g, n_steps - 2)
    prv = lambda g: jnp.maximum(g - 1, 0)
    tile = pl.BlockSpec((1, tm, D_MODEL), lambda g: (cur(g) // n_tiles, cur(g) % n_tiles, 0))
    kern = functools.partial(_rwkv_prompt_kernel, tm=tm, n_tiles=n_tiles, n_steps=n_steps)
    return pl.pallas_call(
        kern,
        grid=(n_steps,),
        in_specs=[tile] + const_specs,
        out_specs=[pl.BlockSpec((1, tm, D_MODEL), lambda g: (prv(g) // n_tiles, prv(g) % n_tiles, 0)),
                   tile,
                   pl.BlockSpec((1, N_HEADS, HEAD_DIM, HEAD_DIM),
                                lambda g: (cur(g) // n_tiles, 0, 0, 0))],
        out_shape=[jax.ShapeDtypeStruct((bsz, seq, D_MODEL), BF16),
                   jax.ShapeDtypeStruct((bsz, seq, D_MODEL), BF16),
                   jax.ShapeDtypeStruct((bsz, N_HEADS, HEAD_DIM, HEAD_DIM), F32)],
        scratch_shapes=[pltpu.VMEM((SUBLANES, N_RW), F32),
                        pltpu.VMEM((N_PAIRS, PAIR, PAIR), F32),
                        pltpu.VMEM((2, tm, D_MODEL), F32),
                        pltpu.VMEM((N_PAIRS, tm, PAIR), F32)],
        compiler_params=pltpu.CompilerParams(
            dimension_semantics=("arbitrary",), vmem_limit_bytes=VMEM_LIMIT),
        name="rwkv_prompt",
    )(x, *consts)


def _conv_gate(p, u, u1, u2, cw):
    conv = cw[0:1, :] * u2 + cw[1:2, :] * u1 + cw[2:3, :] * u
    return p[:, D_MODEL:2 * D_MODEL] * conv * _silu(p[:, 3 * D_MODEL:])


def _merge_out(x, h, yc, yr, wg_ref, woc_ref, wor_ref, wo_ref, fg_ref):
    gates = jnp.dot(h, wg_ref[...], preferred_element_type=F32)
    pr = jnp.dot(yr, wor_ref[...], preferred_element_type=F32)
    pc = jnp.dot(yc, woc_ref[...], preferred_element_type=F32)
    m = _sigmoid(gates[:, :D_MODEL]) * pc + _sigmoid(gates[:, D_MODEL:]) * pr
    out = jnp.dot(m.astype(BF16), wo_ref[...], preferred_element_type=F32)
    return _rms(x + out, fg_ref[...])


def _out_consts(wts):
    return [wts["w_gate"], wts["w_out_c"], wts["w_out_r"], wts["w_o"], _row(wts["final_g"])]


def _conv_out_kernel(x_ref, hb_ref, yr_ref, g_ref, wc_ref, cw_ref, wg_ref, woc_ref, wor_ref, wo_ref,
                     fg_ref, y_ref, u_ref, h_ref, carry_ref, *, tm):
    @pl.when(pl.program_id(1) == 0)
    def _():
        carry_ref[...] = jnp.zeros_like(carry_ref)

    x = x_ref[0]
    h = hb_ref[0]
    p = jnp.dot(h, wc_ref[...], preferred_element_type=F32)
    u = p[:, 2 * D_MODEL:3 * D_MODEL] * p[:, 0:D_MODEL]
    carry = carry_ref[...]
    y_c = _conv_gate(p, u, _shift_rows(u, 1, carry), _shift_rows(u, 2, carry), cw_ref[...])
    carry_ref[...] = u[tm - SUBLANES:, :]
    u_ref[0] = u[tm - SUBLANES:, :]
    h_ref[0] = _rms(x[tm - SUBLANES:, :], g_ref[...])
    y_ref[0] = _merge_out(x, h, y_c.astype(BF16), yr_ref[0], wg_ref, woc_ref, wor_ref, wo_ref,
                          fg_ref)


def _conv_out_prompt(x, hb, yr, wts, tm=512):
    bsz, seq, _ = x.shape
    consts = [_row(wts["norm_g"]), wts["w_conv"], wts["conv_w"]] + _out_consts(wts)
    single = lambda c: pl.BlockSpec(c.shape, lambda *_: (0,) * c.ndim, pipeline_mode=pl.Buffered(1))
    tile = pl.BlockSpec((1, tm, D_MODEL), lambda b, s: (b, s, 0))
    tail = pl.BlockSpec((1, SUBLANES, D_MODEL), lambda b, s: (b, 0, 0))
    return pl.pallas_call(
        functools.partial(_conv_out_kernel, tm=tm),
        grid=(bsz, seq // tm),
        in_specs=[tile, tile, tile] + [single(c) for c in consts],
        out_specs=[tile, tail, tail],
        out_shape=[jax.ShapeDtypeStruct((bsz, seq, D_MODEL), F32),
                   jax.ShapeDtypeStruct((bsz, SUBLANES, D_MODEL), F32),
                   jax.ShapeDtypeStruct((bsz, SUBLANES, D_MODEL), F32)],
        scratch_shapes=[pltpu.VMEM((SUBLANES, D_MODEL), F32)],
        compiler_params=pltpu.CompilerParams(
            dimension_semantics=("arbitrary", "arbitrary"), vmem_limit_bytes=VMEM_LIMIT),
        name="conv_out_prompt",
    )(x, hb, yr, *consts)


N_SCAN_OPS = 6
V_BLOCK = 8
V_SUB = 2


def _sample_front_kernel(x_ref, hl_ref, cb0_ref, cb1_ref, g_ref, wc_ref, cw_ref, w_ref, mu_ref,
                         w0_ref, lw_ref, a0_ref, kk_ref, ka_ref, rk_ref,
                         yc_ref, u_ref, h_ref, tr_ref, bz_ref,
                         hp_s, u1_s, u2_s):
    @pl.when(pl.program_id(0) == 0)
    def _():
        hp_s[...] = hl_ref[...]
        u1_s[...] = cb1_ref[...]
        u2_s[...] = cb0_ref[...]

    nb = x_ref.shape[1]
    hf = _rms(x_ref[0], g_ref[...])
    h = hf.astype(BF16)
    pc = jnp.dot(h, wc_ref[...], preferred_element_type=F32)
    u = pc[:, 2 * D_MODEL:3 * D_MODEL] * pc[:, 0:D_MODEL]
    u1 = u1_s[...]
    yc_ref[0] = _conv_gate(pc, u, u1, u2_s[...], cw_ref[...]).astype(BF16)
    u2_s[...] = u1
    u1_s[...] = u
    u_ref[0] = u
    h_ref[0] = hf
    both = jnp.dot(jnp.concatenate([h, hp_s[...].astype(BF16)], axis=0), w_ref[...],
                   preferred_element_type=F32)
    hp_s[...] = hf
    p, pp = both[:nb], both[nb:]
    xm = p + (pp - p) * mu_ref[...]
    r, k_h, v, kk, b, logd, zr = _rwkv_pointwise(xm, w0_ref[...], lw_ref[...], a0_ref[...],
                                                 kk_ref[...], ka_ref[...])
    for i, arr in enumerate((kk, b, jnp.exp(logd), k_h, r, v)):
        tr_ref[0, i] = arr.T
    bz_ref[0, 0] = _bonus(r, k_h, v, rk_ref[...])
    bz_ref[0, 1] = _silu(zr)


def _sample_front(xs_t, h_last, cb0, cb1, wts):
    n_tok, nb, _ = xs_t.shape
    consts = ([_row(wts["norm_g"]), wts["w_conv"], wts["conv_w"]] + _rwkv_consts(wts)[1:])
    tok = pl.BlockSpec((1, nb, D_MODEL), lambda t: (t, 0, 0))
    seq = pl.BlockSpec((nb, D_MODEL), lambda t: (0, 0))
    return pl.pallas_call(
        _sample_front_kernel,
        grid=(n_tok,),
        in_specs=[tok, seq, seq, seq] + [_const_spec(c.shape) for c in consts],
        out_specs=[tok, tok, tok,
                   pl.BlockSpec((1, N_SCAN_OPS, D_MODEL, nb), lambda t: (t, 0, 0, 0)),
                   pl.BlockSpec((1, 2, nb, D_MODEL), lambda t: (t, 0, 0, 0))],
        out_shape=[jax.ShapeDtypeStruct((n_tok, nb, D_MODEL), BF16),
                   jax.ShapeDtypeStruct((n_tok, nb, D_MODEL), F32),
                   jax.ShapeDtypeStruct((n_tok, nb, D_MODEL), F32),
                   jax.ShapeDtypeStruct((n_tok, N_SCAN_OPS, D_MODEL, nb), F32),
                   jax.ShapeDtypeStruct((n_tok, 2, nb, D_MODEL), F32)],
        scratch_shapes=[pltpu.VMEM((nb, D_MODEL), F32) for _ in range(3)],
        compiler_params=pltpu.CompilerParams(
            dimension_semantics=("arbitrary",), vmem_limit_bytes=VMEM_LIMIT),
        name="sample_front",
    )(xs_t, h_last, cb0, cb1, *consts)


def _sample_scan_kernel(s0_ref, tr_ref, s1_ref, y_ref, *, n_tok):
    def block(vb, carry):
        rows = pl.ds(pl.multiple_of(vb * V_BLOCK, V_BLOCK), V_BLOCK)
        ys = [[] for _ in range(n_tok)]
        for v0 in range(0, V_BLOCK, V_SUB):
            s = s0_ref[0, rows][v0:v0 + V_SUB]
            for t in range(n_tok):
                kk, b, d, k, r = [tr_ref[t, i][None] for i in range(5)]
                vv = tr_ref[t, 5, rows, :][v0:v0 + V_SUB, None, :]
                sa = -jnp.sum(s * kk, axis=1, keepdims=True)
                s = s * d + sa * b + vv * k
                ys[t].append(jnp.sum(s * r, axis=1))
            s1_ref[0, pl.ds(pl.multiple_of(vb * V_BLOCK, V_BLOCK) + v0, V_SUB)] = s
        for t in range(n_tok):
            y_ref[t, rows, :] = jnp.concatenate(ys[t], axis=0)
        return carry

    lax.fori_loop(0, HEAD_DIM // V_BLOCK, block, 0)


def _sample_scan(s0_t, tr):
    n_tok = tr.shape[0]
    nb = s0_t.shape[-1]
    sblk = pl.BlockSpec((1, HEAD_DIM, HEAD_DIM, nb), lambda h: (h, 0, 0, 0))
    return pl.pallas_call(
        functools.partial(_sample_scan_kernel, n_tok=n_tok),
        grid=(N_HEADS,),
        in_specs=[sblk, pl.BlockSpec((n_tok, N_SCAN_OPS, HEAD_DIM, nb), lambda h: (0, 0, h, 0))],
        out_specs=[sblk, pl.BlockSpec((n_tok, HEAD_DIM, nb), lambda h: (0, h, 0))],
        out_shape=[jax.ShapeDtypeStruct(s0_t.shape, F32),
                   jax.ShapeDtypeStruct((n_tok, D_MODEL, nb), F32)],
        compiler_params=pltpu.CompilerParams(
            dimension_semantics=("arbitrary",), vmem_limit_bytes=VMEM_LIMIT),
        name="sample_scan",
    )(s0_t, tr)


def _sample_out_kernel(x_ref, yc_ref, yt_ref, bz_ref, lnw_ref, lnb_ref,
                       g_ref, wg_ref, woc_ref, wor_ref, wo_ref, fg_ref, y_ref):
    ys = yt_ref[0].T
    y_r = (_group_norm(ys, lnw_ref[...], lnb_ref[...]) + bz_ref[0, 0]) * bz_ref[0, 1]
    x = x_ref[0]
    y_ref[0] = _merge_out(x, _rms(x, g_ref[...]).astype(BF16), yc_ref[0], y_r.astype(BF16),
                          wg_ref, woc_ref, wor_ref, wo_ref, fg_ref)


def _sample_out(xs_t, yc, y_t, bz, wts):
    n_tok, nb, _ = xs_t.shape
    consts = [_row(wts["ln_w"]), _row(wts["ln_b"]), _row(wts["norm_g"])] + _out_consts(wts)
    tok = pl.BlockSpec((1, nb, D_MODEL), lambda t: (t, 0, 0))
    return pl.pallas_call(
        _sample_out_kernel,
        grid=(n_tok,),
        in_specs=[tok, tok, pl.BlockSpec((1, D_MODEL, nb), lambda t: (t, 0, 0)),
                  pl.BlockSpec((1, 2, nb, D_MODEL), lambda t: (t, 0, 0, 0))]
        + [_const_spec(c.shape) for c in consts],
        out_specs=tok,
        out_shape=jax.ShapeDtypeStruct((n_tok, nb, D_MODEL), F32),
        compiler_params=pltpu.CompilerParams(
            dimension_semantics=("arbitrary",), vmem_limit_bytes=VMEM_LIMIT),
        name="sample_out",
    )(xs_t, yc, y_t, bz, *consts)


def _prep_weights(norm_g, w_in, conv_w, mu_shift, w0, w_up, a0, a_up, k_k, k_a, r_k, ln_w, ln_b,
                  w_out_c, w_out_r, w_o, final_g):
    d = D_MODEL
    c0 = 4 * d
    zeros = jnp.zeros((LORA, d), F32)
    lora_w = jnp.concatenate([jnp.concatenate([w_up, zeros], axis=1),
                              jnp.concatenate([zeros, a_up], axis=1)], axis=0)
    return dict(
        norm_g=norm_g, final_g=final_g, conv_w=conv_w,
        w_conv=w_in[:, :c0].astype(BF16),
        w_rw=w_in[:, c0:c0 + N_RW].astype(BF16),
        mu_rw=mu_shift,
        w_gate=w_in[:, c0 + N_RW:].astype(BF16),
        w0=w0, a0=a0, lora_w=lora_w.astype(BF16), k_k=k_k, k_a=k_a, r_k=r_k.reshape(-1),
        ln_w=ln_w, ln_b=ln_b,
        w_out_c=w_out_c.astype(BF16), w_out_r=w_out_r.astype(BF16), w_o=w_o.astype(BF16))


def _layer_prompt(x, wts):
    yr, hb, s_fin = _rwkv_prompt(x, wts)
    y, u_tail, h_tail = _conv_out_prompt(x, hb, yr, wts)
    return (y, u_tail[:, SUBLANES - 2:, :], h_tail[:, SUBLANES - 1, :], s_fin)


def _layer_sample(x, conv_buf, h_last, s0, wts):
    seq = x.shape[1]
    xs_t = jnp.swapaxes(x, 0, 1)
    yc, u, h, tr, bz = _sample_front(xs_t, h_last, conv_buf[:, 0], conv_buf[:, 1], wts)
    s1_t, y_t = _sample_scan(jnp.transpose(s0, (1, 2, 3, 0)), tr)
    y = _sample_out(xs_t, yc, y_t, bz, wts)
    return (jnp.swapaxes(y, 0, 1), jnp.swapaxes(u[seq - 2:], 0, 1), h[seq - 1],
            jnp.transpose(s1_t, (3, 0, 1, 2)))


def kernel(x_prompt, x_sample, state_conv, state_shift, state_rwkv, norm_g, w_in, conv_w, mu_shift,
           w0, w_up, a0, a_up, k_k, k_a, r_k, ln_w, ln_b, w_out_c, w_out_r, w_o, final_g):
    assert norm_g.shape[0] == 1, "single-layer step"
    wts = _prep_weights(norm_g[0], w_in[0], conv_w[0], mu_shift[0], w0[0], w_up[0], a0[0], a_up[0],
                        k_k[0], k_a[0], r_k[0], ln_w[0], ln_b[0], w_out_c[0], w_out_r[0], w_o[0],
                        final_g)
    y_p, c_p, s_p, r_p = _layer_prompt(x_prompt, wts)
    y_s, c_s, s_s, r_s = _layer_sample(x_sample, state_conv[0], state_shift[0], state_rwkv[0], wts)
    lead = lambda a: a[None]
    return (y_p, y_s, lead(c_p), lead(s_p), lead(r_p), lead(c_s), lead(s_s), lead(r_s))
```

```python
import functools
import math

import jax
import jax.numpy as jnp
from jax import lax
from jax.experimental import pallas as pl
from jax.experimental.pallas import tpu as pltpu

F32 = jnp.float32
BF16 = jnp.bfloat16

D_MODEL = 1024
HEAD_DIM = 64
N_HEADS = D_MODEL // HEAD_DIM
LORA = 64
RMS_EPS = 1e-6
GN_EPS = 64e-5
EXP_M05 = math.exp(-0.5)
LANES = 128
SUBLANES = 8
PAIR = 2 * HEAD_DIM
N_PAIRS = D_MODEL // PAIR
CHUNK = 64
N_RW = 4 * D_MODEL + 2 * LORA
VMEM_LIMIT = 56 * 1024 * 1024


def _dot(a, b):
    return jnp.dot(a.astype(BF16), b.astype(BF16), preferred_element_type=F32)


def _dot_nt(a, b):
    return lax.dot_general(a.astype(BF16), b.astype(BF16), (((1,), (1,)), ((), ())),
                           preferred_element_type=F32)


def _dot_tn(a, b):
    return lax.dot_general(a.astype(BF16), b.astype(BF16), (((0,), (0,)), ((), ())),
                           preferred_element_type=F32)


def _split(x, n):
    parts = []
    for i in range(n):
        piece = x.astype(BF16)
        parts.append(piece)
        if i + 1 < n:
            x = x - piece.astype(F32)
    return parts


def _dot_exact_lhs(m_bf16, x, n=2):
    return sum(jnp.dot(m_bf16, part, preferred_element_type=F32) for part in _split(x, n))


def _rms(x, g):
    return x * lax.rsqrt(jnp.mean(x * x, axis=-1, keepdims=True) + RMS_EPS) * g


def _sigmoid(x):
    return 1.0 / (1.0 + jnp.exp(-x))


def _silu(x):
    return x * _sigmoid(x)


def _iota(shape, dim):
    return lax.broadcasted_iota(jnp.int32, shape, dim)


def _shift_rows(x, n, carry8):
    rolled = pltpu.roll(x, n, 0)
    head = jnp.where(_iota((SUBLANES, x.shape[1]), 0) < n, pltpu.roll(carry8, n, 0),
                     rolled[:SUBLANES])
    return jnp.concatenate([head, rolled[SUBLANES:]], axis=0)


def _seg_sum(x):
    lo = _iota((1, PAIR), 1) < HEAD_DIM
    tiles = []
    for p in range(x.shape[1] // PAIR):
        xp = x[:, p * PAIR:(p + 1) * PAIR]
        s_lo = jnp.sum(jnp.where(lo, xp, 0.0), axis=-1, keepdims=True)
        s_hi = jnp.sum(jnp.where(lo, 0.0, xp), axis=-1, keepdims=True)
        tiles.append(jnp.where(lo, s_lo, s_hi))
    return jnp.concatenate(tiles, axis=1)


def _block_cumsum(x, blk):
    rows, width = x.shape
    r = _iota((rows, rows), 0)
    c = _iota((rows, rows), 1)
    m_low = jnp.where(((r // blk) == (c // blk)) & (c <= r), 1.0, 0.0).astype(BF16)
    cum3 = _dot_exact_lhs(m_low, x).reshape(rows // blk, blk, width)
    return cum3, cum3[:, blk - 1:blk, :]


def _rwkv_pointwise(xm, w0, lora_w, a0, k_k, k_a):
    r = xm[:, 0:D_MODEL]
    k = xm[:, D_MODEL:2 * D_MODEL]
    v = xm[:, 2 * D_MODEL:3 * D_MODEL]
    da = xm[:, 3 * D_MODEL:3 * D_MODEL + 2 * LORA]
    zr = xm[:, 3 * D_MODEL + 2 * LORA:]
    lane = _iota(da.shape, 1)
    lora_in = jnp.where(lane < LORA, jnp.tanh(da), da)
    lo = _dot(lora_in, lora_w)
    w_logit = w0 + lo[:, :D_MODEL]
    logd = -EXP_M05 * _sigmoid(w_logit)
    a = _sigmoid(a0 + lo[:, D_MODEL:])
    kkr = k * k_k
    kk = kkr * jnp.minimum(lax.rsqrt(_seg_sum(kkr * kkr)), 1e12)
    k_h = k * (1.0 + (a - 1.0) * k_a)
    b = kk * a
    return r, k_h, v, kk, b, logd, zr


def _group_norm(ys, ln_w, ln_b):
    inv_n = 1.0 / HEAD_DIM
    yc = ys - _seg_sum(ys) * inv_n
    return yc * lax.rsqrt(_seg_sum(yc * yc) * inv_n + GN_EPS) * ln_w + ln_b


def _bonus(r, k_h, v, r_k):
    return _seg_sum(r * k_h * r_k) * v


def _scan_operands(r, k_h, v, kk, b, logd, cum3, tot3):
    shape = cum3.shape
    flat = lambda x: x.reshape(shape[0] * shape[1], shape[2])
    cum = flat(cum3)
    g_c3 = jnp.exp(tot3)
    enc3 = jnp.exp(-cum3)
    enc = flat(enc3)
    e_c = flat(enc3 * g_c3)
    khat = kk * jnp.exp(cum - logd)
    rhat = r * jnp.exp(cum)
    btil = b * enc
    ktil = k_h * enc
    bchk = -(b * e_c)
    kchk = k_h * e_c
    return khat, rhat, btil, ktil, bchk, kchk, g_c3


def _chunk_transfer(units, m_bd, m_sl, m_l):
    m_bd2 = jnp.concatenate([m_bd, m_bd], axis=1)
    zero = jnp.zeros((), BF16)

    def expand(x):
        x = x.astype(BF16)
        x2 = jnp.concatenate([x, x], axis=0)
        return jnp.where(m_bd if x.shape[1] == PAIR else m_bd2, x2, zero)

    gs = [_dot_nt(jnp.concatenate([kh, rh.astype(BF16)], axis=0),
                  jnp.concatenate([expand(bt), expand(kt)], axis=0))
          for kh, rh, bt, kt, _, _, _ in units]
    ps = [jnp.where(m_sl, -g[:CHUNK, :PAIR], 0.0).astype(BF16) for g in gs]
    m_as = [jnp.concatenate([jnp.where(m_sl, g[:CHUNK, PAIR:], 0.0),
                             jnp.where(m_l, g[CHUNK:, PAIR:], 0.0)], axis=0).astype(BF16)
            for g in gs]
    m_rbs = [jnp.where(m_l, -g[CHUNK:, :PAIR], 0.0).astype(BF16) for g in gs]
    zs = [_dot(m_a, expand(u[6])) for m_a, u in zip(m_as, units)]
    eye = jnp.where(m_l & ~m_sl, 1.0, 0.0)
    ts = [eye + p.astype(F32) for p in ps]
    ps = [_dot(p, expand(p)).astype(BF16) for p in ps]
    for _ in range(4):
        rs = [_dot(p, expand(jnp.concatenate([p, t.astype(BF16)], axis=1)))
              for p, t in zip(ps, ts)]
        ps = [r[:, :PAIR].astype(BF16) for r in rs]
        ts = [t + r[:, PAIR:] for t, r in zip(ts, rs)]
    ts = [t + _dot(p, expand(t)) for p, t in zip(ps, ts)]
    xs = [_dot(t, expand(jnp.concatenate([u[0], z[:CHUNK].astype(BF16)], axis=1))).astype(BF16)
          for t, u, z in zip(ts, units, zs)]
    t1s = [_dot(m_rb, expand(x)) for m_rb, x in zip(m_rbs, xs)]
    p_ts = [_dot_tn(x[:, :PAIR], u[4]) for x, u in zip(xs, units)]
    q_ts = [_dot_tn(jnp.concatenate([x[:, PAIR:], u[6]], axis=0),
                    jnp.concatenate([u[4], u[5]], axis=0)) for x, u in zip(xs, units)]
    out = []
    for u, z, t1, p_t, q_t in zip(units, zs, t1s, p_ts, q_ts):
        y_w = (u[1] + t1[:, :PAIR]).astype(BF16)
        y_c = t1[:, PAIR:] + z[CHUNK:]
        out.append((y_w, y_c, jnp.where(m_bd, p_t, 0.0).astype(BF16), jnp.where(m_bd, q_t, 0.0)))
    return out


def _rwkv_prompt_kernel(x_ref, g_ref, w_ref, mu_ref, w0_ref, lw_ref, a0_ref, kk_ref, ka_ref,
                        rk_ref, lnw_ref, lnb_ref,
                        yr_ref, h_ref, sfin_ref,
                        carry_ref, st_ref, bz_ref, ys_ref, *, tm, n_tiles, n_steps):
    g = pl.program_id(0)
    n_chunks = tm // CHUNK
    first = g % n_tiles == 0

    @pl.when(g == 0)
    def _():
        for ref in (carry_ref, st_ref, bz_ref, ys_ref):
            ref[...] = jnp.zeros_like(ref)

    def finish_previous():
        ys = jnp.concatenate([ys_ref[hp] for hp in range(N_PAIRS)], axis=1)
        y_r = (_group_norm(ys, lnw_ref[...], lnb_ref[...]) + bz_ref[0]) * bz_ref[1]
        yr_ref[0] = y_r.astype(BF16)
        return y_r

    @pl.when(g == n_steps - 1)
    def _():
        finish_previous()

    pl.when(g < n_steps - 1)(functools.partial(
        _rwkv_prompt_step, x_ref, g_ref, w_ref, mu_ref, w0_ref, lw_ref, a0_ref, kk_ref, ka_ref, rk_ref,
        h_ref, sfin_ref, carry_ref, st_ref, bz_ref, ys_ref, finish_previous, first, tm, n_chunks))


def _rwkv_prompt_step(x_ref, g_ref, w_ref, mu_ref, w0_ref, lw_ref, a0_ref, kk_ref, ka_ref, rk_ref,
                      h_ref, sfin_ref, carry_ref, st_ref, bz_ref, ys_ref, finish_previous, first,
                      tm, n_chunks):
    h = _rms(x_ref[0], g_ref[...]).astype(BF16)
    h_ref[0] = h
    p = jnp.dot(h, w_ref[...], preferred_element_type=F32)

    y_r = finish_previous()
    acc = y_r.reshape(tm // SUBLANES, SUBLANES, D_MODEL).sum(axis=0)
    acc = sum(acc[:, c * LANES:(c + 1) * LANES] for c in range(D_MODEL // LANES))
    bits = lax.bitcast_convert_type(acc[0:1, :], jnp.uint32)
    bits = lax.shift_right_logical(lax.shift_right_logical(bits, jnp.uint32(16)), jnp.uint32(16))
    after = jnp.concatenate([lax.bitcast_convert_type(bits, F32)] * (N_RW // LANES), axis=1)

    prev = _shift_rows(p, 1, jnp.where(first, 0.0, carry_ref[...]))
    carry_ref[...] = p[tm - SUBLANES:, :]
    xm = p + (prev - p) * (mu_ref[...] + after)
    r, k_h, v, kk, b, logd, zr = _rwkv_pointwise(xm, w0_ref[...], lw_ref[...], a0_ref[...],
                                                 kk_ref[...], ka_ref[...])
    cum3, tot3 = _block_cumsum(logd, CHUNK)
    khat, rhat, btil, ktil, bchk, kchk, g_c3 = _scan_operands(r, k_h, v, kk, b, logd, cum3, tot3)
    g_c = g_c3.reshape(n_chunks, D_MODEL)
    bz_ref[0] = _bonus(r, k_h, v, rk_ref[...])
    bz_ref[1] = _silu(zr)

    rr = _iota((PAIR, PAIR), 0)
    cc = _iota((PAIR, PAIR), 1)
    m_bd = (rr // HEAD_DIM) == (cc // HEAD_DIM)
    t_row = _iota((CHUNK, PAIR), 0)
    s_col = _iota((CHUNK, PAIR), 1) % HEAD_DIM
    m_sl = s_col < t_row
    m_l = s_col <= t_row
    rows = [slice(c * CHUNK, (c + 1) * CHUNK) for c in range(n_chunks)]
    lanes = [slice(hp * PAIR, (hp + 1) * PAIR) for hp in range(N_PAIRS)]
    kh16, bt16, kt16, bc16, kc16, v16 = [a.astype(BF16) for a in (khat, btil, ktil, bchk, kchk, v)]
    units = [(kh16[rw, ln], rhat[rw, ln], bt16[rw, ln], kt16[rw, ln], bc16[rw, ln], kc16[rw, ln],
              v16[rw, ln]) for ln in lanes for rw in rows]
    tr = _chunk_transfer(units, m_bd, m_sl, m_l)
    states = [jnp.where(first, 0.0, st_ref[hp]) for hp in range(N_PAIRS)]
    for c in range(n_chunks):
        for hp in range(N_PAIRS):
            y_w, y_c, p_t, q_t = tr[hp * n_chunks + c]
            ys_ref[hp, rows[c], :] = _dot_nt(y_w, states[hp]) + y_c
            states[hp] = states[hp] * g_c[c:c + 1, lanes[hp]] + _dot(states[hp], p_t) + q_t
    for hp in range(N_PAIRS):
        sp = states[hp]
        st_ref[hp] = sp
        sfin_ref[0, 2 * hp] = sp[:HEAD_DIM, :HEAD_DIM]
        sfin_ref[0, 2 * hp + 1] = sp[HEAD_DIM:, HEAD_DIM:]


def _const_spec(shape):
    nd = len(shape)
    return pl.BlockSpec(shape, lambda *_: (0,) * nd)


def _row(a):
    return a.reshape(1, -1)


def _rwkv_consts(wts):
    return [_row(wts["norm_g"]), wts["w_rw"], _row(wts["mu_rw"]), _row(wts["w0"]), wts["lora_w"],
            _row(wts["a0"]), _row(wts["k_k"]), _row(wts["k_a"]), _row(wts["r_k"])]


def _rwkv_prompt(x, wts, tm=512):
    bsz, seq, _ = x.shape
    consts = _rwkv_consts(wts) + [_row(wts["ln_w"]), _row(wts["ln_b"])]
    const_specs = [_const_spec(c.shape) for c in consts]
    const_specs[1] = pl.BlockSpec(consts[1].shape, lambda *_: (0, 0), pipeline_mode=pl.Buffered(1))
    n_tiles = seq // tm
    n_steps = bsz * n_tiles + 1
    cur = lambda g: jnp.minimum(g, n_steps - 2)
    prv = lambda g: jnp.maximum(g - 1, 0)
    tile = pl.BlockSpec((1, tm, D_MODEL), lambda g: (cur(g) // n_tiles, cur(g) % n_tiles, 0))
    kern = functools.partial(_rwkv_prompt_kernel, tm=tm, n_tiles=n_tiles, n_steps=n_steps)
    return pl.pallas_call(
        kern,
        grid=(n_steps,),
        in_specs=[tile] + const_specs,
        out_specs=[pl.BlockSpec((1, tm, D_MODEL), lambda g: (prv(g) // n_tiles, prv(g) % n_tiles, 0)),
                   tile,
                   pl.BlockSpec((1, N_HEADS, HEAD_DIM, HEAD_DIM),
                                lambda g: (cur(g) // n_tiles, 0, 0, 0))],
        out_shape=[jax.ShapeDtypeStruct((bsz, seq, D_MODEL), BF16),
                   jax.ShapeDtypeStruct((bsz, seq, D_MODEL), BF16),
                   jax.ShapeDtypeStruct((bsz, N_HEADS, HEAD_DIM, HEAD_DIM), F32)],
        scratch_shapes=[pltpu.VMEM((SUBLANES, N_RW), F32),
                        pltpu.VMEM((N_PAIRS, PAIR, PAIR), F32),
                        pltpu.VMEM((2, tm, D_MODEL), F32),
                        pltpu.VMEM((N_PAIRS, tm, PAIR), F32)],
        compiler_params=pltpu.CompilerParams(
            dimension_semantics=("arbitrary",), vmem_limit_bytes=VMEM_LIMIT),
        name="rwkv_prompt",
    )(x, *consts)


def _conv_gate(p, u, u1, u2, cw):
    conv = cw[0:1, :] * u2 + cw[1:2, :] * u1 + cw[2:3, :] * u
    return p[:, D_MODEL:2 * D_MODEL] * conv * _silu(p[:, 3 * D_MODEL:])


def _merge_out(x, h, yc, yr, wg_ref, woc_ref, wor_ref, wo_ref, fg_ref):
    gates = jnp.dot(h, wg_ref[...], preferred_element_type=F32)
    pr = jnp.dot(yr, wor_ref[...], preferred_element_type=F32)
    pc = jnp.dot(yc, woc_ref[...], preferred_element_type=F32)
    m = _sigmoid(gates[:, :D_MODEL]) * pc + _sigmoid(gates[:, D_MODEL:]) * pr
    out = jnp.dot(m.astype(BF16), wo_ref[...], preferred_element_type=F32)
    return _rms(x + out, fg_ref[...])


def _out_consts(wts):
    return [wts["w_gate"], wts["w_out_c"], wts["w_out_r"], wts["w_o"], _row(wts["final_g"])]


def _conv_out_kernel(x_ref, hb_ref, yr_ref, g_ref, wc_ref, cw_ref, wg_ref, woc_ref, wor_ref, wo_ref,
                     fg_ref, y_ref, u_ref, h_ref, carry_ref, *, tm):
    @pl.when(pl.program_id(1) == 0)
    def _():
        carry_ref[...] = jnp.zeros_like(carry_ref)

    x = x_ref[0]
    h = hb_ref[0]
    p = jnp.dot(h, wc_ref[...], preferred_element_type=F32)
    u = p[:, 2 * D_MODEL:3 * D_MODEL] * p[:, 0:D_MODEL]
    carry = carry_ref[...]
    y_c = _conv_gate(p, u, _shift_rows(u, 1, carry), _shift_rows(u, 2, carry), cw_ref[...])
    carry_ref[...] = u[tm - SUBLANES:, :]
    u_ref[0] = u[tm - SUBLANES:, :]
    h_ref[0] = _rms(x[tm - SUBLANES:, :], g_ref[...])
    y_ref[0] = _merge_out(x, h, y_c.astype(BF16), yr_ref[0], wg_ref, woc_ref, wor_ref, wo_ref,
                          fg_ref)


def _conv_out_prompt(x, hb, yr, wts, tm=512):
    bsz, seq, _ = x.shape
    consts = [_row(wts["norm_g"]), wts["w_conv"], wts["conv_w"]] + _out_consts(wts)
    single = lambda c: pl.BlockSpec(c.shape, lambda *_: (0,) * c.ndim, pipeline_mode=pl.Buffered(1))
    tile = pl.BlockSpec((1, tm, D_MODEL), lambda b, s: (b, s, 0))
    tail = pl.BlockSpec((1, SUBLANES, D_MODEL), lambda b, s: (b, 0, 0))
    return pl.pallas_call(
        functools.partial(_conv_out_kernel, tm=tm),
        grid=(bsz, seq // tm),
        in_specs=[tile, tile, tile] + [single(c) for c in consts],
        out_specs=[tile, tail, tail],
        out_shape=[jax.ShapeDtypeStruct((bsz, seq, D_MODEL), F32),
                   jax.ShapeDtypeStruct((bsz, SUBLANES, D_MODEL), F32),
                   jax.ShapeDtypeStruct((bsz, SUBLANES, D_MODEL), F32)],
        scratch_shapes=[pltpu.VMEM((SUBLANES, D_MODEL), F32)],
        compiler_params=pltpu.CompilerParams(
            dimension_semantics=("arbitrary", "arbitrary"), vmem_limit_bytes=VMEM_LIMIT),
        name="conv_out_prompt",
    )(x, hb, yr, *consts)


N_SCAN_OPS = 6
V_BLOCK = 8
V_SUB = 2


def _sample_front_kernel(x_ref, hl_ref, cb0_ref, cb1_ref, g_ref, wc_ref, cw_ref, w_ref, mu_ref,
                         w0_ref, lw_ref, a0_ref, kk_ref, ka_ref, rk_ref,
                         yc_ref, u_ref, h_ref, tr_ref, bz_ref,
                         hp_s, u1_s, u2_s):
    @pl.when(pl.program_id(0) == 0)
    def _():
        hp_s[...] = hl_ref[...]
        u1_s[...] = cb1_ref[...]
        u2_s[...] = cb0_ref[...]

    nb = x_ref.shape[1]
    hf = _rms(x_ref[0], g_ref[...])
    h = hf.astype(BF16)
    pc = jnp.dot(h, wc_ref[...], preferred_element_type=F32)
    u = pc[:, 2 * D_MODEL:3 * D_MODEL] * pc[:, 0:D_MODEL]
    u1 = u1_s[...]
    yc_ref[0] = _conv_gate(pc, u, u1, u2_s[...], cw_ref[...]).astype(BF16)
    u2_s[...] = u1
    u1_s[...] = u
    u_ref[0] = u
    h_ref[0] = hf
    both = jnp.dot(jnp.concatenate([h, hp_s[...].astype(BF16)], axis=0), w_ref[...],
                   preferred_element_type=F32)
    hp_s[...] = hf
    p, pp = both[:nb], both[nb:]
    xm = p + (pp - p) * mu_ref[...]
    r, k_h, v, kk, b, logd, zr = _rwkv_pointwise(xm, w0_ref[...], lw_ref[...], a0_ref[...],
                                                 kk_ref[...], ka_ref[...])
    for i, arr in enumerate((kk, b, jnp.exp(logd), k_h, r, v)):
        tr_ref[0, i] = arr.T
    bz_ref[0, 0] = _bonus(r, k_h, v, rk_ref[...])
    bz_ref[0, 1] = _silu(zr)


def _sample_front(xs_t, h_last, cb0, cb1, wts):
    n_tok, nb, _ = xs_t.shape
    consts = ([_row(wts["norm_g"]), wts["w_conv"], wts["conv_w"]] + _rwkv_consts(wts)[1:])
    tok = pl.BlockSpec((1, nb, D_MODEL), lambda t: (t, 0, 0))
    seq = pl.BlockSpec((nb, D_MODEL), lambda t: (0, 0))
    return pl.pallas_call(
        _sample_front_kernel,
        grid=(n_tok,),
        in_specs=[tok, seq, seq, seq] + [_const_spec(c.shape) for c in consts],
        out_specs=[tok, tok, tok,
                   pl.BlockSpec((1, N_SCAN_OPS, D_MODEL, nb), lambda t: (t, 0, 0, 0)),
                   pl.BlockSpec((1, 2, nb, D_MODEL), lambda t: (t, 0, 0, 0))],
        out_shape=[jax.ShapeDtypeStruct((n_tok, nb, D_MODEL), BF16),
                   jax.ShapeDtypeStruct((n_tok, nb, D_MODEL), F32),
                   jax.ShapeDtypeStruct((n_tok, nb, D_MODEL), F32),
                   jax.ShapeDtypeStruct((n_tok, N_SCAN_OPS, D_MODEL, nb), F32),
                   jax.ShapeDtypeStruct((n_tok, 2, nb, D_MODEL), F32)],
        scratch_shapes=[pltpu.VMEM((nb, D_MODEL), F32) for _ in range(3)],
        compiler_params=pltpu.CompilerParams(
            dimension_semantics=("arbitrary",), vmem_limit_bytes=VMEM_LIMIT),
        name="sample_front",
    )(xs_t, h_last, cb0, cb1, *consts)


def _sample_scan_kernel(s0_ref, tr_ref, s1_ref, y_ref, *, n_tok):
    def block(vb, carry):
        rows = pl.ds(pl.multiple_of(vb * V_BLOCK, V_BLOCK), V_BLOCK)
        ys = [[] for _ in range(n_tok)]
        for v0 in range(0, V_BLOCK, V_SUB):
            s = s0_ref[0, rows][v0:v0 + V_SUB]
            for t in range(n_tok):
                kk, b, d, k, r = [tr_ref[t, i][None] for i in range(5)]
                vv = tr_ref[t, 5, rows, :][v0:v0 + V_SUB, None, :]
                sa = -jnp.sum(s * kk, axis=1, keepdims=True)
                s = s * d + sa * b + vv * k
                ys[t].append(jnp.sum(s * r, axis=1))
            s1_ref[0, pl.ds(pl.multiple_of(vb * V_BLOCK, V_BLOCK) + v0, V_SUB)] = s
        for t in range(n_tok):
            y_ref[t, rows, :] = jnp.concatenate(ys[t], axis=0)
        return carry

    lax.fori_loop(0, HEAD_DIM // V_BLOCK, block, 0)


def _sample_scan(s0_t, tr):
    n_tok = tr.shape[0]
    nb = s0_t.shape[-1]
    sblk = pl.BlockSpec((1, HEAD_DIM, HEAD_DIM, nb), lambda h: (h, 0, 0, 0))
    return pl.pallas_call(
        functools.partial(_sample_scan_kernel, n_tok=n_tok),
        grid=(N_HEADS,),
        in_specs=[sblk, pl.BlockSpec((n_tok, N_SCAN_OPS, HEAD_DIM, nb), lambda h: (0, 0, h, 0))],
        out_specs=[sblk, pl.BlockSpec((n_tok, HEAD_DIM, nb), lambda h: (0, h, 0))],
        out_shape=[jax.ShapeDtypeStruct(s0_t.shape, F32),
                   jax.ShapeDtypeStruct((n_tok, D_MODEL, nb), F32)],
        compiler_params=pltpu.CompilerParams(
            dimension_semantics=("arbitrary",), vmem_limit_bytes=VMEM_LIMIT),
        name="sample_scan",
    )(s0_t, tr)


def _sample_out_kernel(x_ref, yc_ref, yt_ref, bz_ref, lnw_ref, lnb_ref,
                       g_ref, wg_ref, woc_ref, wor_ref, wo_ref, fg_ref, y_ref):
    ys = yt_ref[0].T
    y_r = (_group_norm(ys, lnw_ref[...], lnb_ref[...]) + bz_ref[0, 0]) * bz_ref[0, 1]
    x = x_ref[0]
    y_ref[0] = _merge_out(x, _rms(x, g_ref[...]).astype(BF16), yc_ref[0], y_r.astype(BF16),
                          wg_ref, woc_ref, wor_ref, wo_ref, fg_ref)


def _sample_out(xs_t, yc, y_t, bz, wts):
    n_tok, nb, _ = xs_t.shape
    consts = [_row(wts["ln_w"]), _row(wts["ln_b"]), _row(wts["norm_g"])] + _out_consts(wts)
    tok = pl.BlockSpec((1, nb, D_MODEL), lambda t: (t, 0, 0))
    return pl.pallas_call(
        _sample_out_kernel,
        grid=(n_tok,),
        in_specs=[tok, tok, pl.BlockSpec((1, D_MODEL, nb), lambda t: (t, 0, 0)),
                  pl.BlockSpec((1, 2, nb, D_MODEL), lambda t: (t, 0, 0, 0))]
        + [_const_spec(c.shape) for c in consts],
        out_specs=tok,
        out_shape=jax.ShapeDtypeStruct((n_tok, nb, D_MODEL), F32),
        compiler_params=pltpu.CompilerParams(
            dimension_semantics=("arbitrary",), vmem_limit_bytes=VMEM_LIMIT),
        name="sample_out",
    )(xs_t, yc, y_t, bz, *consts)


def _prep_weights(norm_g, w_in, conv_w, mu_shift, w0, w_up, a0, a_up, k_k, k_a, r_k, ln_w, ln_b,
                  w_out_c, w_out_r, w_o, final_g):
    d = D_MODEL
    c0 = 4 * d
    zeros = jnp.zeros((LORA, d), F32)
    lora_w = jnp.concatenate([jnp.concatenate([w_up, zeros], axis=1),
                              jnp.concatenate([zeros, a_up], axis=1)], axis=0)
    return dict(
        norm_g=norm_g, final_g=final_g, conv_w=conv_w,
        w_conv=w_in[:, :c0].astype(BF16),
        w_rw=w_in[:, c0:c0 + N_RW].astype(BF16),
        mu_rw=mu_shift,
        w_gate=w_in[:, c0 + N_RW:].astype(BF16),
        w0=w0, a0=a0, lora_w=lora_w.astype(BF16), k_k=k_k, k_a=k_a, r_k=r_k.reshape(-1),
        ln_w=ln_w, ln_b=ln_b,
        w_out_c=w_out_c.astype(BF16), w_out_r=w_out_r.astype(BF16), w_o=w_o.astype(BF16))


def _layer_prompt(x, wts):
    yr, hb, s_fin = _rwkv_prompt(x, wts)
    y, u_tail, h_tail = _conv_out_prompt(x, hb, yr, wts)
    return (y, u_tail[:, SUBLANES - 2:, :], h_tail[:, SUBLANES - 1, :], s_fin)


def _layer_sample(x, conv_buf, h_last, s0, wts):
    seq = x.shape[1]
    xs_t = jnp.swapaxes(x, 0, 1)
    yc, u, h, tr, bz = _sample_front(xs_t, h_last, conv_buf[:, 0], conv_buf[:, 1], wts)
    s1_t, y_t = _sample_scan(jnp.transpose(s0, (1, 2, 3, 0)), tr)
    y = _sample_out(xs_t, yc, y_t, bz, wts)
    return (jnp.swapaxes(y, 0, 1), jnp.swapaxes(u[seq - 2:], 0, 1), h[seq - 1],
            jnp.transpose(s1_t, (3, 0, 1, 2)))


def kernel(x_prompt, x_sample, state_conv, state_shift, state_rwkv, norm_g, w_in, conv_w, mu_shift,
           w0, w_up, a0, a_up, k_k, k_a, r_k, ln_w, ln_b, w_out_c, w_out_r, w_o, final_g):
    assert norm_g.shape[0] == 1, "single-layer step"
    wts = _prep_weights(norm_g[0], w_in[0], conv_w[0], mu_shift[0], w0[0], w_up[0], a0[0], a_up[0],
                        k_k[0], k_a[0], r_k[0], ln_w[0], ln_b[0], w_out_c[0], w_out_r[0], w_o[0],
                        final_g)
    y_p, c_p, s_p, r_p = _layer_prompt(x_prompt, wts)
    y_s, c_s, s_s, r_s = _layer_sample(x_sample, state_conv[0], state_shift[0], state_rwkv[0], wts)
    lead = lambda a: a[None]
    return (y_p, y_s, lead(c_p), lead(s_p), lead(r_p), lead(c_s), lead(s_s), lead(r_s))
```

```python
import functools
import math

import jax
import jax.numpy as jnp
from jax import lax
from jax.experimental import pallas as pl
from jax.experimental.pallas import tpu as pltpu

F32 = jnp.float32
BF16 = jnp.bfloat16

D_MODEL = 1024
HEAD_DIM = 64
N_HEADS = D_MODEL // HEAD_DIM
LORA = 64
RMS_EPS = 1e-6
GN_EPS = 64e-5
EXP_M05 = math.exp(-0.5)
LANES = 128
SUBLANES = 8
MXU_DEPTH = 256
PAIR = 2 * HEAD_DIM
N_PAIRS = D_MODEL // PAIR
CHUNK = 64
N_RW = 4 * D_MODEL + 2 * LORA
VMEM_LIMIT = 56 * 1024 * 1024


def _dot(a, b):
    return jnp.dot(a.astype(BF16), b.astype(BF16), preferred_element_type=F32)


def _dot_nt(a, b):
    return lax.dot_general(a.astype(BF16), b.astype(BF16), (((1,), (1,)), ((), ())),
                           preferred_element_type=F32)


def _dot_tn(a, b):
    return lax.dot_general(a.astype(BF16), b.astype(BF16), (((0,), (0,)), ((), ())),
                           preferred_element_type=F32)


def _split(x, n):
    parts = []
    for i in range(n):
        piece = x.astype(BF16)
        parts.append(piece)
        if i + 1 < n:
            x = x - piece.astype(F32)
    return parts


def _dot_exact_lhs(m_bf16, x, n=2):
    return sum(jnp.dot(m_bf16, part, preferred_element_type=F32) for part in _split(x, n))


def _rms(x, g):
    return x * lax.rsqrt(jnp.mean(x * x, axis=-1, keepdims=True) + RMS_EPS) * g


def _sigmoid(x):
    return 1.0 / (1.0 + jnp.exp(-x))


def _silu(x):
    return x * _sigmoid(x)


def _iota(shape, dim):
    return lax.broadcasted_iota(jnp.int32, shape, dim)


def _shift_rows(x, n, carry8):
    rolled = pltpu.roll(x, n, 0)
    head = jnp.where(_iota((SUBLANES, x.shape[1]), 0) < n, pltpu.roll(carry8, n, 0),
                     rolled[:SUBLANES])
    return jnp.concatenate([head, rolled[SUBLANES:]], axis=0)


def _seg_sum(x):
    lo = _iota((1, PAIR), 1) < HEAD_DIM
    tiles = []
    for p in range(x.shape[1] // PAIR):
        xp = x[:, p * PAIR:(p + 1) * PAIR]
        s_lo = jnp.sum(jnp.where(lo, xp, 0.0), axis=-1, keepdims=True)
        s_hi = jnp.sum(jnp.where(lo, 0.0, xp), axis=-1, keepdims=True)
        tiles.append(jnp.where(lo, s_lo, s_hi))
    return jnp.concatenate(tiles, axis=1)


def _block_cumsum(x, blk):
    rows, width = x.shape
    grp = min(rows, MXU_DEPTH)
    r = _iota((grp, grp), 0)
    c = _iota((grp, grp), 1)
    m_low = jnp.where(((r // blk) == (c // blk)) & (c <= r), 1.0, 0.0).astype(BF16)
    cum = jnp.concatenate([_dot_exact_lhs(m_low, x[i:i + grp]) for i in range(0, rows, grp)], axis=0)
    cum3 = cum.reshape(rows // blk, blk, width)
    return cum3, cum3[:, blk - 1:blk, :]


def _rwkv_pointwise(xm, w0, lora_w, a0, k_k, k_a):
    r = xm[:, 0:D_MODEL]
    k = xm[:, D_MODEL:2 * D_MODEL]
    v = xm[:, 2 * D_MODEL:3 * D_MODEL]
    da = xm[:, 3 * D_MODEL:3 * D_MODEL + 2 * LORA]
    zr = xm[:, 3 * D_MODEL + 2 * LORA:]
    lane = _iota(da.shape, 1)
    lora_in = jnp.where(lane < LORA, jnp.tanh(da), da)
    lo = _dot(lora_in, lora_w)
    w_logit = w0 + lo[:, :D_MODEL]
    logd = -EXP_M05 * _sigmoid(w_logit)
    a = _sigmoid(a0 + lo[:, D_MODEL:])
    kkr = k * k_k
    kk = kkr * jnp.minimum(lax.rsqrt(_seg_sum(kkr * kkr)), 1e12)
    k_h = k * (1.0 + (a - 1.0) * k_a)
    b = kk * a
    return r, k_h, v, kk, b, logd, zr


def _group_norm(ys, ln_w, ln_b):
    inv_n = 1.0 / HEAD_DIM
    yc = ys - _seg_sum(ys) * inv_n
    return yc * lax.rsqrt(_seg_sum(yc * yc) * inv_n + GN_EPS) * ln_w + ln_b


def _bonus(r, k_h, v, r_k):
    return _seg_sum(r * k_h * r_k) * v


def _scan_operands(r, k_h, v, kk, b, logd, cum3, tot3):
    shape = cum3.shape
    flat = lambda x: x.reshape(shape[0] * shape[1], shape[2])
    cum = flat(cum3)
    g_c3 = jnp.exp(tot3)
    enc3 = jnp.exp(-cum3)
    enc = flat(enc3)
    e_c = flat(enc3 * g_c3)
    khat = kk * jnp.exp(cum - logd)
    rhat = r * jnp.exp(cum)
    btil = b * enc
    ktil = k_h * enc
    bchk = -(b * e_c)
    kchk = k_h * e_c
    return khat, rhat, btil, ktil, bchk, kchk, g_c3


def _chunk_transfer(units, m_bd, m_sl, m_l):
    m_bd2 = jnp.concatenate([m_bd, m_bd], axis=1)
    zero = jnp.zeros((), BF16)

    def expand(x):
        x = x.astype(BF16)
        x2 = jnp.concatenate([x, x], axis=0)
        return jnp.where(m_bd if x.shape[1] == PAIR else m_bd2, x2, zero)

    gs = [_dot_nt(jnp.concatenate([kh, rh.astype(BF16)], axis=0),
                  jnp.concatenate([expand(bt), expand(kt)], axis=0))
          for kh, rh, bt, kt, _, _, _ in units]
    ps = [jnp.where(m_sl, -g[:CHUNK, :PAIR], 0.0).astype(BF16) for g in gs]
    m_as = [jnp.concatenate([jnp.where(m_sl, g[:CHUNK, PAIR:], 0.0),
                             jnp.where(m_l, g[CHUNK:, PAIR:], 0.0)], axis=0).astype(BF16)
            for g in gs]
    m_rbs = [jnp.where(m_l, -g[CHUNK:, :PAIR], 0.0).astype(BF16) for g in gs]
    zs = [_dot(m_a, expand(u[6])) for m_a, u in zip(m_as, units)]
    eye = jnp.where(m_l & ~m_sl, 1.0, 0.0)
    ts = [eye + p.astype(F32) for p in ps]
    ps = [_dot(p, expand(p)).astype(BF16) for p in ps]
    for _ in range(4):
        rs = [_dot(p, expand(jnp.concatenate([p, t.astype(BF16)], axis=1)))
              for p, t in zip(ps, ts)]
        ps = [r[:, :PAIR].astype(BF16) for r in rs]
        ts = [t + r[:, PAIR:] for t, r in zip(ts, rs)]
    ts = [t + _dot(p, expand(t)) for p, t in zip(ps, ts)]
    xs = [_dot(t, expand(jnp.concatenate([u[0], z[:CHUNK].astype(BF16)], axis=1))).astype(BF16)
          for t, u, z in zip(ts, units, zs)]
    t1s = [_dot(m_rb, expand(x)) for m_rb, x in zip(m_rbs, xs)]
    p_ts = [_dot_tn(x[:, :PAIR], u[4]) for x, u in zip(xs, units)]
    q_ts = [_dot_tn(jnp.concatenate([x[:, PAIR:], u[6]], axis=0),
                    jnp.concatenate([u[4], u[5]], axis=0)) for x, u in zip(xs, units)]
    out = []
    for u, z, t1, p_t, q_t in zip(units, zs, t1s, p_ts, q_ts):
        y_w = (u[1] + t1[:, :PAIR]).astype(BF16)
        y_c = t1[:, PAIR:] + z[CHUNK:]
        out.append((y_w, y_c, jnp.where(m_bd, p_t, 0.0).astype(BF16), jnp.where(m_bd, q_t, 0.0)))
    return out


def _rwkv_prompt_kernel(x_ref, g_ref, w_ref, mu_ref, w0_ref, lw_ref, a0_ref, kk_ref, ka_ref,
                        rk_ref, lnw_ref, lnb_ref,
                        yr_ref, h_ref, sfin_ref,
                        carry_ref, st_ref, bz_ref, ys_ref, *, tm, n_tiles, n_steps):
    g = pl.program_id(0)
    n_chunks = tm // CHUNK
    first = g % n_tiles == 0

    @pl.when(g == 0)
    def _():
        for ref in (carry_ref, st_ref, bz_ref, ys_ref):
            ref[...] = jnp.zeros_like(ref)

    def finish_previous():
        ys = jnp.concatenate([ys_ref[hp] for hp in range(N_PAIRS)], axis=1)
        y_r = (_group_norm(ys, lnw_ref[...], lnb_ref[...]) + bz_ref[0]) * bz_ref[1]
        yr_ref[0] = y_r.astype(BF16)
        return y_r

    @pl.when(g == n_steps - 1)
    def _():
        finish_previous()

    pl.when(g < n_steps - 1)(functools.partial(
        _rwkv_prompt_step, x_ref, g_ref, w_ref, mu_ref, w0_ref, lw_ref, a0_ref, kk_ref, ka_ref, rk_ref,
        h_ref, sfin_ref, carry_ref, st_ref, bz_ref, ys_ref, finish_previous, first, tm, n_chunks))


def _rwkv_prompt_step(x_ref, g_ref, w_ref, mu_ref, w0_ref, lw_ref, a0_ref, kk_ref, ka_ref, rk_ref,
                      h_ref, sfin_ref, carry_ref, st_ref, bz_ref, ys_ref, finish_previous, first,
                      tm, n_chunks):
    h = _rms(x_ref[0], g_ref[...]).astype(BF16)
    h_ref[0] = h
    p = jnp.dot(h, w_ref[...], preferred_element_type=F32)

    y_r = finish_previous()
    acc = y_r.reshape(tm // SUBLANES, SUBLANES, D_MODEL).sum(axis=0)
    acc = sum(acc[:, c * LANES:(c + 1) * LANES] for c in range(D_MODEL // LANES))
    bits = lax.bitcast_convert_type(acc[0:1, :], jnp.uint32)
    bits = lax.shift_right_logical(lax.shift_right_logical(bits, jnp.uint32(16)), jnp.uint32(16))
    after = jnp.concatenate([lax.bitcast_convert_type(bits, F32)] * (N_RW // LANES), axis=1)

    prev = _shift_rows(p, 1, jnp.where(first, 0.0, carry_ref[...]))
    carry_ref[...] = p[tm - SUBLANES:, :]
    xm = p + (prev - p) * (mu_ref[...] + after)
    r, k_h, v, kk, b, logd, zr = _rwkv_pointwise(xm, w0_ref[...], lw_ref[...], a0_ref[...],
                                                 kk_ref[...], ka_ref[...])
    cum3, tot3 = _block_cumsum(logd, CHUNK)
    khat, rhat, btil, ktil, bchk, kchk, g_c3 = _scan_operands(r, k_h, v, kk, b, logd, cum3, tot3)
    g_c = g_c3.reshape(n_chunks, D_MODEL)
    bz_ref[0] = _bonus(r, k_h, v, rk_ref[...])
    bz_ref[1] = _silu(zr)

    rr = _iota((PAIR, PAIR), 0)
    cc = _iota((PAIR, PAIR), 1)
    m_bd = (rr // HEAD_DIM) == (cc // HEAD_DIM)
    t_row = _iota((CHUNK, PAIR), 0)
    s_col = _iota((CHUNK, PAIR), 1) % HEAD_DIM
    m_sl = s_col < t_row
    m_l = s_col <= t_row
    rows = [slice(c * CHUNK, (c + 1) * CHUNK) for c in range(n_chunks)]
    lanes = [slice(hp * PAIR, (hp + 1) * PAIR) for hp in range(N_PAIRS)]
    kh16, bt16, kt16, bc16, kc16, v16 = [a.astype(BF16) for a in (khat, btil, ktil, bchk, kchk, v)]
    units = [(kh16[rw, ln], rhat[rw, ln], bt16[rw, ln], kt16[rw, ln], bc16[rw, ln], kc16[rw, ln],
              v16[rw, ln]) for ln in lanes for rw in rows]
    tr = _chunk_transfer(units, m_bd, m_sl, m_l)
    states = [jnp.where(first, 0.0, st_ref[hp]) for hp in range(N_PAIRS)]
    for c in range(n_chunks):
        for hp in range(N_PAIRS):
            y_w, y_c, p_t, q_t = tr[hp * n_chunks + c]
            ys_ref[hp, rows[c], :] = _dot_nt(y_w, states[hp]) + y_c
            states[hp] = states[hp] * g_c[c:c + 1, lanes[hp]] + _dot(states[hp], p_t) + q_t
    for hp in range(N_PAIRS):
        sp = states[hp]
        st_ref[hp] = sp
        sfin_ref[0, 2 * hp] = sp[:HEAD_DIM, :HEAD_DIM]
        sfin_ref[0, 2 * hp + 1] = sp[HEAD_DIM:, HEAD_DIM:]


def _const_spec(shape):
    nd = len(shape)
    return pl.BlockSpec(shape, lambda *_: (0,) * nd)


def _row(a):
    return a.reshape(1, -1)


def _rwkv_consts(wts):
    return [_row(wts["norm_g"]), wts["w_rw"], _row(wts["mu_rw"]), _row(wts["w0"]), wts["lora_w"],
            _row(wts["a0"]), _row(wts["k_k"]), _row(wts["k_a"]), _row(wts["r_k"])]


def _rwkv_prompt(x, wts, tm=512):
    bsz, seq, _ = x.shape
    consts = _rwkv_consts(wts) + [_row(wts["ln_w"]), _row(wts["ln_b"])]
    const_specs = [_const_spec(c.shape) for c in consts]
    const_specs[1] = pl.BlockSpec(consts[1].shape, lambda *_: (0, 0), pipeline_mode=pl.Buffered(1))
    n_tiles = seq // tm
    n_steps = bsz * n_tiles + 1
    cur = lambda g: jnp.minimum(g, n_steps - 2)
    prv = lambda g: jnp.maximum(g - 1, 0)
    tile = pl.BlockSpec((1, tm, D_MODEL), lambda g: (cur(g) // n_tiles, cur(g) % n_tiles, 0))
    kern = functools.partial(_rwkv_prompt_kernel, tm=tm, n_tiles=n_tiles, n_steps=n_steps)
    return pl.pallas_call(
        kern,
        grid=(n_steps,),
        in_specs=[tile] + const_specs,
        out_specs=[pl.BlockSpec((1, tm, D_MODEL), lambda g: (prv(g) // n_tiles, prv(g) % n_tiles, 0)),
                   tile,
                   pl.BlockSpec((1, N_HEADS, HEAD_DIM, HEAD_DIM),
                                lambda g: (cur(g) // n_tiles, 0, 0, 0))],
        out_shape=[jax.ShapeDtypeStruct((bsz, seq, D_MODEL), BF16),
                   jax.ShapeDtypeStruct((bsz, seq, D_MODEL), BF16),
                   jax.ShapeDtypeStruct((bsz, N_HEADS, HEAD_DIM, HEAD_DIM), F32)],
        scratch_shapes=[pltpu.VMEM((SUBLANES, N_RW), F32),
                        pltpu.VMEM((N_PAIRS, PAIR, PAIR), F32),
                        pltpu.VMEM((2, tm, D_MODEL), F32),
                        pltpu.VMEM((N_PAIRS, tm, PAIR), F32)],
        compiler_params=pltpu.CompilerParams(
            dimension_semantics=("arbitrary",), vmem_limit_bytes=VMEM_LIMIT),
        name="rwkv_prompt",
    )(x, *consts)


def _conv_gate(p, u, u1, u2, cw):
    conv = cw[0:1, :] * u2 + cw[1:2, :] * u1 + cw[2:3, :] * u
    return p[:, D_MODEL:2 * D_MODEL] * conv * _silu(p[:, 3 * D_MODEL:])


def _merge_out(x, h, yc, yr, wg_ref, woc_ref, wor_ref, wo_ref, fg_ref):
    gates = jnp.dot(h, wg_ref[...], preferred_element_type=F32)
    pr = jnp.dot(yr, wor_ref[...], preferred_element_type=F32)
    pc = jnp.dot(yc, woc_ref[...], preferred_element_type=F32)
    m = _sigmoid(gates[:, :D_MODEL]) * pc + _sigmoid(gates[:, D_MODEL:]) * pr
    out = jnp.dot(m.astype(BF16), wo_ref[...], preferred_element_type=F32)
    return _rms(x + out, fg_ref[...])


def _out_consts(wts):
    return [wts["w_gate"], wts["w_out_c"], wts["w_out_r"], wts["w_o"], _row(wts["final_g"])]


def _conv_out_kernel(x_ref, hb_ref, yr_ref, g_ref, wc_ref, cw_ref, wg_ref, woc_ref, wor_ref, wo_ref,
                     fg_ref, y_ref, u_ref, h_ref, carry_ref, *, tm):
    @pl.when(pl.program_id(1) == 0)
    def _():
        carry_ref[...] = jnp.zeros_like(carry_ref)

    x = x_ref[0]
    h = hb_ref[0]
    p = jnp.dot(h, wc_ref[...], preferred_element_type=F32)
    u = p[:, 2 * D_MODEL:3 * D_MODEL] * p[:, 0:D_MODEL]
    carry = carry_ref[...]
    y_c = _conv_gate(p, u, _shift_rows(u, 1, carry), _shift_rows(u, 2, carry), cw_ref[...])
    carry_ref[...] = u[tm - SUBLANES:, :]
    u_ref[0] = u[tm - SUBLANES:, :]
    h_ref[0] = _rms(x[tm - SUBLANES:, :], g_ref[...])
    y_ref[0] = _merge_out(x, h, y_c.astype(BF16), yr_ref[0], wg_ref, woc_ref, wor_ref, wo_ref,
                          fg_ref)


def _conv_out_prompt(x, hb, yr, wts, tm=512):
    bsz, seq, _ = x.shape
    consts = [_row(wts["norm_g"]), wts["w_conv"], wts["conv_w"]] + _out_consts(wts)
    single = lambda c: pl.BlockSpec(c.shape, lambda *_: (0,) * c.ndim, pipeline_mode=pl.Buffered(1))
    tile = pl.BlockSpec((1, tm, D_MODEL), lambda b, s: (b, s, 0))
    tail = pl.BlockSpec((1, SUBLANES, D_MODEL), lambda b, s: (b, 0, 0))
    return pl.pallas_call(
        functools.partial(_conv_out_kernel, tm=tm),
        grid=(bsz, seq // tm),
        in_specs=[tile, tile, tile] + [single(c) for c in consts],
        out_specs=[tile, tail, tail],
        out_shape=[jax.ShapeDtypeStruct((bsz, seq, D_MODEL), F32),
                   jax.ShapeDtypeStruct((bsz, SUBLANES, D_MODEL), F32),
                   jax.ShapeDtypeStruct((bsz, SUBLANES, D_MODEL), F32)],
        scratch_shapes=[pltpu.VMEM((SUBLANES, D_MODEL), F32)],
        compiler_params=pltpu.CompilerParams(
            dimension_semantics=("arbitrary", "arbitrary"), vmem_limit_bytes=VMEM_LIMIT),
        name="conv_out_prompt",
    )(x, hb, yr, *consts)


N_SCAN_OPS = 6
V_BLOCK = 8
V_SUB = 2


def _sample_front_kernel(x_ref, hl_ref, cb0_ref, cb1_ref, g_ref, wc_ref, cw_ref, w_ref, mu_ref,
                         w0_ref, lw_ref, a0_ref, kk_ref, ka_ref, rk_ref,
                         yc_ref, u_ref, h_ref, tr_ref, bz_ref,
                         hp_s, u1_s, u2_s):
    @pl.when(pl.program_id(0) == 0)
    def _():
        hp_s[...] = hl_ref[...]
        u1_s[...] = cb1_ref[...]
        u2_s[...] = cb0_ref[...]

    nb = x_ref.shape[1]
    hf = _rms(x_ref[0], g_ref[...])
    h = hf.astype(BF16)
    pc = jnp.dot(h, wc_ref[...], preferred_element_type=F32)
    u = pc[:, 2 * D_MODEL:3 * D_MODEL] * pc[:, 0:D_MODEL]
    u1 = u1_s[...]
    yc_ref[0] = _conv_gate(pc, u, u1, u2_s[...], cw_ref[...]).astype(BF16)
    u2_s[...] = u1
    u1_s[...] = u
    u_ref[0] = u
    h_ref[0] = hf
    both = jnp.dot(jnp.concatenate([h, hp_s[...].astype(BF16)], axis=0), w_ref[...],
                   preferred_element_type=F32)
    hp_s[...] = hf
    p, pp = both[:nb], both[nb:]
    xm = p + (pp - p) * mu_ref[...]
    r, k_h, v, kk, b, logd, zr = _rwkv_pointwise(xm, w0_ref[...], lw_ref[...], a0_ref[...],
                                                 kk_ref[...], ka_ref[...])
    for i, arr in enumerate((kk, b, jnp.exp(logd), k_h, r, v)):
        tr_ref[0, i] = arr.T
    bz_ref[0, 0] = _bonus(r, k_h, v, rk_ref[...])
    bz_ref[0, 1] = _silu(zr)


def _sample_front(xs_t, h_last, cb0, cb1, wts):
    n_tok, nb, _ = xs_t.shape
    consts = ([_row(wts["norm_g"]), wts["w_conv"], wts["conv_w"]] + _rwkv_consts(wts)[1:])
    tok = pl.BlockSpec((1, nb, D_MODEL), lambda t: (t, 0, 0))
    seq = pl.BlockSpec((nb, D_MODEL), lambda t: (0, 0))
    return pl.pallas_call(
        _sample_front_kernel,
        grid=(n_tok,),
        in_specs=[tok, seq, seq, seq] + [_const_spec(c.shape) for c in consts],
        out_specs=[tok, tok, tok,
                   pl.BlockSpec((1, N_SCAN_OPS, D_MODEL, nb), lambda t: (t, 0, 0, 0)),
                   pl.BlockSpec((1, 2, nb, D_MODEL), lambda t: (t, 0, 0, 0))],
        out_shape=[jax.ShapeDtypeStruct((n_tok, nb, D_MODEL), BF16),
                   jax.ShapeDtypeStruct((n_tok, nb, D_MODEL), F32),
                   jax.ShapeDtypeStruct((n_tok, nb, D_MODEL), F32),
                   jax.ShapeDtypeStruct((n_tok, N_SCAN_OPS, D_MODEL, nb), F32),
                   jax.ShapeDtypeStruct((n_tok, 2, nb, D_MODEL), F32)],
        scratch_shapes=[pltpu.VMEM((nb, D_MODEL), F32) for _ in range(3)],
        compiler_params=pltpu.CompilerParams(
            dimension_semantics=("arbitrary",), vmem_limit_bytes=VMEM_LIMIT),
        name="sample_front",
    )(xs_t, h_last, cb0, cb1, *consts)


def _sample_scan_kernel(s0_ref, tr_ref, s1_ref, y_ref, *, n_tok):
    def block(vb, carry):
        rows = pl.ds(pl.multiple_of(vb * V_BLOCK, V_BLOCK), V_BLOCK)
        ys = [[] for _ in range(n_tok)]
        for v0 in range(0, V_BLOCK, V_SUB):
            s = s0_ref[0, rows][v0:v0 + V_SUB]
            for t in range(n_tok):
                kk, b, d, k, r = [tr_ref[t, i][None] for i in range(5)]
                vv = tr_ref[t, 5, rows, :][v0:v0 + V_SUB, None, :]
                sa = -jnp.sum(s * kk, axis=1, keepdims=True)
                s = s * d + sa * b + vv * k
                ys[t].append(jnp.sum(s * r, axis=1))
            s1_ref[0, pl.ds(pl.multiple_of(vb * V_BLOCK, V_BLOCK) + v0, V_SUB)] = s
        for t in range(n_tok):
            y_ref[t, rows, :] = jnp.concatenate(ys[t], axis=0)
        return carry

    lax.fori_loop(0, HEAD_DIM // V_BLOCK, block, 0)


def _sample_scan(s0_t, tr):
    n_tok = tr.shape[0]
    nb = s0_t.shape[-1]
    sblk = pl.BlockSpec((1, HEAD_DIM, HEAD_DIM, nb), lambda h: (h, 0, 0, 0))
    return pl.pallas_call(
        functools.partial(_sample_scan_kernel, n_tok=n_tok),
        grid=(N_HEADS,),
        in_specs=[sblk, pl.BlockSpec((n_tok, N_SCAN_OPS, HEAD_DIM, nb), lambda h: (0, 0, h, 0))],
        out_specs=[sblk, pl.BlockSpec((n_tok, HEAD_DIM, nb), lambda h: (0, h, 0))],
        out_shape=[jax.ShapeDtypeStruct(s0_t.shape, F32),
                   jax.ShapeDtypeStruct((n_tok, D_MODEL, nb), F32)],
        compiler_params=pltpu.CompilerParams(
            dimension_semantics=("arbitrary",), vmem_limit_bytes=VMEM_LIMIT),
        name="sample_scan",
    )(s0_t, tr)


def _sample_out_kernel(x_ref, yc_ref, yt_ref, bz_ref, lnw_ref, lnb_ref,
                       g_ref, wg_ref, woc_ref, wor_ref, wo_ref, fg_ref, y_ref):
    ys = yt_ref[0].T
    y_r = (_group_norm(ys, lnw_ref[...], lnb_ref[...]) + bz_ref[0, 0]) * bz_ref[0, 1]
    x = x_ref[0]
    y_ref[0] = _merge_out(x, _rms(x, g_ref[...]).astype(BF16), yc_ref[0], y_r.astype(BF16),
                          wg_ref, woc_ref, wor_ref, wo_ref, fg_ref)


def _sample_out(xs_t, yc, y_t, bz, wts):
    n_tok, nb, _ = xs_t.shape
    consts = [_row(wts["ln_w"]), _row(wts["ln_b"]), _row(wts["norm_g"])] + _out_consts(wts)
    tok = pl.BlockSpec((1, nb, D_MODEL), lambda t: (t, 0, 0))
    return pl.pallas_call(
        _sample_out_kernel,
        grid=(n_tok,),
        in_specs=[tok, tok, pl.BlockSpec((1, D_MODEL, nb), lambda t: (t, 0, 0)),
                  pl.BlockSpec((1, 2, nb, D_MODEL), lambda t: (t, 0, 0, 0))]
        + [_const_spec(c.shape) for c in consts],
        out_specs=tok,
        out_shape=jax.ShapeDtypeStruct((n_tok, nb, D_MODEL), F32),
        compiler_params=pltpu.CompilerParams(
            dimension_semantics=("arbitrary",), vmem_limit_bytes=VMEM_LIMIT),
        name="sample_out",
    )(xs_t, yc, y_t, bz, *consts)


def _prep_weights(norm_g, w_in, conv_w, mu_shift, w0, w_up, a0, a_up, k_k, k_a, r_k, ln_w, ln_b,
                  w_out_c, w_out_r, w_o, final_g):
    d = D_MODEL
    c0 = 4 * d
    zeros = jnp.zeros((LORA, d), F32)
    lora_w = jnp.concatenate([jnp.concatenate([w_up, zeros], axis=1),
                              jnp.concatenate([zeros, a_up], axis=1)], axis=0)
    return dict(
        norm_g=norm_g, final_g=final_g, conv_w=conv_w,
        w_conv=w_in[:, :c0].astype(BF16),
        w_rw=w_in[:, c0:c0 + N_RW].astype(BF16),
        mu_rw=mu_shift,
        w_gate=w_in[:, c0 + N_RW:].astype(BF16),
        w0=w0, a0=a0, lora_w=lora_w.astype(BF16), k_k=k_k, k_a=k_a, r_k=r_k.reshape(-1),
        ln_w=ln_w, ln_b=ln_b,
        w_out_c=w_out_c.astype(BF16), w_out_r=w_out_r.astype(BF16), w_o=w_o.astype(BF16))


def _layer_prompt(x, wts):
    yr, hb, s_fin = _rwkv_prompt(x, wts)
    y, u_tail, h_tail = _conv_out_prompt(x, hb, yr, wts)
    return (y, u_tail[:, SUBLANES - 2:, :], h_tail[:, SUBLANES - 1, :], s_fin)


def _layer_sample(x, conv_buf, h_last, s0, wts):
    seq = x.shape[1]
    xs_t = jnp.swapaxes(x, 0, 1)
    yc, u, h, tr, bz = _sample_front(xs_t, h_last, conv_buf[:, 0], conv_buf[:, 1], wts)
    s1_t, y_t = _sample_scan(jnp.transpose(s0, (1, 2, 3, 0)), tr)
    y = _sample_out(xs_t, yc, y_t, bz, wts)
    return (jnp.swapaxes(y, 0, 1), jnp.swapaxes(u[seq - 2:], 0, 1), h[seq - 1],
            jnp.transpose(s1_t, (3, 0, 1, 2)))


def kernel(x_prompt, x_sample, state_conv, state_shift, state_rwkv, norm_g, w_in, conv_w, mu_shift,
           w0, w_up, a0, a_up, k_k, k_a, r_k, ln_w, ln_b, w_out_c, w_out_r, w_o, final_g):
    assert norm_g.shape[0] == 1, "single-layer step"
    wts = _prep_weights(norm_g[0], w_in[0], conv_w[0], mu_shift[0], w0[0], w_up[0], a0[0], a_up[0],
                        k_k[0], k_a[0], r_k[0], ln_w[0], ln_b[0], w_out_c[0], w_out_r[0], w_o[0],
                        final_g)
    y_p, c_p, s_p, r_p = _layer_prompt(x_prompt, wts)
    y_s, c_s, s_s, r_s = _layer_sample(x_sample, state_conv[0], state_shift[0], state_rwkv[0], wts)
    lead = lambda a: a[None]
    return (y_p, y_s, lead(c_p), lead(s_p), lead(r_p), lead(c_s), lead(s_s), lead(r_s))
```

```python
import functools
import math

import jax
import jax.numpy as jnp
from jax import lax
from jax.experimental import pallas as pl
from jax.experimental.pallas import tpu as pltpu

F32 = jnp.float32
BF16 = jnp.bfloat16

D_MODEL = 1024
HEAD_DIM = 64
N_HEADS = D_MODEL // HEAD_DIM
LORA = 64
RMS_EPS = 1e-6
GN_EPS = 64e-5
EXP_M05 = math.exp(-0.5)
LANES = 128
SUBLANES = 8
MXU_DEPTH = 256
PAIR = 2 * HEAD_DIM
N_PAIRS = D_MODEL // PAIR
CHUNK = 64
N_RW = 4 * D_MODEL + 2 * LORA
VMEM_LIMIT = 56 * 1024 * 1024


def _dot(a, b):
    return jnp.dot(a.astype(BF16), b.astype(BF16), preferred_element_type=F32)


def _dot_nt(a, b):
    return lax.dot_general(a.astype(BF16), b.astype(BF16), (((1,), (1,)), ((), ())),
                           preferred_element_type=F32)


def _dot_tn(a, b):
    return lax.dot_general(a.astype(BF16), b.astype(BF16), (((0,), (0,)), ((), ())),
                           preferred_element_type=F32)


def _split(x, n):
    parts = []
    for i in range(n):
        piece = x.astype(BF16)
        parts.append(piece)
        if i + 1 < n:
            x = x - piece.astype(F32)
    return parts


def _dot_exact_lhs(m_bf16, x, n=2):
    return sum(jnp.dot(m_bf16, part, preferred_element_type=F32) for part in _split(x, n))


def _rms(x, g):
    return x * lax.rsqrt(jnp.mean(x * x, axis=-1, keepdims=True) + RMS_EPS) * g


def _sigmoid(x):
    return 1.0 / (1.0 + jnp.exp(-x))


def _silu(x):
    return x * _sigmoid(x)


def _iota(shape, dim):
    return lax.broadcasted_iota(jnp.int32, shape, dim)


def _shift_rows(x, n, carry8):
    rolled = pltpu.roll(x, n, 0)
    head = jnp.where(_iota((SUBLANES, x.shape[1]), 0) < n, pltpu.roll(carry8, n, 0),
                     rolled[:SUBLANES])
    return jnp.concatenate([head, rolled[SUBLANES:]], axis=0)


def _seg_sum(x):
    lo = _iota((1, PAIR), 1) < HEAD_DIM
    tiles = []
    for p in range(x.shape[1] // PAIR):
        xp = x[:, p * PAIR:(p + 1) * PAIR]
        s_lo = jnp.sum(jnp.where(lo, xp, 0.0), axis=-1, keepdims=True)
        s_hi = jnp.sum(jnp.where(lo, 0.0, xp), axis=-1, keepdims=True)
        tiles.append(jnp.where(lo, s_lo, s_hi))
    return jnp.concatenate(tiles, axis=1)


def _block_cumsum(x, blk):
    rows, width = x.shape
    grp = min(rows, MXU_DEPTH)
    r = _iota((grp, grp), 0)
    c = _iota((grp, grp), 1)
    m_low = jnp.where(((r // blk) == (c // blk)) & (c <= r), 1.0, 0.0).astype(BF16)
    cum = jnp.concatenate([_dot_exact_lhs(m_low, x[i:i + grp]) for i in range(0, rows, grp)], axis=0)
    cum3 = cum.reshape(rows // blk, blk, width)
    return cum3, cum3[:, blk - 1:blk, :]


def _rwkv_pointwise(xm, w0, lora_w, a0, k_k, k_a):
    r = xm[:, 0:D_MODEL]
    k = xm[:, D_MODEL:2 * D_MODEL]
    v = xm[:, 2 * D_MODEL:3 * D_MODEL]
    da = xm[:, 3 * D_MODEL:3 * D_MODEL + 2 * LORA]
    zr = xm[:, 3 * D_MODEL + 2 * LORA:]
    lane = _iota(da.shape, 1)
    lora_in = jnp.where(lane < LORA, jnp.tanh(da), da)
    lo = _dot(lora_in, lora_w)
    w_logit = w0 + lo[:, :D_MODEL]
    logd = -EXP_M05 * _sigmoid(w_logit)
    a = _sigmoid(a0 + lo[:, D_MODEL:])
    kkr = k * k_k
    kk = kkr * jnp.minimum(lax.rsqrt(_seg_sum(kkr * kkr)), 1e12)
    k_h = k * (1.0 + (a - 1.0) * k_a)
    b = kk * a
    return r, k_h, v, kk, b, logd, zr


def _group_norm(ys, ln_w, ln_b):
    inv_n = 1.0 / HEAD_DIM
    yc = ys - _seg_sum(ys) * inv_n
    return yc * lax.rsqrt(_seg_sum(yc * yc) * inv_n + GN_EPS) * ln_w + ln_b


def _bonus(r, k_h, v, r_k):
    return _seg_sum(r * k_h * r_k) * v


def _scan_operands(r, k_h, v, kk, b, logd, cum3, tot3):
    shape = cum3.shape
    flat = lambda x: x.reshape(shape[0] * shape[1], shape[2])
    cum = flat(cum3)
    g_c3 = jnp.exp(tot3)
    enc3 = jnp.exp(-cum3)
    enc = flat(enc3)
    e_c = flat(enc3 * g_c3)
    khat = kk * jnp.exp(cum - logd)
    rhat = r * jnp.exp(cum)
    btil = b * enc
    ktil = k_h * enc
    bchk = -(b * e_c)
    kchk = k_h * e_c
    return khat, rhat, btil, ktil, bchk, kchk, g_c3


def _chunk_transfer(units, m_bd, m_sl, m_l):
    m_bd2 = jnp.concatenate([m_bd, m_bd], axis=1)
    zero = jnp.zeros((), BF16)

    def expand(x):
        x = x.astype(BF16)
        x2 = jnp.concatenate([x, x], axis=0)
        return jnp.where(m_bd if x.shape[1] == PAIR else m_bd2, x2, zero)

    gs = [_dot_nt(jnp.concatenate([kh, rh.astype(BF16)], axis=0),
                  jnp.concatenate([expand(bt), expand(kt)], axis=0))
          for kh, rh, bt, kt, _, _, _ in units]
    ps = [jnp.where(m_sl, -g[:CHUNK, :PAIR], 0.0).astype(BF16) for g in gs]
    m_as = [jnp.concatenate([jnp.where(m_sl, g[:CHUNK, PAIR:], 0.0),
                             jnp.where(m_l, g[CHUNK:, PAIR:], 0.0)], axis=0).astype(BF16)
            for g in gs]
    m_rbs = [jnp.where(m_l, -g[CHUNK:, :PAIR], 0.0).astype(BF16) for g in gs]
    zs = [_dot(m_a, expand(u[6])) for m_a, u in zip(m_as, units)]
    eye = jnp.where(m_l & ~m_sl, 1.0, 0.0)
    ts = [eye + p.astype(F32) for p in ps]
    ps = [_dot(p, expand(p)).astype(BF16) for p in ps]
    for _ in range(4):
        rs = [_dot(p, expand(jnp.concatenate([p, t.astype(BF16)], axis=1)))
              for p, t in zip(ps, ts)]
        ps = [r[:, :PAIR].astype(BF16) for r in rs]
        ts = [t + r[:, PAIR:] for t, r in zip(ts, rs)]
    ts = [t + _dot(p, expand(t)) for p, t in zip(ps, ts)]
    xs = [_dot(t, expand(jnp.concatenate([u[0], z[:CHUNK].astype(BF16)], axis=1))).astype(BF16)
          for t, u, z in zip(ts, units, zs)]
    t1s = [_dot(m_rb, expand(x)) for m_rb, x in zip(m_rbs, xs)]
    p_ts = [_dot_tn(x[:, :PAIR], u[4]) for x, u in zip(xs, units)]
    q_ts = [_dot_tn(jnp.concatenate([x[:, PAIR:], u[6]], axis=0),
                    jnp.concatenate([u[4], u[5]], axis=0)) for x, u in zip(xs, units)]
    out = []
    for u, z, t1, p_t, q_t in zip(units, zs, t1s, p_ts, q_ts):
        y_w = (u[1] + t1[:, :PAIR]).astype(BF16)
        y_c = t1[:, PAIR:] + z[CHUNK:]
        out.append((y_w, y_c, jnp.where(m_bd, p_t, 0.0).astype(BF16), jnp.where(m_bd, q_t, 0.0)))
    return out


def _rwkv_prompt_kernel(x_ref, g_ref, wa_ref, wb_ref, wc_ref, mu_ref, w0_ref, lw_ref, a0_ref, kk_ref,
                        ka_ref, rk_ref, lnw_ref, lnb_ref,
                        yr_ref, h_ref, sfin_ref,
                        carry_ref, st_ref, bz_ref, ys_ref, *, tm, n_tiles, n_steps):
    g = pl.program_id(0)
    n_chunks = tm // CHUNK
    first = g % n_tiles == 0

    @pl.when(g == 0)
    def _():
        for ref in (carry_ref, st_ref, bz_ref, ys_ref):
            ref[...] = jnp.zeros_like(ref)

    def finish_previous():
        ys = jnp.concatenate([ys_ref[hp] for hp in range(N_PAIRS)], axis=1)
        y_r = (_group_norm(ys, lnw_ref[...], lnb_ref[...]) + bz_ref[0]) * bz_ref[1]
        yr_ref[0] = y_r.astype(BF16)
        return y_r

    @pl.when(g == n_steps - 1)
    def _():
        finish_previous()

    pl.when(g < n_steps - 1)(functools.partial(
        _rwkv_prompt_step, x_ref, g_ref, (wa_ref, wb_ref, wc_ref), mu_ref, w0_ref, lw_ref, a0_ref,
        kk_ref, ka_ref, rk_ref, h_ref, sfin_ref, carry_ref, st_ref, bz_ref, ys_ref, finish_previous,
        first, tm, n_chunks))


def _rwkv_prompt_step(x_ref, g_ref, w_refs, mu_ref, w0_ref, lw_ref, a0_ref, kk_ref, ka_ref, rk_ref,
                      h_ref, sfin_ref, carry_ref, st_ref, bz_ref, ys_ref, finish_previous, first,
                      tm, n_chunks):
    h = _rms(x_ref[0], g_ref[...]).astype(BF16)
    h_ref[0] = h
    p = _project(h, w_refs)

    y_r = finish_previous()
    acc = y_r.reshape(tm // SUBLANES, SUBLANES, D_MODEL).sum(axis=0)
    acc = sum(acc[:, c * LANES:(c + 1) * LANES] for c in range(D_MODEL // LANES))
    bits = lax.bitcast_convert_type(acc[0:1, :], jnp.uint32)
    bits = lax.shift_right_logical(lax.shift_right_logical(bits, jnp.uint32(16)), jnp.uint32(16))
    after = jnp.concatenate([lax.bitcast_convert_type(bits, F32)] * (N_RW // LANES), axis=1)

    prev = _shift_rows(p, 1, jnp.where(first, 0.0, carry_ref[...]))
    carry_ref[...] = p[tm - SUBLANES:, :]
    xm = p + (prev - p) * (mu_ref[...] + after)
    r, k_h, v, kk, b, logd, zr = _rwkv_pointwise(xm, w0_ref[...], lw_ref[...], a0_ref[...],
                                                 kk_ref[...], ka_ref[...])
    cum3, tot3 = _block_cumsum(logd, CHUNK)
    khat, rhat, btil, ktil, bchk, kchk, g_c3 = _scan_operands(r, k_h, v, kk, b, logd, cum3, tot3)
    g_c = g_c3.reshape(n_chunks, D_MODEL)
    bz_ref[0] = _bonus(r, k_h, v, rk_ref[...])
    bz_ref[1] = _silu(zr)

    rr = _iota((PAIR, PAIR), 0)
    cc = _iota((PAIR, PAIR), 1)
    m_bd = (rr // HEAD_DIM) == (cc // HEAD_DIM)
    t_row = _iota((CHUNK, PAIR), 0)
    s_col = _iota((CHUNK, PAIR), 1) % HEAD_DIM
    m_sl = s_col < t_row
    m_l = s_col <= t_row
    rows = [slice(c * CHUNK, (c + 1) * CHUNK) for c in range(n_chunks)]
    lanes = [slice(hp * PAIR, (hp + 1) * PAIR) for hp in range(N_PAIRS)]
    kh16, bt16, kt16, bc16, kc16, v16 = [a.astype(BF16) for a in (khat, btil, ktil, bchk, kchk, v)]
    units = [(kh16[rw, ln], rhat[rw, ln], bt16[rw, ln], kt16[rw, ln], bc16[rw, ln], kc16[rw, ln],
              v16[rw, ln]) for ln in lanes for rw in rows]
    tr = _chunk_transfer(units, m_bd, m_sl, m_l)
    states = [jnp.where(first, 0.0, st_ref[hp]) for hp in range(N_PAIRS)]
    for c in range(n_chunks):
        for hp in range(N_PAIRS):
            y_w, y_c, p_t, q_t = tr[hp * n_chunks + c]
            ys_ref[hp, rows[c], :] = _dot_nt(y_w, states[hp]) + y_c
            states[hp] = states[hp] * g_c[c:c + 1, lanes[hp]] + _dot(states[hp], p_t) + q_t
    for hp in range(N_PAIRS):
        sp = states[hp]
        st_ref[hp] = sp
        sfin_ref[0, 2 * hp] = sp[:HEAD_DIM, :HEAD_DIM]
        sfin_ref[0, 2 * hp + 1] = sp[HEAD_DIM:, HEAD_DIM:]


def _const_spec(shape):
    nd = len(shape)
    return pl.BlockSpec(shape, lambda *_: (0,) * nd)


def _row(a):
    return a.reshape(1, -1)


def _const(a):
    return a, _const_spec(a.shape)


def _w_in_window(w_in_bf16, width, index):
    return w_in_bf16, pl.BlockSpec((D_MODEL, width), lambda *_: (0, index),
                                   pipeline_mode=pl.Buffered(1))


def _w_in_windows(w):
    return dict(conv=[_w_in_window(w, 4 * D_MODEL, 0)],
                rw=[_w_in_window(w, 2 * D_MODEL, 2), _w_in_window(w, 2 * D_MODEL, 3),
                    _w_in_window(w, LANES, 8 * D_MODEL // LANES)],
                gate=[_w_in_window(w, 2 * D_MODEL, 4), _w_in_window(w, LANES, 10 * D_MODEL // LANES)])


def _project(h, w_refs):
    return jnp.concatenate([jnp.dot(h, w[...], preferred_element_type=F32) for w in w_refs], axis=1)


def _project_gates(h, wd_ref, we_ref):
    return jnp.concatenate([jnp.dot(h, wd_ref[:, LANES:], preferred_element_type=F32),
                            jnp.dot(h, we_ref[...], preferred_element_type=F32)], axis=1)


def _rwkv_consts(wts):
    return ([_const(_row(wts["norm_g"]))] + wts["w_in"]["rw"]
            + [_const(a) for a in (_row(wts["mu_rw"]), _row(wts["w0"]), wts["lora_w"], _row(wts["a0"]),
                                   _row(wts["k_k"]), _row(wts["k_a"]), _row(wts["r_k"]))])


def _rwkv_prompt(x, wts, tm=512):
    bsz, seq, _ = x.shape
    consts = _rwkv_consts(wts) + [_const(_row(wts["ln_w"])), _const(_row(wts["ln_b"]))]
    n_tiles = seq // tm
    n_steps = bsz * n_tiles + 1
    cur = lambda g: jnp.minimum(g, n_steps - 2)
    prv = lambda g: jnp.maximum(g - 1, 0)
    tile = pl.BlockSpec((1, tm, D_MODEL), lambda g: (cur(g) // n_tiles, cur(g) % n_tiles, 0))
    kern = functools.partial(_rwkv_prompt_kernel, tm=tm, n_tiles=n_tiles, n_steps=n_steps)
    return pl.pallas_call(
        kern,
        grid=(n_steps,),
        in_specs=[tile] + [spec for _, spec in consts],
        out_specs=[pl.BlockSpec((1, tm, D_MODEL), lambda g: (prv(g) // n_tiles, prv(g) % n_tiles, 0)),
                   tile,
                   pl.BlockSpec((1, N_HEADS, HEAD_DIM, HEAD_DIM),
                                lambda g: (cur(g) // n_tiles, 0, 0, 0))],
        out_shape=[jax.ShapeDtypeStruct((bsz, seq, D_MODEL), BF16),
                   jax.ShapeDtypeStruct((bsz, seq, D_MODEL), BF16),
                   jax.ShapeDtypeStruct((bsz, N_HEADS, HEAD_DIM, HEAD_DIM), F32)],
        scratch_shapes=[pltpu.VMEM((SUBLANES, N_RW), F32),
                        pltpu.VMEM((N_PAIRS, PAIR, PAIR), F32),
                        pltpu.VMEM((2, tm, D_MODEL), F32),
                        pltpu.VMEM((N_PAIRS, tm, PAIR), F32)],
        compiler_params=pltpu.CompilerParams(
            dimension_semantics=("arbitrary",), vmem_limit_bytes=VMEM_LIMIT),
        name="rwkv_prompt",
    )(x, *[a for a, _ in consts])


def _conv_gate(p, u, u1, u2, cw):
    conv = cw[0:1, :] * u2 + cw[1:2, :] * u1 + cw[2:3, :] * u
    return p[:, D_MODEL:2 * D_MODEL] * conv * _silu(p[:, 3 * D_MODEL:])


def _merge_out(x, h, yc, yr, wgd_ref, wge_ref, woc_ref, wor_ref, wo_ref, fg_ref):
    gates = _project_gates(h, wgd_ref, wge_ref)
    pr = jnp.dot(yr, wor_ref[...], preferred_element_type=F32)
    pc = jnp.dot(yc, woc_ref[...], preferred_element_type=F32)
    m = _sigmoid(gates[:, :D_MODEL]) * pc + _sigmoid(gates[:, D_MODEL:]) * pr
    out = jnp.dot(m.astype(BF16), wo_ref[...], preferred_element_type=F32)
    return _rms(x + out, fg_ref[...])


def _single(a):
    return a, pl.BlockSpec(a.shape, lambda *_: (0,) * a.ndim, pipeline_mode=pl.Buffered(1))


def _out_consts(wts):
    return (wts["w_in"]["gate"] + [_single(wts[k]) for k in ("w_out_c", "w_out_r", "w_o")]
            + [_const(_row(wts["final_g"]))])


def _conv_out_kernel(x_ref, hb_ref, yr_ref, g_ref, wc_ref, cw_ref, wgd_ref, wge_ref, woc_ref, wor_ref,
                     wo_ref, fg_ref, y_ref, u_ref, h_ref, carry_ref, *, tm):
    @pl.when(pl.program_id(1) == 0)
    def _():
        carry_ref[...] = jnp.zeros_like(carry_ref)

    x = x_ref[0]
    h = hb_ref[0]
    p = jnp.dot(h, wc_ref[...], preferred_element_type=F32)
    u = p[:, 2 * D_MODEL:3 * D_MODEL] * p[:, 0:D_MODEL]
    carry = carry_ref[...]
    y_c = _conv_gate(p, u, _shift_rows(u, 1, carry), _shift_rows(u, 2, carry), cw_ref[...])
    carry_ref[...] = u[tm - SUBLANES:, :]
    u_ref[0] = u[tm - SUBLANES:, :]
    h_ref[0] = _rms(x[tm - SUBLANES:, :], g_ref[...])
    y_ref[0] = _merge_out(x, h, y_c.astype(BF16), yr_ref[0], wgd_ref, wge_ref, woc_ref, wor_ref,
                          wo_ref, fg_ref)


def _conv_out_prompt(x, hb, yr, wts, tm=512):
    bsz, seq, _ = x.shape
    consts = ([_const(_row(wts["norm_g"]))] + wts["w_in"]["conv"] + [_const(wts["conv_w"])]
              + _out_consts(wts))
    tile = pl.BlockSpec((1, tm, D_MODEL), lambda b, s: (b, s, 0))
    tail = pl.BlockSpec((1, SUBLANES, D_MODEL), lambda b, s: (b, 0, 0))
    return pl.pallas_call(
        functools.partial(_conv_out_kernel, tm=tm),
        grid=(bsz, seq // tm),
        in_specs=[tile, tile, tile] + [spec for _, spec in consts],
        out_specs=[tile, tail, tail],
        out_shape=[jax.ShapeDtypeStruct((bsz, seq, D_MODEL), F32),
                   jax.ShapeDtypeStruct((bsz, SUBLANES, D_MODEL), F32),
                   jax.ShapeDtypeStruct((bsz, SUBLANES, D_MODEL), F32)],
        scratch_shapes=[pltpu.VMEM((SUBLANES, D_MODEL), F32)],
        compiler_params=pltpu.CompilerParams(
            dimension_semantics=("arbitrary", "arbitrary"), vmem_limit_bytes=VMEM_LIMIT),
        name="conv_out_prompt",
    )(x, hb, yr, *[a for a, _ in consts])


N_SCAN_OPS = 6
V_BLOCK = 8
V_SUB = 2


def _sample_front_kernel(x_ref, hl_ref, cb0_ref, cb1_ref, g_ref, wc_ref, cw_ref, wa_ref, wb_ref,
                         wc2_ref, mu_ref, w0_ref, lw_ref, a0_ref, kk_ref, ka_ref, rk_ref,
                         yc_ref, u_ref, h_ref, tr_ref, bz_ref,
                         hp_s, u1_s, u2_s):
    @pl.when(pl.program_id(0) == 0)
    def _():
        hp_s[...] = hl_ref[...]
        u1_s[...] = cb1_ref[...]
        u2_s[...] = cb0_ref[...]

    nb = x_ref.shape[1]
    hf = _rms(x_ref[0], g_ref[...])
    h = hf.astype(BF16)
    pc = jnp.dot(h, wc_ref[...], preferred_element_type=F32)
    u = pc[:, 2 * D_MODEL:3 * D_MODEL] * pc[:, 0:D_MODEL]
    u1 = u1_s[...]
    yc_ref[0] = _conv_gate(pc, u, u1, u2_s[...], cw_ref[...]).astype(BF16)
    u2_s[...] = u1
    u1_s[...] = u
    u_ref[0] = u
    h_ref[0] = hf
    both = _project(jnp.concatenate([h, hp_s[...].astype(BF16)], axis=0),
                    (wa_ref, wb_ref, wc2_ref))
    hp_s[...] = hf
    p, pp = both[:nb], both[nb:]
    xm = p + (pp - p) * mu_ref[...]
    r, k_h, v, kk, b, logd, zr = _rwkv_pointwise(xm, w0_ref[...], lw_ref[...], a0_ref[...],
                                                 kk_ref[...], ka_ref[...])
    for i, arr in enumerate((kk, b, jnp.exp(logd), k_h, r, v)):
        tr_ref[0, i] = arr.T
    bz_ref[0, 0] = _bonus(r, k_h, v, rk_ref[...])
    bz_ref[0, 1] = _silu(zr)


def _sample_front(xs_t, h_last, cb0, cb1, wts):
    n_tok, nb, _ = xs_t.shape
    consts = ([_const(_row(wts["norm_g"]))] + wts["w_in"]["conv"] + [_const(wts["conv_w"])]
              + _rwkv_consts(wts)[1:])
    tok = pl.BlockSpec((1, nb, D_MODEL), lambda t: (t, 0, 0))
    seq = pl.BlockSpec((nb, D_MODEL), lambda t: (0, 0))
    return pl.pallas_call(
        _sample_front_kernel,
        grid=(n_tok,),
        in_specs=[tok, seq, seq, seq] + [spec for _, spec in consts],
        out_specs=[tok, tok, tok,
                   pl.BlockSpec((1, N_SCAN_OPS, D_MODEL, nb), lambda t: (t, 0, 0, 0)),
                   pl.BlockSpec((1, 2, nb, D_MODEL), lambda t: (t, 0, 0, 0))],
        out_shape=[jax.ShapeDtypeStruct((n_tok, nb, D_MODEL), BF16),
                   jax.ShapeDtypeStruct((n_tok, nb, D_MODEL), F32),
                   jax.ShapeDtypeStruct((n_tok, nb, D_MODEL), F32),
                   jax.ShapeDtypeStruct((n_tok, N_SCAN_OPS, D_MODEL, nb), F32),
                   jax.ShapeDtypeStruct((n_tok, 2, nb, D_MODEL), F32)],
        scratch_shapes=[pltpu.VMEM((nb, D_MODEL), F32) for _ in range(3)],
        compiler_params=pltpu.CompilerParams(
            dimension_semantics=("arbitrary",), vmem_limit_bytes=VMEM_LIMIT),
        name="sample_front",
    )(xs_t, h_last, cb0, cb1, *[a for a, _ in consts])


def _sample_scan_kernel(s0_ref, tr_ref, s1_ref, y_ref, *, n_tok):
    def block(vb, carry):
        rows = pl.ds(pl.multiple_of(vb * V_BLOCK, V_BLOCK), V_BLOCK)
        ys = [[] for _ in range(n_tok)]
        for v0 in range(0, V_BLOCK, V_SUB):
            s = s0_ref[0, rows][v0:v0 + V_SUB]
            for t in range(n_tok):
                kk, b, d, k, r = [tr_ref[t, i][None] for i in range(5)]
                vv = tr_ref[t, 5, rows, :][v0:v0 + V_SUB, None, :]
                sa = -jnp.sum(s * kk, axis=1, keepdims=True)
                s = s * d + sa * b + vv * k
                ys[t].append(jnp.sum(s * r, axis=1))
            s1_ref[0, pl.ds(pl.multiple_of(vb * V_BLOCK, V_BLOCK) + v0, V_SUB)] = s
        for t in range(n_tok):
            y_ref[t, rows, :] = jnp.concatenate(ys[t], axis=0)
        return carry

    lax.fori_loop(0, HEAD_DIM // V_BLOCK, block, 0)


def _sample_scan(s0_t, tr):
    n_tok = tr.shape[0]
    nb = s0_t.shape[-1]
    sblk = pl.BlockSpec((1, HEAD_DIM, HEAD_DIM, nb), lambda h: (h, 0, 0, 0))
    return pl.pallas_call(
        functools.partial(_sample_scan_kernel, n_tok=n_tok),
        grid=(N_HEADS,),
        in_specs=[sblk, pl.BlockSpec((n_tok, N_SCAN_OPS, HEAD_DIM, nb), lambda h: (0, 0, h, 0))],
        out_specs=[sblk, pl.BlockSpec((n_tok, HEAD_DIM, nb), lambda h: (0, h, 0))],
        out_shape=[jax.ShapeDtypeStruct(s0_t.shape, F32),
                   jax.ShapeDtypeStruct((n_tok, D_MODEL, nb), F32)],
        compiler_params=pltpu.CompilerParams(
            dimension_semantics=("arbitrary",), vmem_limit_bytes=VMEM_LIMIT),
        name="sample_scan",
    )(s0_t, tr)


def _sample_out_kernel(x_ref, yc_ref, yt_ref, bz_ref, lnw_ref, lnb_ref,
                       g_ref, wgd_ref, wge_ref, woc_ref, wor_ref, wo_ref, fg_ref, y_ref):
    ys = yt_ref[0].T
    y_r = (_group_norm(ys, lnw_ref[...], lnb_ref[...]) + bz_ref[0, 0]) * bz_ref[0, 1]
    x = x_ref[0]
    y_ref[0] = _merge_out(x, _rms(x, g_ref[...]).astype(BF16), yc_ref[0], y_r.astype(BF16),
                          wgd_ref, wge_ref, woc_ref, wor_ref, wo_ref, fg_ref)


def _sample_out(xs_t, yc, y_t, bz, wts):
    n_tok, nb, _ = xs_t.shape
    consts = ([_const(_row(wts[k])) for k in ("ln_w", "ln_b", "norm_g")] + _out_consts(wts))
    tok = pl.BlockSpec((1, nb, D_MODEL), lambda t: (t, 0, 0))
    return pl.pallas_call(
        _sample_out_kernel,
        grid=(n_tok,),
        in_specs=[tok, tok, pl.BlockSpec((1, D_MODEL, nb), lambda t: (t, 0, 0)),
                  pl.BlockSpec((1, 2, nb, D_MODEL), lambda t: (t, 0, 0, 0))]
        + [spec for _, spec in consts],
        out_specs=tok,
        out_shape=jax.ShapeDtypeStruct((n_tok, nb, D_MODEL), F32),
        compiler_params=pltpu.CompilerParams(
            dimension_semantics=("arbitrary",), vmem_limit_bytes=VMEM_LIMIT),
        name="sample_out",
    )(xs_t, yc, y_t, bz, *[a for a, _ in consts])


def _prep_weights(norm_g, w_in, conv_w, mu_shift, w0, w_up, a0, a_up, k_k, k_a, r_k, ln_w, ln_b,
                  w_out_c, w_out_r, w_o, final_g):
    zeros = jnp.zeros((LORA, D_MODEL), F32)
    lora_w = jnp.concatenate([jnp.concatenate([w_up, zeros], axis=1),
                              jnp.concatenate([zeros, a_up], axis=1)], axis=0)
    return dict(
        norm_g=norm_g, final_g=final_g, conv_w=conv_w,
        w_in=_w_in_windows(w_in.astype(BF16)),
        mu_rw=mu_shift,
        w0=w0, a0=a0, lora_w=lora_w.astype(BF16), k_k=k_k, k_a=k_a, r_k=r_k.reshape(-1),
        ln_w=ln_w, ln_b=ln_b,
        w_out_c=w_out_c.astype(BF16), w_out_r=w_out_r.astype(BF16), w_o=w_o.astype(BF16))


def _layer_prompt(x, wts):
    yr, hb, s_fin = _rwkv_prompt(x, wts)
    y, u_tail, h_tail = _conv_out_prompt(x, hb, yr, wts)
    return (y, u_tail[:, SUBLANES - 2:, :], h_tail[:, SUBLANES - 1, :], s_fin)


def _layer_sample(x, conv_buf, h_last, s0, wts):
    seq = x.shape[1]
    xs_t = jnp.swapaxes(x, 0, 1)
    yc, u, h, tr, bz = _sample_front(xs_t, h_last, conv_buf[:, 0], conv_buf[:, 1], wts)
    s1_t, y_t = _sample_scan(jnp.transpose(s0, (1, 2, 3, 0)), tr)
    y = _sample_out(xs_t, yc, y_t, bz, wts)
    return (jnp.swapaxes(y, 0, 1), jnp.swapaxes(u[seq - 2:], 0, 1), h[seq - 1],
            jnp.transpose(s1_t, (3, 0, 1, 2)))


def kernel(x_prompt, x_sample, state_conv, state_shift, state_rwkv, norm_g, w_in, conv_w, mu_shift,
           w0, w_up, a0, a_up, k_k, k_a, r_k, ln_w, ln_b, w_out_c, w_out_r, w_o, final_g):
    assert norm_g.shape[0] == 1, "single-layer step"
    wts = _prep_weights(norm_g[0], w_in[0], conv_w[0], mu_shift[0], w0[0], w_up[0], a0[0], a_up[0],
                        k_k[0], k_a[0], r_k[0], ln_w[0], ln_b[0], w_out_c[0], w_out_r[0], w_o[0],
                        final_g)
    y_p, c_p, s_p, r_p = _layer_prompt(x_prompt, wts)
    y_s, c_s, s_s, r_s = _layer_sample(x_sample, state_conv[0], state_shift[0], state_rwkv[0], wts)
    lead = lambda a: a[None]
    return (y_p, y_s, lead(c_p), lead(s_p), lead(r_p), lead(c_s), lead(s_s), lead(r_s))
```

```python
import functools
import math

import jax
import jax.numpy as jnp
from jax import lax
from jax.experimental import pallas as pl
from jax.experimental.pallas import tpu as pltpu

F32 = jnp.float32
BF16 = jnp.bfloat16

D_MODEL = 1024
HEAD_DIM = 64
N_HEADS = D_MODEL // HEAD_DIM
LORA = 64
RMS_EPS = 1e-6
GN_EPS = 64e-5
EXP_M05 = math.exp(-0.5)
LANES = 128
SUBLANES = 8
MXU_DEPTH = 256
PAIR = 2 * HEAD_DIM
N_PAIRS = D_MODEL // PAIR
CHUNK = 64
N_RW = 4 * D_MODEL + 2 * LORA
VMEM_LIMIT = 56 * 1024 * 1024


def _dot(a, b):
    return jnp.dot(a.astype(BF16), b.astype(BF16), preferred_element_type=F32)


def _dot_nt(a, b):
    return lax.dot_general(a.astype(BF16), b.astype(BF16), (((1,), (1,)), ((), ())),
                           preferred_element_type=F32)


def _dot_tn(a, b):
    return lax.dot_general(a.astype(BF16), b.astype(BF16), (((0,), (0,)), ((), ())),
                           preferred_element_type=F32)


def _split(x, n):
    parts = []
    for i in range(n):
        piece = x.astype(BF16)
        parts.append(piece)
        if i + 1 < n:
            x = x - piece.astype(F32)
    return parts


def _dot_exact_lhs(m_bf16, x, n=2):
    return sum(jnp.dot(m_bf16, part, preferred_element_type=F32) for part in _split(x, n))


def _rms(x, g):
    return x * lax.rsqrt(jnp.mean(x * x, axis=-1, keepdims=True) + RMS_EPS) * g


def _sigmoid(x):
    return 1.0 / (1.0 + jnp.exp(-x))


def _silu(x):
    return x * _sigmoid(x)


def _iota(shape, dim):
    return lax.broadcasted_iota(jnp.int32, shape, dim)


def _shift_rows(x, n, carry8):
    rolled = pltpu.roll(x, n, 0)
    head = jnp.where(_iota((SUBLANES, x.shape[1]), 0) < n, pltpu.roll(carry8, n, 0),
                     rolled[:SUBLANES])
    return jnp.concatenate([head, rolled[SUBLANES:]], axis=0)


def _seg_sum(x):
    lo = _iota((1, PAIR), 1) < HEAD_DIM
    tiles = []
    for p in range(x.shape[1] // PAIR):
        xp = x[:, p * PAIR:(p + 1) * PAIR]
        s_lo = jnp.sum(jnp.where(lo, xp, 0.0), axis=-1, keepdims=True)
        s_hi = jnp.sum(jnp.where(lo, 0.0, xp), axis=-1, keepdims=True)
        tiles.append(jnp.where(lo, s_lo, s_hi))
    return jnp.concatenate(tiles, axis=1)


def _block_cumsum(x, blk):
    rows, width = x.shape
    grp = min(rows, MXU_DEPTH)
    r = _iota((grp, grp), 0)
    c = _iota((grp, grp), 1)
    m_low = jnp.where(((r // blk) == (c // blk)) & (c <= r), 1.0, 0.0).astype(BF16)
    cum = jnp.concatenate([_dot_exact_lhs(m_low, x[i:i + grp]) for i in range(0, rows, grp)], axis=0)
    cum3 = cum.reshape(rows // blk, blk, width)
    return cum3, cum3[:, blk - 1:blk, :]


def _rwkv_pointwise(xm, w0, lora_w, a0, k_k, k_a):
    r = xm[:, 0:D_MODEL]
    k = xm[:, D_MODEL:2 * D_MODEL]
    v = xm[:, 2 * D_MODEL:3 * D_MODEL]
    da = xm[:, 3 * D_MODEL:3 * D_MODEL + 2 * LORA]
    zr = xm[:, 3 * D_MODEL + 2 * LORA:]
    lane = _iota(da.shape, 1)
    lora_in = jnp.where(lane < LORA, jnp.tanh(da), da)
    lo = _dot(lora_in, lora_w)
    w_logit = w0 + lo[:, :D_MODEL]
    logd = -EXP_M05 * _sigmoid(w_logit)
    a = _sigmoid(a0 + lo[:, D_MODEL:])
    kkr = k * k_k
    kk = kkr * jnp.minimum(lax.rsqrt(_seg_sum(kkr * kkr)), 1e12)
    k_h = k * (1.0 + (a - 1.0) * k_a)
    b = kk * a
    return r, k_h, v, kk, b, logd, zr


def _group_norm(ys, ln_w, ln_b):
    inv_n = 1.0 / HEAD_DIM
    yc = ys - _seg_sum(ys) * inv_n
    return yc * lax.rsqrt(_seg_sum(yc * yc) * inv_n + GN_EPS) * ln_w + ln_b


def _bonus(r, k_h, v, r_k):
    return _seg_sum(r * k_h * r_k) * v


def _scan_operands(r, k_h, v, kk, b, logd, cum3, tot3):
    shape = cum3.shape
    flat = lambda x: x.reshape(shape[0] * shape[1], shape[2])
    cum = flat(cum3)
    g_c3 = jnp.exp(tot3)
    enc3 = jnp.exp(-cum3)
    enc = flat(enc3)
    e_c = flat(enc3 * g_c3)
    khat = kk * jnp.exp(cum - logd)
    rhat = r * jnp.exp(cum)
    btil = b * enc
    ktil = k_h * enc
    bchk = -(b * e_c)
    kchk = k_h * e_c
    return khat, rhat, btil, ktil, bchk, kchk, g_c3


def _chunk_transfer(units, m_bd, m_sl, m_l):
    m_bd2 = jnp.concatenate([m_bd, m_bd], axis=1)
    zero = jnp.zeros((), BF16)

    def expand(x):
        x = x.astype(BF16)
        x2 = jnp.concatenate([x, x], axis=0)
        return jnp.where(m_bd if x.shape[1] == PAIR else m_bd2, x2, zero)

    gs = [_dot_nt(jnp.concatenate([kh, rh.astype(BF16)], axis=0),
                  jnp.concatenate([expand(bt), expand(kt)], axis=0))
          for kh, rh, bt, kt, _, _, _ in units]
    ps = [jnp.where(m_sl, -g[:CHUNK, :PAIR], 0.0).astype(BF16) for g in gs]
    m_as = [jnp.concatenate([jnp.where(m_sl, g[:CHUNK, PAIR:], 0.0),
                             jnp.where(m_l, g[CHUNK:, PAIR:], 0.0)], axis=0).astype(BF16)
            for g in gs]
    m_rbs = [jnp.where(m_l, -g[CHUNK:, :PAIR], 0.0).astype(BF16) for g in gs]
    zs = [_dot(m_a, expand(u[6])) for m_a, u in zip(m_as, units)]
    eye = jnp.where(m_l & ~m_sl, 1.0, 0.0)
    ts = [eye + p.astype(F32) for p in ps]
    ps = [_dot(p, expand(p)).astype(BF16) for p in ps]
    for _ in range(4):
        rs = [_dot(p, expand(jnp.concatenate([p, t.astype(BF16)], axis=1)))
              for p, t in zip(ps, ts)]
        ps = [r[:, :PAIR].astype(BF16) for r in rs]
        ts = [t + r[:, PAIR:] for t, r in zip(ts, rs)]
    ts = [t + _dot(p, expand(t)) for p, t in zip(ps, ts)]
    xs = [_dot(t, expand(jnp.concatenate([u[0], z[:CHUNK].astype(BF16)], axis=1))).astype(BF16)
          for t, u, z in zip(ts, units, zs)]
    t1s = [_dot(m_rb, expand(x)) for m_rb, x in zip(m_rbs, xs)]
    p_ts = [_dot_tn(x[:, :PAIR], u[4]) for x, u in zip(xs, units)]
    q_ts = [_dot_tn(jnp.concatenate([x[:, PAIR:], u[6]], axis=0),
                    jnp.concatenate([u[4], u[5]], axis=0)) for x, u in zip(xs, units)]
    out = []
    for u, z, t1, p_t, q_t in zip(units, zs, t1s, p_ts, q_ts):
        y_w = (u[1] + t1[:, :PAIR]).astype(BF16)
        y_c = t1[:, PAIR:] + z[CHUNK:]
        out.append((y_w, y_c, jnp.where(m_bd, p_t, 0.0).astype(BF16), jnp.where(m_bd, q_t, 0.0)))
    return out


def _rwkv_prompt_kernel(x_ref, g_ref, wa_ref, wb_ref, wc_ref, mu_ref, w0_ref, lw_ref, a0_ref, kk_ref,
                        ka_ref, rk_ref, lnw_ref, lnb_ref,
                        yr_ref, h_ref, sfin_ref,
                        carry_ref, st_ref, bz_ref, ys_ref, *, tm, n_tiles, n_steps):
    g = pl.program_id(0)
    n_chunks = tm // CHUNK
    first = g % n_tiles == 0

    @pl.when(g == 0)
    def _():
        for ref in (carry_ref, st_ref, bz_ref, ys_ref):
            ref[...] = jnp.zeros_like(ref)

    def finish_previous():
        ys = jnp.concatenate([ys_ref[hp] for hp in range(N_PAIRS)], axis=1)
        y_r = (_group_norm(ys, lnw_ref[...], lnb_ref[...]) + bz_ref[0]) * bz_ref[1]
        yr_ref[0] = y_r.astype(BF16)
        return y_r

    @pl.when(g == n_steps - 1)
    def _():
        finish_previous()

    pl.when(g < n_steps - 1)(functools.partial(
        _rwkv_prompt_step, x_ref, g_ref, (wa_ref, wb_ref, wc_ref), mu_ref, w0_ref, lw_ref, a0_ref,
        kk_ref, ka_ref, rk_ref, h_ref, sfin_ref, carry_ref, st_ref, bz_ref, ys_ref, finish_previous,
        first, tm, n_chunks))


def _rwkv_prompt_step(x_ref, g_ref, w_refs, mu_ref, w0_ref, lw_ref, a0_ref, kk_ref, ka_ref, rk_ref,
                      h_ref, sfin_ref, carry_ref, st_ref, bz_ref, ys_ref, finish_previous, first,
                      tm, n_chunks):
    h = _rms(x_ref[0], g_ref[...]).astype(BF16)
    h_ref[0] = h
    p = _project(h, w_refs)

    y_r = finish_previous()
    acc = y_r.reshape(tm // SUBLANES, SUBLANES, D_MODEL).sum(axis=0)
    acc = sum(acc[:, c * LANES:(c + 1) * LANES] for c in range(D_MODEL // LANES))
    bits = lax.bitcast_convert_type(acc[0:1, :], jnp.uint32)
    bits = lax.shift_right_logical(lax.shift_right_logical(bits, jnp.uint32(16)), jnp.uint32(16))
    after = jnp.concatenate([lax.bitcast_convert_type(bits, F32)] * (N_RW // LANES), axis=1)

    prev = _shift_rows(p, 1, jnp.where(first, 0.0, carry_ref[...]))
    carry_ref[...] = p[tm - SUBLANES:, :]
    xm = p + (prev - p) * (mu_ref[...] + after)
    r, k_h, v, kk, b, logd, zr = _rwkv_pointwise(xm, w0_ref[...], lw_ref[...], a0_ref[...],
                                                 kk_ref[...], ka_ref[...])
    cum3, tot3 = _block_cumsum(logd, CHUNK)
    khat, rhat, btil, ktil, bchk, kchk, g_c3 = _scan_operands(r, k_h, v, kk, b, logd, cum3, tot3)
    g_c = g_c3.reshape(n_chunks, D_MODEL)
    bz_ref[0] = _bonus(r, k_h, v, rk_ref[...])
    bz_ref[1] = _silu(zr)

    rr = _iota((PAIR, PAIR), 0)
    cc = _iota((PAIR, PAIR), 1)
    m_bd = (rr // HEAD_DIM) == (cc // HEAD_DIM)
    t_row = _iota((CHUNK, PAIR), 0)
    s_col = _iota((CHUNK, PAIR), 1) % HEAD_DIM
    m_sl = s_col < t_row
    m_l = s_col <= t_row
    rows = [slice(c * CHUNK, (c + 1) * CHUNK) for c in range(n_chunks)]
    lanes = [slice(hp * PAIR, (hp + 1) * PAIR) for hp in range(N_PAIRS)]
    kh16, bt16, kt16, bc16, kc16, v16 = [a.astype(BF16) for a in (khat, btil, ktil, bchk, kchk, v)]
    units = [(kh16[rw, ln], rhat[rw, ln], bt16[rw, ln], kt16[rw, ln], bc16[rw, ln], kc16[rw, ln],
              v16[rw, ln]) for ln in lanes for rw in rows]
    tr = _chunk_transfer(units, m_bd, m_sl, m_l)
    states = [jnp.where(first, 0.0, st_ref[hp]) for hp in range(N_PAIRS)]
    for c in range(n_chunks):
        for hp in range(N_PAIRS):
            y_w, y_c, p_t, q_t = tr[hp * n_chunks + c]
            ys_ref[hp, rows[c], :] = _dot_nt(y_w, states[hp]) + y_c
            states[hp] = states[hp] * g_c[c:c + 1, lanes[hp]] + _dot(states[hp], p_t) + q_t
    for hp in range(N_PAIRS):
        sp = states[hp]
        st_ref[hp] = sp
        sfin_ref[0, 2 * hp] = sp[:HEAD_DIM, :HEAD_DIM]
        sfin_ref[0, 2 * hp + 1] = sp[HEAD_DIM:, HEAD_DIM:]


def _const_spec(shape):
    nd = len(shape)
    return pl.BlockSpec(shape, lambda *_: (0,) * nd)


def _row(a):
    return a.reshape(1, -1)


def _const(a):
    return a, _const_spec(a.shape)


def _w_in_window(w_in_bf16, width, index):
    return w_in_bf16, pl.BlockSpec((D_MODEL, width), lambda *_: (0, index),
                                   pipeline_mode=pl.Buffered(1))


def _w_in_windows(w):
    return dict(conv=[_w_in_window(w, 4 * D_MODEL, 0)],
                rw=[_w_in_window(w, 2 * D_MODEL, 2), _w_in_window(w, 2 * D_MODEL, 3),
                    _w_in_window(w, LANES, 8 * D_MODEL // LANES)],
                gate=[_w_in_window(w, 2 * D_MODEL, 4), _w_in_window(w, LANES, 10 * D_MODEL // LANES)])


def _project(h, w_refs):
    return jnp.concatenate([jnp.dot(h, w[...], preferred_element_type=F32) for w in w_refs], axis=1)


def _project_gates(h, wd_ref, we_ref):
    return jnp.concatenate([jnp.dot(h, wd_ref[:, LANES:], preferred_element_type=F32),
                            jnp.dot(h, we_ref[...], preferred_element_type=F32)], axis=1)


def _rwkv_consts(wts):
    return ([_const(_row(wts["norm_g"]))] + wts["w_in"]["rw"]
            + [_const(a) for a in (_row(wts["mu_rw"]), _row(wts["w0"]), wts["lora_w"], _row(wts["a0"]),
                                   _row(wts["k_k"]), _row(wts["k_a"]), _row(wts["r_k"]))])


def _rwkv_prompt(x, wts, tm=512):
    bsz, seq, _ = x.shape
    consts = _rwkv_consts(wts) + [_const(_row(wts["ln_w"])), _const(_row(wts["ln_b"]))]
    n_tiles = seq // tm
    n_steps = bsz * n_tiles + 1
    cur = lambda g: jnp.minimum(g, n_steps - 2)
    prv = lambda g: jnp.maximum(g - 1, 0)
    tile = pl.BlockSpec((1, tm, D_MODEL), lambda g: (cur(g) // n_tiles, cur(g) % n_tiles, 0))
    kern = functools.partial(_rwkv_prompt_kernel, tm=tm, n_tiles=n_tiles, n_steps=n_steps)
    return pl.pallas_call(
        kern,
        grid=(n_steps,),
        in_specs=[tile] + [spec for _, spec in consts],
        out_specs=[pl.BlockSpec((1, tm, D_MODEL), lambda g: (prv(g) // n_tiles, prv(g) % n_tiles, 0)),
                   tile,
                   pl.BlockSpec((1, N_HEADS, HEAD_DIM, HEAD_DIM),
                                lambda g: (cur(g) // n_tiles, 0, 0, 0))],
        out_shape=[jax.ShapeDtypeStruct((bsz, seq, D_MODEL), BF16),
                   jax.ShapeDtypeStruct((bsz, seq, D_MODEL), BF16),
                   jax.ShapeDtypeStruct((bsz, N_HEADS, HEAD_DIM, HEAD_DIM), F32)],
        scratch_shapes=[pltpu.VMEM((SUBLANES, N_RW), F32),
                        pltpu.VMEM((N_PAIRS, PAIR, PAIR), F32),
                        pltpu.VMEM((2, tm, D_MODEL), F32),
                        pltpu.VMEM((N_PAIRS, tm, PAIR), F32)],
        compiler_params=pltpu.CompilerParams(
            dimension_semantics=("arbitrary",), vmem_limit_bytes=VMEM_LIMIT),
        name="rwkv_prompt",
    )(x, *[a for a, _ in consts])


def _conv_gate(p, u, u1, u2, cw):
    conv = cw[0:1, :] * u2 + cw[1:2, :] * u1 + cw[2:3, :] * u
    return p[:, D_MODEL:2 * D_MODEL] * conv * _silu(p[:, 3 * D_MODEL:])


def _merge_out(x, h, yc, yr, wgd_ref, wge_ref, woc_ref, wor_ref, wo_ref, fg_ref):
    gates = _project_gates(h, wgd_ref, wge_ref)
    pr = jnp.dot(yr, wor_ref[...], preferred_element_type=F32)
    pc = jnp.dot(yc, woc_ref[...], preferred_element_type=F32)
    m = _sigmoid(gates[:, :D_MODEL]) * pc + _sigmoid(gates[:, D_MODEL:]) * pr
    out = jnp.dot(m.astype(BF16), wo_ref[...], preferred_element_type=F32)
    return _rms(x + out, fg_ref[...])


def _single(a):
    return a, pl.BlockSpec(a.shape, lambda *_: (0,) * a.ndim, pipeline_mode=pl.Buffered(1))


def _out_consts(wts):
    return (wts["w_in"]["gate"] + [_single(wts[k]) for k in ("w_out_c", "w_out_r", "w_o")]
            + [_const(_row(wts["final_g"]))])


def _conv_out_kernel(x_ref, hb_ref, yr_ref, g_ref, wc_ref, cw_ref, wgd_ref, wge_ref, woc_ref, wor_ref,
                     wo_ref, fg_ref, y_ref, u_ref, h_ref, carry_ref, *, tm):
    @pl.when(pl.program_id(1) == 0)
    def _():
        carry_ref[...] = jnp.zeros_like(carry_ref)

    x = x_ref[0]
    h = hb_ref[0]
    p = jnp.dot(h, wc_ref[...], preferred_element_type=F32)
    u = p[:, 2 * D_MODEL:3 * D_MODEL] * p[:, 0:D_MODEL]
    carry = carry_ref[...]
    y_c = _conv_gate(p, u, _shift_rows(u, 1, carry), _shift_rows(u, 2, carry), cw_ref[...])
    carry_ref[...] = u[tm - SUBLANES:, :]
    u_ref[0] = u[tm - SUBLANES:, :]
    h_ref[0] = _rms(x[tm - SUBLANES:, :], g_ref[...])
    y_ref[0] = _merge_out(x, h, y_c.astype(BF16), yr_ref[0], wgd_ref, wge_ref, woc_ref, wor_ref,
                          wo_ref, fg_ref)


def _conv_out_prompt(x, hb, yr, wts, tm=1024):
    bsz, seq, _ = x.shape
    consts = ([_const(_row(wts["norm_g"]))] + wts["w_in"]["conv"] + [_const(wts["conv_w"])]
              + _out_consts(wts))
    tile = pl.BlockSpec((1, tm, D_MODEL), lambda b, s: (b, s, 0))
    tail = pl.BlockSpec((1, SUBLANES, D_MODEL), lambda b, s: (b, 0, 0))
    return pl.pallas_call(
        functools.partial(_conv_out_kernel, tm=tm),
        grid=(bsz, seq // tm),
        in_specs=[tile, tile, tile] + [spec for _, spec in consts],
        out_specs=[tile, tail, tail],
        out_shape=[jax.ShapeDtypeStruct((bsz, seq, D_MODEL), F32),
                   jax.ShapeDtypeStruct((bsz, SUBLANES, D_MODEL), F32),
                   jax.ShapeDtypeStruct((bsz, SUBLANES, D_MODEL), F32)],
        scratch_shapes=[pltpu.VMEM((SUBLANES, D_MODEL), F32)],
        compiler_params=pltpu.CompilerParams(
            dimension_semantics=("arbitrary", "arbitrary"), vmem_limit_bytes=VMEM_LIMIT),
        name="conv_out_prompt",
    )(x, hb, yr, *[a for a, _ in consts])


N_SCAN_OPS = 6
V_BLOCK = 8
V_SUB = 2


def _sample_front_kernel(x_ref, hl_ref, cb0_ref, cb1_ref, g_ref, wc_ref, cw_ref, wa_ref, wb_ref,
                         wc2_ref, mu_ref, w0_ref, lw_ref, a0_ref, kk_ref, ka_ref, rk_ref,
                         yc_ref, u_ref, h_ref, tr_ref, bz_ref,
                         hp_s, u1_s, u2_s):
    @pl.when(pl.program_id(0) == 0)
    def _():
        hp_s[...] = hl_ref[...]
        u1_s[...] = cb1_ref[...]
        u2_s[...] = cb0_ref[...]

    nb = x_ref.shape[1]
    hf = _rms(x_ref[0], g_ref[...])
    h = hf.astype(BF16)
    pc = jnp.dot(h, wc_ref[...], preferred_element_type=F32)
    u = pc[:, 2 * D_MODEL:3 * D_MODEL] * pc[:, 0:D_MODEL]
    u1 = u1_s[...]
    yc_ref[0] = _conv_gate(pc, u, u1, u2_s[...], cw_ref[...]).astype(BF16)
    u2_s[...] = u1
    u1_s[...] = u
    u_ref[0] = u
    h_ref[0] = hf
    both = _project(jnp.concatenate([h, hp_s[...].astype(BF16)], axis=0),
                    (wa_ref, wb_ref, wc2_ref))
    hp_s[...] = hf
    p, pp = both[:nb], both[nb:]
    xm = p + (pp - p) * mu_ref[...]
    r, k_h, v, kk, b, logd, zr = _rwkv_pointwise(xm, w0_ref[...], lw_ref[...], a0_ref[...],
                                                 kk_ref[...], ka_ref[...])
    for i, arr in enumerate((kk, b, jnp.exp(logd), k_h, r, v)):
        tr_ref[0, i] = arr.T
    bz_ref[0, 0] = _bonus(r, k_h, v, rk_ref[...])
    bz_ref[0, 1] = _silu(zr)


def _sample_front(xs_t, h_last, cb0, cb1, wts):
    n_tok, nb, _ = xs_t.shape
    consts = ([_const(_row(wts["norm_g"]))] + wts["w_in"]["conv"] + [_const(wts["conv_w"])]
              + _rwkv_consts(wts)[1:])
    tok = pl.BlockSpec((1, nb, D_MODEL), lambda t: (t, 0, 0))
    seq = pl.BlockSpec((nb, D_MODEL), lambda t: (0, 0))
    return pl.pallas_call(
        _sample_front_kernel,
        grid=(n_tok,),
        in_specs=[tok, seq, seq, seq] + [spec for _, spec in consts],
        out_specs=[tok, tok, tok,
                   pl.BlockSpec((1, N_SCAN_OPS, D_MODEL, nb), lambda t: (t, 0, 0, 0)),
                   pl.BlockSpec((1, 2, nb, D_MODEL), lambda t: (t, 0, 0, 0))],
        out_shape=[jax.ShapeDtypeStruct((n_tok, nb, D_MODEL), BF16),
                   jax.ShapeDtypeStruct((n_tok, nb, D_MODEL), F32),
                   jax.ShapeDtypeStruct((n_tok, nb, D_MODEL), F32),
                   jax.ShapeDtypeStruct((n_tok, N_SCAN_OPS, D_MODEL, nb), F32),
                   jax.ShapeDtypeStruct((n_tok, 2, nb, D_MODEL), F32)],
        scratch_shapes=[pltpu.VMEM((nb, D_MODEL), F32) for _ in range(3)],
        compiler_params=pltpu.CompilerParams(
            dimension_semantics=("arbitrary",), vmem_limit_bytes=VMEM_LIMIT),
        name="sample_front",
    )(xs_t, h_last, cb0, cb1, *[a for a, _ in consts])


def _sample_scan_kernel(s0_ref, tr_ref, s1_ref, y_ref, *, n_tok):
    def block(vb, carry):
        rows = pl.ds(pl.multiple_of(vb * V_BLOCK, V_BLOCK), V_BLOCK)
        ys = [[] for _ in range(n_tok)]
        for v0 in range(0, V_BLOCK, V_SUB):
            s = s0_ref[0, rows][v0:v0 + V_SUB]
            for t in range(n_tok):
                kk, b, d, k, r = [tr_ref[t, i][None] for i in range(5)]
                vv = tr_ref[t, 5, rows, :][v0:v0 + V_SUB, None, :]
                sa = -jnp.sum(s * kk, axis=1, keepdims=True)
                s = s * d + sa * b + vv * k
                ys[t].append(jnp.sum(s * r, axis=1))
            s1_ref[0, pl.ds(pl.multiple_of(vb * V_BLOCK, V_BLOCK) + v0, V_SUB)] = s
        for t in range(n_tok):
            y_ref[t, rows, :] = jnp.concatenate(ys[t], axis=0)
        return carry

    lax.fori_loop(0, HEAD_DIM // V_BLOCK, block, 0)


def _sample_scan(s0_t, tr):
    n_tok = tr.shape[0]
    nb = s0_t.shape[-1]
    sblk = pl.BlockSpec((1, HEAD_DIM, HEAD_DIM, nb), lambda h: (h, 0, 0, 0))
    return pl.pallas_call(
        functools.partial(_sample_scan_kernel, n_tok=n_tok),
        grid=(N_HEADS,),
        in_specs=[sblk, pl.BlockSpec((n_tok, N_SCAN_OPS, HEAD_DIM, nb), lambda h: (0, 0, h, 0))],
        out_specs=[sblk, pl.BlockSpec((n_tok, HEAD_DIM, nb), lambda h: (0, h, 0))],
        out_shape=[jax.ShapeDtypeStruct(s0_t.shape, F32),
                   jax.ShapeDtypeStruct((n_tok, D_MODEL, nb), F32)],
        compiler_params=pltpu.CompilerParams(
            dimension_semantics=("arbitrary",), vmem_limit_bytes=VMEM_LIMIT),
        name="sample_scan",
    )(s0_t, tr)


def _sample_out_kernel(x_ref, yc_ref, yt_ref, bz_ref, lnw_ref, lnb_ref,
                       g_ref, wgd_ref, wge_ref, woc_ref, wor_ref, wo_ref, fg_ref, y_ref):
    ys = yt_ref[0].T
    y_r = (_group_norm(ys, lnw_ref[...], lnb_ref[...]) + bz_ref[0, 0]) * bz_ref[0, 1]
    x = x_ref[0]
    y_ref[0] = _merge_out(x, _rms(x, g_ref[...]).astype(BF16), yc_ref[0], y_r.astype(BF16),
                          wgd_ref, wge_ref, woc_ref, wor_ref, wo_ref, fg_ref)


def _sample_out(xs_t, yc, y_t, bz, wts):
    n_tok, nb, _ = xs_t.shape
    consts = ([_const(_row(wts[k])) for k in ("ln_w", "ln_b", "norm_g")] + _out_consts(wts))
    tok = pl.BlockSpec((1, nb, D_MODEL), lambda t: (t, 0, 0))
    return pl.pallas_call(
        _sample_out_kernel,
        grid=(n_tok,),
        in_specs=[tok, tok, pl.BlockSpec((1, D_MODEL, nb), lambda t: (t, 0, 0)),
                  pl.BlockSpec((1, 2, nb, D_MODEL), lambda t: (t, 0, 0, 0))]
        + [spec for _, spec in consts],
        out_specs=tok,
        out_shape=jax.ShapeDtypeStruct((n_tok, nb, D_MODEL), F32),
        compiler_params=pltpu.CompilerParams(
            dimension_semantics=("arbitrary",), vmem_limit_bytes=VMEM_LIMIT),
        name="sample_out",
    )(xs_t, yc, y_t, bz, *[a for a, _ in consts])


def _prep_weights(norm_g, w_in, conv_w, mu_shift, w0, w_up, a0, a_up, k_k, k_a, r_k, ln_w, ln_b,
                  w_out_c, w_out_r, w_o, final_g):
    zeros = jnp.zeros((LORA, D_MODEL), F32)
    lora_w = jnp.concatenate([jnp.concatenate([w_up, zeros], axis=1),
                              jnp.concatenate([zeros, a_up], axis=1)], axis=0)
    return dict(
        norm_g=norm_g, final_g=final_g, conv_w=conv_w,
        w_in=_w_in_windows(w_in.astype(BF16)),
        mu_rw=mu_shift,
        w0=w0, a0=a0, lora_w=lora_w.astype(BF16), k_k=k_k, k_a=k_a, r_k=r_k.reshape(-1),
        ln_w=ln_w, ln_b=ln_b,
        w_out_c=w_out_c.astype(BF16), w_out_r=w_out_r.astype(BF16), w_o=w_o.astype(BF16))


def _layer_prompt(x, wts):
    yr, hb, s_fin = _rwkv_prompt(x, wts)
    y, u_tail, h_tail = _conv_out_prompt(x, hb, yr, wts)
    return (y, u_tail[:, SUBLANES - 2:, :], h_tail[:, SUBLANES - 1, :], s_fin)


def _layer_sample(x, conv_buf, h_last, s0, wts):
    seq = x.shape[1]
    xs_t = jnp.swapaxes(x, 0, 1)
    yc, u, h, tr, bz = _sample_front(xs_t, h_last, conv_buf[:, 0], conv_buf[:, 1], wts)
    s1_t, y_t = _sample_scan(jnp.transpose(s0, (1, 2, 3, 0)), tr)
    y = _sample_out(xs_t, yc, y_t, bz, wts)
    return (jnp.swapaxes(y, 0, 1), jnp.swapaxes(u[seq - 2:], 0, 1), h[seq - 1],
            jnp.transpose(s1_t, (3, 0, 1, 2)))


def kernel(x_prompt, x_sample, state_conv, state_shift, state_rwkv, norm_g, w_in, conv_w, mu_shift,
           w0, w_up, a0, a_up, k_k, k_a, r_k, ln_w, ln_b, w_out_c, w_out_r, w_o, final_g):
    assert norm_g.shape[0] == 1, "single-layer step"
    wts = _prep_weights(norm_g[0], w_in[0], conv_w[0], mu_shift[0], w0[0], w_up[0], a0[0], a_up[0],
                        k_k[0], k_a[0], r_k[0], ln_w[0], ln_b[0], w_out_c[0], w_out_r[0], w_o[0],
                        final_g)
    y_p, c_p, s_p, r_p = _layer_prompt(x_prompt, wts)
    y_s, c_s, s_s, r_s = _layer_sample(x_sample, state_conv[0], state_shift[0], state_rwkv[0], wts)
    lead = lambda a: a[None]
    return (y_p, y_s, lead(c_p), lead(s_p), lead(r_p), lead(c_s), lead(s_s), lead(r_s))
```

```python
import functools
import math

import jax
import jax.numpy as jnp
from jax import lax
from jax.experimental import pallas as pl
from jax.experimental.pallas import tpu as pltpu

F32 = jnp.float32
BF16 = jnp.bfloat16

D_MODEL = 1024
HEAD_DIM = 64
N_HEADS = D_MODEL // HEAD_DIM
LORA = 64
RMS_EPS = 1e-6
GN_EPS = 64e-5
EXP_M05 = math.exp(-0.5)
LANES = 128
SUBLANES = 8
MXU_DEPTH = 256
PAIR = 2 * HEAD_DIM
N_PAIRS = D_MODEL // PAIR
CHUNK = 64
N_RW = 4 * D_MODEL + 2 * LORA
VMEM_LIMIT = 56 * 1024 * 1024


def _dot(a, b):
    return jnp.dot(a.astype(BF16), b.astype(BF16), preferred_element_type=F32)


def _dot_nt(a, b):
    return lax.dot_general(a.astype(BF16), b.astype(BF16), (((1,), (1,)), ((), ())),
                           preferred_element_type=F32)


def _dot_tn(a, b):
    return lax.dot_general(a.astype(BF16), b.astype(BF16), (((0,), (0,)), ((), ())),
                           preferred_element_type=F32)


def _split(x, n):
    parts = []
    for i in range(n):
        piece = x.astype(BF16)
        parts.append(piece)
        if i + 1 < n:
            x = x - piece.astype(F32)
    return parts


def _dot_exact_lhs(m_bf16, x, n=2):
    return sum(jnp.dot(m_bf16, part, preferred_element_type=F32) for part in _split(x, n))


def _rms(x, g):
    return x * lax.rsqrt(jnp.mean(x * x, axis=-1, keepdims=True) + RMS_EPS) * g


def _sigmoid(x):
    return 1.0 / (1.0 + jnp.exp(-x))


def _silu(x):
    return x * _sigmoid(x)


def _iota(shape, dim):
    return lax.broadcasted_iota(jnp.int32, shape, dim)


def _shift_rows(x, n, carry8):
    rolled = pltpu.roll(x, n, 0)
    head = jnp.where(_iota((SUBLANES, x.shape[1]), 0) < n, pltpu.roll(carry8, n, 0),
                     rolled[:SUBLANES])
    return jnp.concatenate([head, rolled[SUBLANES:]], axis=0)


def _seg_sum(x):
    lo = _iota((1, PAIR), 1) < HEAD_DIM
    tiles = []
    for p in range(x.shape[1] // PAIR):
        xp = x[:, p * PAIR:(p + 1) * PAIR]
        s_lo = jnp.sum(jnp.where(lo, xp, 0.0), axis=-1, keepdims=True)
        s_hi = jnp.sum(jnp.where(lo, 0.0, xp), axis=-1, keepdims=True)
        tiles.append(jnp.where(lo, s_lo, s_hi))
    return jnp.concatenate(tiles, axis=1)


def _block_cumsum(x, blk):
    rows, width = x.shape
    grp = min(rows, MXU_DEPTH)
    r = _iota((grp, grp), 0)
    c = _iota((grp, grp), 1)
    m_low = jnp.where(((r // blk) == (c // blk)) & (c <= r), 1.0, 0.0).astype(BF16)
    cum = jnp.concatenate([_dot_exact_lhs(m_low, x[i:i + grp]) for i in range(0, rows, grp)], axis=0)
    cum3 = cum.reshape(rows // blk, blk, width)
    return cum3, cum3[:, blk - 1:blk, :]


def _rwkv_pointwise(xm, w0, lora_w, a0, k_k, k_a):
    r = xm[:, 0:D_MODEL]
    k = xm[:, D_MODEL:2 * D_MODEL]
    v = xm[:, 2 * D_MODEL:3 * D_MODEL]
    da = xm[:, 3 * D_MODEL:3 * D_MODEL + 2 * LORA]
    zr = xm[:, 3 * D_MODEL + 2 * LORA:]
    lane = _iota(da.shape, 1)
    lora_in = jnp.where(lane < LORA, jnp.tanh(da), da)
    lo = _dot(lora_in, lora_w)
    w_logit = w0 + lo[:, :D_MODEL]
    logd = -EXP_M05 * _sigmoid(w_logit)
    a = _sigmoid(a0 + lo[:, D_MODEL:])
    kkr = k * k_k
    kk = kkr * jnp.minimum(lax.rsqrt(_seg_sum(kkr * kkr)), 1e12)
    k_h = k * (1.0 + (a - 1.0) * k_a)
    b = kk * a
    return r, k_h, v, kk, b, logd, zr


def _group_norm(ys, ln_w, ln_b):
    inv_n = 1.0 / HEAD_DIM
    yc = ys - _seg_sum(ys) * inv_n
    return yc * lax.rsqrt(_seg_sum(yc * yc) * inv_n + GN_EPS) * ln_w + ln_b


def _bonus(r, k_h, v, r_k):
    return _seg_sum(r * k_h * r_k) * v


def _scan_operands(r, k_h, v, kk, b, logd, cum3, tot3):
    shape = cum3.shape
    flat = lambda x: x.reshape(shape[0] * shape[1], shape[2])
    cum = flat(cum3)
    g_c3 = jnp.exp(tot3)
    enc3 = jnp.exp(-cum3)
    enc = flat(enc3)
    e_c = flat(enc3 * g_c3)
    khat = kk * jnp.exp(cum - logd)
    rhat = r * jnp.exp(cum)
    btil = b * enc
    ktil = k_h * enc
    bchk = -(b * e_c)
    kchk = k_h * e_c
    return khat, rhat, btil, ktil, bchk, kchk, g_c3


def _chunk_transfer(units, m_bd, m_sl, m_l):
    m_bd2 = jnp.concatenate([m_bd, m_bd], axis=1)
    zero = jnp.zeros((), BF16)

    def expand(x):
        x = x.astype(BF16)
        x2 = jnp.concatenate([x, x], axis=0)
        return jnp.where(m_bd if x.shape[1] == PAIR else m_bd2, x2, zero)

    gs = [_dot_nt(jnp.concatenate([kh, rh.astype(BF16)], axis=0),
                  jnp.concatenate([expand(bt), expand(kt)], axis=0))
          for kh, rh, bt, kt, _, _, _ in units]
    ps = [jnp.where(m_sl, -g[:CHUNK, :PAIR], 0.0).astype(BF16) for g in gs]
    m_as = [jnp.concatenate([jnp.where(m_sl, g[:CHUNK, PAIR:], 0.0),
                             jnp.where(m_l, g[CHUNK:, PAIR:], 0.0)], axis=0).astype(BF16)
            for g in gs]
    m_rbs = [jnp.where(m_l, -g[CHUNK:, :PAIR], 0.0).astype(BF16) for g in gs]
    zs = [_dot(m_a, expand(u[6])) for m_a, u in zip(m_as, units)]
    eye = jnp.where(m_l & ~m_sl, 1.0, 0.0)
    ts = [eye + p.astype(F32) for p in ps]
    ps = [_dot(p, expand(p)).astype(BF16) for p in ps]
    for _ in range(4):
        rs = [_dot(p, expand(jnp.concatenate([p, t.astype(BF16)], axis=1)))
              for p, t in zip(ps, ts)]
        ps = [r[:, :PAIR].astype(BF16) for r in rs]
        ts = [t + r[:, PAIR:] for t, r in zip(ts, rs)]
    ts = [t + _dot(p, expand(t)) for p, t in zip(ps, ts)]
    xs = [_dot(t, expand(jnp.concatenate([u[0], z[:CHUNK].astype(BF16)], axis=1))).astype(BF16)
          for t, u, z in zip(ts, units, zs)]
    t1s = [_dot(m_rb, expand(x)) for m_rb, x in zip(m_rbs, xs)]
    p_ts = [_dot_tn(x[:, :PAIR], u[4]) for x, u in zip(xs, units)]
    q_ts = [_dot_tn(jnp.concatenate([x[:, PAIR:], u[6]], axis=0),
                    jnp.concatenate([u[4], u[5]], axis=0)) for x, u in zip(xs, units)]
    out = []
    for u, z, t1, p_t, q_t in zip(units, zs, t1s, p_ts, q_ts):
        y_w = (u[1] + t1[:, :PAIR]).astype(BF16)
        y_c = t1[:, PAIR:] + z[CHUNK:]
        out.append((y_w, y_c, jnp.where(m_bd, p_t, 0.0).astype(BF16), jnp.where(m_bd, q_t, 0.0)))
    return out


def _rwkv_prompt_kernel(x_ref, g_ref, wa_ref, wb_ref, wc_ref, mu_ref, w0_ref, lw_ref, a0_ref, kk_ref,
                        ka_ref, rk_ref, lnw_ref, lnb_ref,
                        yr_ref, h_ref, sfin_ref,
                        carry_ref, st_ref, bz_ref, ys_ref, *, tm, n_tiles, n_steps):
    g = pl.program_id(0)
    n_chunks = tm // CHUNK
    first = g % n_tiles == 0

    @pl.when(g == 0)
    def _():
        for ref in (carry_ref, st_ref, bz_ref, ys_ref):
            ref[...] = jnp.zeros_like(ref)

    def finish_previous():
        ys = jnp.concatenate([ys_ref[hp] for hp in range(N_PAIRS)], axis=1)
        y_r = (_group_norm(ys, lnw_ref[...], lnb_ref[...]) + bz_ref[0]) * bz_ref[1]
        yr_ref[0] = y_r.astype(BF16)
        return y_r

    @pl.when(g == n_steps - 1)
    def _():
        finish_previous()

    pl.when(g < n_steps - 1)(functools.partial(
        _rwkv_prompt_step, x_ref, g_ref, (wa_ref, wb_ref, wc_ref), mu_ref, w0_ref, lw_ref, a0_ref,
        kk_ref, ka_ref, rk_ref, h_ref, sfin_ref, carry_ref, st_ref, bz_ref, ys_ref, finish_previous,
        first, tm, n_chunks))


def _rwkv_prompt_step(x_ref, g_ref, w_refs, mu_ref, w0_ref, lw_ref, a0_ref, kk_ref, ka_ref, rk_ref,
                      h_ref, sfin_ref, carry_ref, st_ref, bz_ref, ys_ref, finish_previous, first,
                      tm, n_chunks):
    h = _rms(x_ref[0], g_ref[...]).astype(BF16)
    h_ref[0] = h
    p = _project(h, w_refs)

    y_r = finish_previous()
    acc = y_r.reshape(tm // SUBLANES, SUBLANES, D_MODEL).sum(axis=0)
    acc = sum(acc[:, c * LANES:(c + 1) * LANES] for c in range(D_MODEL // LANES))
    bits = lax.bitcast_convert_type(acc[0:1, :], jnp.uint32)
    bits = lax.shift_right_logical(lax.shift_right_logical(bits, jnp.uint32(16)), jnp.uint32(16))
    after = jnp.concatenate([lax.bitcast_convert_type(bits, F32)] * (N_RW // LANES), axis=1)

    prev = _shift_rows(p, 1, jnp.where(first, 0.0, carry_ref[...]))
    carry_ref[...] = p[tm - SUBLANES:, :]
    xm = p + (prev - p) * (mu_ref[...] + after)
    r, k_h, v, kk, b, logd, zr = _rwkv_pointwise(xm, w0_ref[...], lw_ref[...], a0_ref[...],
                                                 kk_ref[...], ka_ref[...])
    cum3, tot3 = _block_cumsum(logd, CHUNK)
    khat, rhat, btil, ktil, bchk, kchk, g_c3 = _scan_operands(r, k_h, v, kk, b, logd, cum3, tot3)
    g_c = g_c3.reshape(n_chunks, D_MODEL)
    bz_ref[0] = _bonus(r, k_h, v, rk_ref[...])
    bz_ref[1] = _silu(zr)

    rr = _iota((PAIR, PAIR), 0)
    cc = _iota((PAIR, PAIR), 1)
    m_bd = (rr // HEAD_DIM) == (cc // HEAD_DIM)
    t_row = _iota((CHUNK, PAIR), 0)
    s_col = _iota((CHUNK, PAIR), 1) % HEAD_DIM
    m_sl = s_col < t_row
    m_l = s_col <= t_row
    rows = [slice(c * CHUNK, (c + 1) * CHUNK) for c in range(n_chunks)]
    lanes = [slice(hp * PAIR, (hp + 1) * PAIR) for hp in range(N_PAIRS)]
    kh16, bt16, kt16, bc16, kc16, v16 = [a.astype(BF16) for a in (khat, btil, ktil, bchk, kchk, v)]
    units = [(kh16[rw, ln], rhat[rw, ln], bt16[rw, ln], kt16[rw, ln], bc16[rw, ln], kc16[rw, ln],
              v16[rw, ln]) for ln in lanes for rw in rows]
    tr = _chunk_transfer(units, m_bd, m_sl, m_l)
    states = [jnp.where(first, 0.0, st_ref[hp]) for hp in range(N_PAIRS)]
    for c in range(n_chunks):
        for hp in range(N_PAIRS):
            y_w, y_c, p_t, q_t = tr[hp * n_chunks + c]
            ys_ref[hp, rows[c], :] = _dot_nt(y_w, states[hp]) + y_c
            states[hp] = states[hp] * g_c[c:c + 1, lanes[hp]] + _dot(states[hp], p_t) + q_t
    for hp in range(N_PAIRS):
        sp = states[hp]
        st_ref[hp] = sp
        sfin_ref[0, 2 * hp] = sp[:HEAD_DIM, :HEAD_DIM]
        sfin_ref[0, 2 * hp + 1] = sp[HEAD_DIM:, HEAD_DIM:]


def _const_spec(shape):
    nd = len(shape)
    return pl.BlockSpec(shape, lambda *_: (0,) * nd)


def _row(a):
    return a.reshape(1, -1)


def _const(a):
    return a, _const_spec(a.shape)


def _w_in_window(w_in_bf16, width, index):
    return w_in_bf16, pl.BlockSpec((D_MODEL, width), lambda *_: (0, index),
                                   pipeline_mode=pl.Buffered(1))


def _w_in_windows(w):
    return dict(conv=[_w_in_window(w, 4 * D_MODEL, 0)],
                rw=[_w_in_window(w, 2 * D_MODEL, 2), _w_in_window(w, 2 * D_MODEL, 3),
                    _w_in_window(w, LANES, 8 * D_MODEL // LANES)],
                gate=[_w_in_window(w, 2 * D_MODEL, 4), _w_in_window(w, LANES, 10 * D_MODEL // LANES)])


def _project(h, w_refs):
    return jnp.concatenate([jnp.dot(h, w[...], preferred_element_type=F32) for w in w_refs], axis=1)


def _project_gates(h, wd_ref, we_ref):
    return jnp.concatenate([jnp.dot(h, wd_ref[:, LANES:], preferred_element_type=F32),
                            jnp.dot(h, we_ref[...], preferred_element_type=F32)], axis=1)


def _rwkv_consts(wts):
    return ([_const(_row(wts["norm_g"]))] + wts["w_in"]["rw"]
            + [_const(a) for a in (_row(wts["mu_rw"]), _row(wts["w0"]), wts["lora_w"], _row(wts["a0"]),
                                   _row(wts["k_k"]), _row(wts["k_a"]), _row(wts["r_k"]))])


def _rwkv_prompt(x, wts, tm=512):
    bsz, seq, _ = x.shape
    consts = _rwkv_consts(wts) + [_const(_row(wts["ln_w"])), _const(_row(wts["ln_b"]))]
    n_tiles = seq // tm
    n_steps = bsz * n_tiles + 1
    cur = lambda g: jnp.minimum(g, n_steps - 2)
    prv = lambda g: jnp.maximum(g - 1, 0)
    tile = pl.BlockSpec((1, tm, D_MODEL), lambda g: (cur(g) // n_tiles, cur(g) % n_tiles, 0))
    kern = functools.partial(_rwkv_prompt_kernel, tm=tm, n_tiles=n_tiles, n_steps=n_steps)
    return pl.pallas_call(
        kern,
        grid=(n_steps,),
        in_specs=[tile] + [spec for _, spec in consts],
        out_specs=[pl.BlockSpec((1, tm, D_MODEL), lambda g: (prv(g) // n_tiles, prv(g) % n_tiles, 0)),
                   tile,
                   pl.BlockSpec((1, N_HEADS, HEAD_DIM, HEAD_DIM),
                                lambda g: (cur(g) // n_tiles, 0, 0, 0))],
        out_shape=[jax.ShapeDtypeStruct((bsz, seq, D_MODEL), BF16),
                   jax.ShapeDtypeStruct((bsz, seq, D_MODEL), BF16),
                   jax.ShapeDtypeStruct((bsz, N_HEADS, HEAD_DIM, HEAD_DIM), F32)],
        scratch_shapes=[pltpu.VMEM((SUBLANES, N_RW), F32),
                        pltpu.VMEM((N_PAIRS, PAIR, PAIR), F32),
                        pltpu.VMEM((2, tm, D_MODEL), F32),
                        pltpu.VMEM((N_PAIRS, tm, PAIR), F32)],
        compiler_params=pltpu.CompilerParams(
            dimension_semantics=("arbitrary",), vmem_limit_bytes=VMEM_LIMIT),
        name="rwkv_prompt",
    )(x, *[a for a, _ in consts])


def _conv_gate(p, u, u1, u2, cw):
    conv = cw[0:1, :] * u2 + cw[1:2, :] * u1 + cw[2:3, :] * u
    return p[:, D_MODEL:2 * D_MODEL] * conv * _silu(p[:, 3 * D_MODEL:])


def _merge_out(x, h, yc, yr, wgd_ref, wge_ref, woc_ref, wor_ref, wo_ref, fg_ref):
    gates = _project_gates(h, wgd_ref, wge_ref)
    pr = jnp.dot(yr, wor_ref[...], preferred_element_type=F32)
    pc = jnp.dot(yc, woc_ref[...], preferred_element_type=F32)
    m = _sigmoid(gates[:, :D_MODEL]) * pc + _sigmoid(gates[:, D_MODEL:]) * pr
    out = jnp.dot(m.astype(BF16), wo_ref[...], preferred_element_type=F32)
    return _rms(x + out, fg_ref[...])


def _single(a):
    return a, pl.BlockSpec(a.shape, lambda *_: (0,) * a.ndim, pipeline_mode=pl.Buffered(1))


def _out_consts(wts):
    return (wts["w_in"]["gate"] + [_single(wts[k]) for k in ("w_out_c", "w_out_r", "w_o")]
            + [_const(_row(wts["final_g"]))])


def _conv_out_kernel(x_ref, hb_ref, yr_ref, g_ref, wc_ref, cw_ref, wgd_ref, wge_ref, woc_ref, wor_ref,
                     wo_ref, fg_ref, y_ref, u_ref, h_ref, carry_ref, *, tm):
    @pl.when(pl.program_id(1) == 0)
    def _():
        carry_ref[...] = jnp.zeros_like(carry_ref)

    x = x_ref[0]
    h = hb_ref[0]
    p = jnp.dot(h, wc_ref[...], preferred_element_type=F32)
    u = p[:, 2 * D_MODEL:3 * D_MODEL] * p[:, 0:D_MODEL]
    carry = carry_ref[...]
    y_c = _conv_gate(p, u, _shift_rows(u, 1, carry), _shift_rows(u, 2, carry), cw_ref[...])
    carry_ref[...] = u[tm - SUBLANES:, :]
    u_ref[0] = u[tm - SUBLANES:, :]
    h_ref[0] = _rms(x[tm - SUBLANES:, :], g_ref[...])
    y_ref[0] = _merge_out(x, h, y_c.astype(BF16), yr_ref[0], wgd_ref, wge_ref, woc_ref, wor_ref,
                          wo_ref, fg_ref)


def _conv_out_prompt(x, hb, yr, wts, tm=1024):
    bsz, seq, _ = x.shape
    consts = ([_const(_row(wts["norm_g"]))] + wts["w_in"]["conv"] + [_const(wts["conv_w"])]
              + _out_consts(wts))
    tile = pl.BlockSpec((1, tm, D_MODEL), lambda b, s: (b, s, 0))
    tail = pl.BlockSpec((1, SUBLANES, D_MODEL), lambda b, s: (b, 0, 0))
    return pl.pallas_call(
        functools.partial(_conv_out_kernel, tm=tm),
        grid=(bsz, seq // tm),
        in_specs=[tile, tile, tile] + [spec for _, spec in consts],
        out_specs=[tile, tail, tail],
        out_shape=[jax.ShapeDtypeStruct((bsz, seq, D_MODEL), F32),
                   jax.ShapeDtypeStruct((bsz, SUBLANES, D_MODEL), F32),
                   jax.ShapeDtypeStruct((bsz, SUBLANES, D_MODEL), F32)],
        scratch_shapes=[pltpu.VMEM((SUBLANES, D_MODEL), F32)],
        compiler_params=pltpu.CompilerParams(
            dimension_semantics=("arbitrary", "arbitrary"), vmem_limit_bytes=VMEM_LIMIT),
        name="conv_out_prompt",
    )(x, hb, yr, *[a for a, _ in consts])


N_SCAN_OPS = 6
V_BLOCK = 8
V_SUB = 2


def _sample_front_kernel(x_ref, hl_ref, cb0_ref, cb1_ref, g_ref, wc_ref, cw_ref, wa_ref, wb_ref,
                         wc2_ref, mu_ref, w0_ref, lw_ref, a0_ref, kk_ref, ka_ref, rk_ref,
                         yc_ref, u_ref, h_ref, tr_ref, bz_ref,
                         hp_s, u1_s, u2_s, cum_s):
    @pl.when(pl.program_id(0) == 0)
    def _():
        hp_s[...] = hl_ref[...]
        u1_s[...] = cb1_ref[...]
        u2_s[...] = cb0_ref[...]
        cum_s[...] = jnp.zeros_like(cum_s)

    nb = x_ref.shape[1]
    hf = _rms(x_ref[0], g_ref[...])
    h = hf.astype(BF16)
    pc = jnp.dot(h, wc_ref[...], preferred_element_type=F32)
    u = pc[:, 2 * D_MODEL:3 * D_MODEL] * pc[:, 0:D_MODEL]
    u1 = u1_s[...]
    yc_ref[0] = _conv_gate(pc, u, u1, u2_s[...], cw_ref[...]).astype(BF16)
    u2_s[...] = u1
    u1_s[...] = u
    u_ref[0] = u
    h_ref[0] = hf
    both = _project(jnp.concatenate([h, hp_s[...].astype(BF16)], axis=0),
                    (wa_ref, wb_ref, wc2_ref))
    hp_s[...] = hf
    p, pp = both[:nb], both[nb:]
    xm = p + (pp - p) * mu_ref[...]
    r, k_h, v, kk, b, logd, zr = _rwkv_pointwise(xm, w0_ref[...], lw_ref[...], a0_ref[...],
                                                 kk_ref[...], ka_ref[...])
    cum_prev = cum_s[...]
    cum = cum_prev + logd
    cum_s[...] = cum
    inv_d = jnp.exp(-cum)
    d_t = jnp.exp(cum)
    for i, arr in enumerate((kk * jnp.exp(cum_prev), b * inv_d, k_h * inv_d, r * d_t, v, d_t)):
        tr_ref[0, i] = arr.T
    bz_ref[0, 0] = _bonus(r, k_h, v, rk_ref[...])
    bz_ref[0, 1] = _silu(zr)


def _sample_front(xs_t, h_last, cb0, cb1, wts):
    n_tok, nb, _ = xs_t.shape
    consts = ([_const(_row(wts["norm_g"]))] + wts["w_in"]["conv"] + [_const(wts["conv_w"])]
              + _rwkv_consts(wts)[1:])
    tok = pl.BlockSpec((1, nb, D_MODEL), lambda t: (t, 0, 0))
    seq = pl.BlockSpec((nb, D_MODEL), lambda t: (0, 0))
    return pl.pallas_call(
        _sample_front_kernel,
        grid=(n_tok,),
        in_specs=[tok, seq, seq, seq] + [spec for _, spec in consts],
        out_specs=[tok, tok, tok,
                   pl.BlockSpec((1, N_SCAN_OPS, D_MODEL, nb), lambda t: (t, 0, 0, 0)),
                   pl.BlockSpec((1, 2, nb, D_MODEL), lambda t: (t, 0, 0, 0))],
        out_shape=[jax.ShapeDtypeStruct((n_tok, nb, D_MODEL), BF16),
                   jax.ShapeDtypeStruct((n_tok, nb, D_MODEL), F32),
                   jax.ShapeDtypeStruct((n_tok, nb, D_MODEL), F32),
                   jax.ShapeDtypeStruct((n_tok, N_SCAN_OPS, D_MODEL, nb), F32),
                   jax.ShapeDtypeStruct((n_tok, 2, nb, D_MODEL), F32)],
        scratch_shapes=[pltpu.VMEM((nb, D_MODEL), F32) for _ in range(4)],
        compiler_params=pltpu.CompilerParams(
            dimension_semantics=("arbitrary",), vmem_limit_bytes=VMEM_LIMIT),
        name="sample_front",
    )(xs_t, h_last, cb0, cb1, *[a for a, _ in consts])


def _sample_scan_kernel(s0_ref, tr_ref, s1_ref, y_ref, *, n_tok):
    def block(vb, carry):
        rows = pl.ds(pl.multiple_of(vb * V_BLOCK, V_BLOCK), V_BLOCK)
        ys = [[] for _ in range(n_tok)]
        for v0 in range(0, V_BLOCK, V_SUB):
            z = s0_ref[0, rows][v0:v0 + V_SUB]
            for t in range(n_tok):
                kk, b, k, r = [tr_ref[t, i][None] for i in range(4)]
                vv = tr_ref[t, 4, rows, :][v0:v0 + V_SUB, None, :]
                sa = -jnp.sum(z * kk, axis=1, keepdims=True)
                z = z + sa * b + vv * k
                ys[t].append(jnp.sum(z * r, axis=1))
            s1_ref[0, pl.ds(pl.multiple_of(vb * V_BLOCK, V_BLOCK) + v0, V_SUB)] = (
                z * tr_ref[n_tok - 1, 5][None])
        for t in range(n_tok):
            y_ref[t, rows, :] = jnp.concatenate(ys[t], axis=0)
        return carry

    lax.fori_loop(0, HEAD_DIM // V_BLOCK, block, 0)


def _sample_scan(s0_t, tr):
    n_tok = tr.shape[0]
    nb = s0_t.shape[-1]
    sblk = pl.BlockSpec((1, HEAD_DIM, HEAD_DIM, nb), lambda h: (h, 0, 0, 0))
    return pl.pallas_call(
        functools.partial(_sample_scan_kernel, n_tok=n_tok),
        grid=(N_HEADS,),
        in_specs=[sblk, pl.BlockSpec((n_tok, N_SCAN_OPS, HEAD_DIM, nb), lambda h: (0, 0, h, 0))],
        out_specs=[sblk, pl.BlockSpec((n_tok, HEAD_DIM, nb), lambda h: (0, h, 0))],
        out_shape=[jax.ShapeDtypeStruct(s0_t.shape, F32),
                   jax.ShapeDtypeStruct((n_tok, D_MODEL, nb), F32)],
        compiler_params=pltpu.CompilerParams(
            dimension_semantics=("arbitrary",), vmem_limit_bytes=VMEM_LIMIT),
        name="sample_scan",
    )(s0_t, tr)


def _sample_out_kernel(x_ref, yc_ref, yt_ref, bz_ref, lnw_ref, lnb_ref,
                       g_ref, wgd_ref, wge_ref, woc_ref, wor_ref, wo_ref, fg_ref, y_ref):
    ys = yt_ref[0].T
    y_r = (_group_norm(ys, lnw_ref[...], lnb_ref[...]) + bz_ref[0, 0]) * bz_ref[0, 1]
    x = x_ref[0]
    y_ref[0] = _merge_out(x, _rms(x, g_ref[...]).astype(BF16), yc_ref[0], y_r.astype(BF16),
                          wgd_ref, wge_ref, woc_ref, wor_ref, wo_ref, fg_ref)


def _sample_out(xs_t, yc, y_t, bz, wts):
    n_tok, nb, _ = xs_t.shape
    consts = ([_const(_row(wts[k])) for k in ("ln_w", "ln_b", "norm_g")] + _out_consts(wts))
    tok = pl.BlockSpec((1, nb, D_MODEL), lambda t: (t, 0, 0))
    return pl.pallas_call(
        _sample_out_kernel,
        grid=(n_tok,),
        in_specs=[tok, tok, pl.BlockSpec((1, D_MODEL, nb), lambda t: (t, 0, 0)),
                  pl.BlockSpec((1, 2, nb, D_MODEL), lambda t: (t, 0, 0, 0))]
        + [spec for _, spec in consts],
        out_specs=tok,
        out_shape=jax.ShapeDtypeStruct((n_tok, nb, D_MODEL), F32),
        compiler_params=pltpu.CompilerParams(
            dimension_semantics=("arbitrary",), vmem_limit_bytes=VMEM_LIMIT),
        name="sample_out",
    )(xs_t, yc, y_t, bz, *[a for a, _ in consts])


def _prep_weights(norm_g, w_in, conv_w, mu_shift, w0, w_up, a0, a_up, k_k, k_a, r_k, ln_w, ln_b,
                  w_out_c, w_out_r, w_o, final_g):
    zeros = jnp.zeros((LORA, D_MODEL), F32)
    lora_w = jnp.concatenate([jnp.concatenate([w_up, zeros], axis=1),
                              jnp.concatenate([zeros, a_up], axis=1)], axis=0)
    return dict(
        norm_g=norm_g, final_g=final_g, conv_w=conv_w,
        w_in=_w_in_windows(w_in.astype(BF16)),
        mu_rw=mu_shift,
        w0=w0, a0=a0, lora_w=lora_w.astype(BF16), k_k=k_k, k_a=k_a, r_k=r_k.reshape(-1),
        ln_w=ln_w, ln_b=ln_b,
        w_out_c=w_out_c.astype(BF16), w_out_r=w_out_r.astype(BF16), w_o=w_o.astype(BF16))


def _layer_prompt(x, wts):
    yr, hb, s_fin = _rwkv_prompt(x, wts)
    y, u_tail, h_tail = _conv_out_prompt(x, hb, yr, wts)
    return (y, u_tail[:, SUBLANES - 2:, :], h_tail[:, SUBLANES - 1, :], s_fin)


def _layer_sample(x, conv_buf, h_last, s0, wts):
    seq = x.shape[1]
    xs_t = jnp.swapaxes(x, 0, 1)
    yc, u, h, tr, bz = _sample_front(xs_t, h_last, conv_buf[:, 0], conv_buf[:, 1], wts)
    s1_t, y_t = _sample_scan(jnp.transpose(s0, (1, 2, 3, 0)), tr)
    y = _sample_out(xs_t, yc, y_t, bz, wts)
    return (jnp.swapaxes(y, 0, 1), jnp.swapaxes(u[seq - 2:], 0, 1), h[seq - 1],
            jnp.transpose(s1_t, (3, 0, 1, 2)))


def kernel(x_prompt, x_sample, state_conv, state_shift, state_rwkv, norm_g, w_in, conv_w, mu_shift,
           w0, w_up, a0, a_up, k_k, k_a, r_k, ln_w, ln_b, w_out_c, w_out_r, w_o, final_g):
    assert norm_g.shape[0] == 1, "single-layer step"
    wts = _prep_weights(norm_g[0], w_in[0], conv_w[0], mu_shift[0], w0[0], w_up[0], a0[0], a_up[0],
                        k_k[0], k_a[0], r_k[0], ln_w[0], ln_b[0], w_out_c[0], w_out_r[0], w_o[0],
                        final_g)
    y_p, c_p, s_p, r_p = _layer_prompt(x_prompt, wts)
    y_s, c_s, s_s, r_s = _layer_sample(x_sample, state_conv[0], state_shift[0], state_rwkv[0], wts)
    lead = lambda a: a[None]
    return (y_p, y_s, lead(c_p), lead(s_p), lead(r_p), lead(c_s), lead(s_s), lead(r_s))
```
